```python
import jax, jax.numpy as jnp
from jax import lax
import numpy as np

D_MODEL = 2048
BATCH = 8
SEQ = 4096
DEPTH = 4

CHUNK = 64
N_MIXERS = 2
N_CONV_LAYERS = (DEPTH + N_MIXERS - 1) // N_MIXERS
N_SSM_LAYERS = DEPTH // N_MIXERS

CONV_KERNEL = 31

SSM_EXPAND = 2
SSM_D_INNER = SSM_EXPAND * D_MODEL
SSM_HEAD_DIM = 64
SSM_N_HEADS = SSM_D_INNER // SSM_HEAD_DIM
SSM_N_GROUPS = 8
SSM_HEADS_PER_GROUP = SSM_N_HEADS // SSM_N_GROUPS
SSM_D_STATE = 128
SSM_CONV_KERNEL = 4
SSM_CONV_DIM = SSM_D_INNER + 2 * SSM_N_GROUPS * SSM_D_STATE
SSM_IN_DIM = SSM_D_INNER + SSM_CONV_DIM + SSM_N_HEADS

FFN_HIDDEN = 5632
FFN_CONV_KERNEL = 3

RMS_EPS = 1e-6
LN_EPS = 1e-5

kernel_name = "hybrid_conformer_mamba2_convffn_trunk"


def rms_norm(x, g, eps=RMS_EPS):
    xf = x.astype(jnp.float32)
    y = xf * lax.rsqrt(jnp.mean(xf * xf, axis=-1, keepdims=True) + eps)
    return (y * g.astype(jnp.float32)).astype(x.dtype)


def layer_norm(x, g, b, eps=LN_EPS):
    xf = x.astype(jnp.float32)
    mu = jnp.mean(xf, axis=-1, keepdims=True)
    xc = xf - mu
    var = jnp.mean(xc * xc, axis=-1, keepdims=True)
    y = xc * lax.rsqrt(var + eps) * g.astype(jnp.float32) + b.astype(jnp.float32)
    return y.astype(x.dtype)


def causal_depthwise_conv(x, w, b):
    k, c = w.shape
    xp = jnp.pad(x, ((0, 0), (k - 1, 0), (0, 0)))
    y = lax.conv_general_dilated(
        xp, w[:, None, :].astype(x.dtype), window_strides=(1,), padding="VALID",
        dimension_numbers=("NWC", "WIO", "NWC"), feature_group_count=c)
    return y + b.astype(x.dtype)


def conformer_conv_module(h, w_in, b_in, w_dw, b_dw, ln_g, ln_b, w_out, b_out):
    u = h @ w_in + b_in
    a, gate = jnp.split(u, 2, axis=-1)
    v = a * jax.nn.sigmoid(gate)
    v = causal_depthwise_conv(v, w_dw, b_dw)
    v = jax.nn.silu(layer_norm(v, ln_g, ln_b))
    return v @ w_out + b_out


def segsum(a):
    q = a.shape[-1]
    a_rep = jnp.broadcast_to(a[..., :, None], a.shape + (q,))
    strict = jnp.tril(jnp.ones((q, q), dtype=bool), -1)
    ss = jnp.cumsum(jnp.where(strict, a_rep, 0.0), axis=-2)
    return jnp.where(jnp.tril(jnp.ones((q, q), dtype=bool)), ss, -jnp.inf)


def ssd_chunked(x, dt, a_neg, bm, cm):
    bsz, seq, _, _ = x.shape
    nc = seq // CHUNK
    g, r, p, n = SSM_N_GROUPS, SSM_HEADS_PER_GROUP, SSM_HEAD_DIM, SSM_D_STATE
    xd = (x * dt[..., None]).reshape(bsz, nc, CHUNK, g, r, p)
    a = jnp.moveaxis((dt * a_neg).reshape(bsz, nc, CHUNK, g, r), 2, -1)
    bc = bm.reshape(bsz, nc, CHUNK, g, n)
    cc = cm.reshape(bsz, nc, CHUNK, g, n)
    a_cs = jnp.cumsum(a, axis=-1)
    decay_in = jnp.exp(segsum(a))
    cb = jnp.einsum("bclgn,bcsgn->bcgls", cc, bc)
    y_diag = jnp.einsum("bcgls,bcgrls,bcsgrp->bclgrp", cb, decay_in, xd)
    decay_to_end = jnp.exp(a_cs[..., -1:] - a_cs)
    states = jnp.einsum("bcsgn,bcgrs,bcsgrp->bcgrpn", bc, decay_to_end, xd)
    chunk_decay = jnp.exp(a_cs[..., -1])

    def step(carry, inp):
        st, dec = inp
        return carry * dec[..., None, None] + st, carry

    init = jnp.zeros((bsz, g, r, p, n), dtype=x.dtype)
    _, prev = lax.scan(step, init, (jnp.moveaxis(states, 1, 0), jnp.moveaxis(chunk_decay, 1, 0)))
    prev = jnp.moveaxis(prev, 0, 1)
    y_off = jnp.einsum("bclgn,bcgrpn,bcgrl->bclgrp", cc, prev, jnp.exp(a_cs))
    return (y_diag + y_off).reshape(bsz, seq, g * r, p)


def gated_group_rms_norm(y, z, g):
    yf = (y * jax.nn.silu(z)).astype(jnp.float32)
    shp = yf.shape
    yg = yf.reshape(shp[:-1] + (SSM_N_GROUPS, shp[-1] // SSM_N_GROUPS))
    yg = yg * lax.rsqrt(jnp.mean(yg * yg, axis=-1, keepdims=True) + RMS_EPS)
    return (yg.reshape(shp) * g.astype(jnp.float32)).astype(z.dtype)


def mamba2_mixer(h, w_in, w_conv, b_conv, dt_bias, a_log, d_skip, norm_g, w_out):
    bsz, seq, _ = h.shape
    zxbcdt = h @ w_in
    z, xbc, dt = jnp.split(zxbcdt, [SSM_D_INNER, SSM_D_INNER + SSM_CONV_DIM], axis=-1)
    xbc = jax.nn.silu(causal_depthwise_conv(xbc, w_conv, b_conv))
    gn = SSM_N_GROUPS * SSM_D_STATE
    xs, bm, cm = jnp.split(xbc, [SSM_D_INNER, SSM_D_INNER + gn], axis=-1)
    xs = xs.reshape(bsz, seq, SSM_N_HEADS, SSM_HEAD_DIM).astype(jnp.float32)
    bm = bm.reshape(bsz, seq, SSM_N_GROUPS, SSM_D_STATE).astype(jnp.float32)
    cm = cm.reshape(bsz, seq, SSM_N_GROUPS, SSM_D_STATE).astype(jnp.float32)
    dt = jax.nn.softplus(dt.astype(jnp.float32) + dt_bias.astype(jnp.float32))
    a_neg = -jnp.exp(a_log.astype(jnp.float32))
    y = ssd_chunked(xs, dt, a_neg, bm, cm)
    y = y + d_skip.astype(jnp.float32)[:, None] * xs
    y = y.reshape(bsz, seq, SSM_D_INNER).astype(h.dtype)
    return gated_group_rms_norm(y, z, norm_g) @ w_out


def conv_ffn(h, w_up, w_dw, b_dw, w_down):
    u = causal_depthwise_conv(h @ w_up, w_dw, b_dw)
    gate, val = jnp.split(u, 2, axis=-1)
    return (jax.nn.silu(gate) * val) @ w_down


def _fwd_setup_inputs(seed: int = 0) -> dict:
    key = jax.random.key(seed)
    ks = jax.random.split(key, 32)
    d, f = D_MODEL, FFN_HIDDEN
    nc, ns = N_CONV_LAYERS, N_SSM_LAYERS

    def nrm(k, shape, scale):
        return jax.random.normal(k, shape, dtype=jnp.float32) * scale

    dt0 = jnp.exp(jax.random.uniform(ks[16], (ns, SSM_N_HEADS), minval=math_log(1e-3), maxval=math_log(1e-1)))
    return {
        "x": nrm(ks[0], (BATCH, SEQ, d), 1.0),
        "norm_mix_g": 1.0 + nrm(ks[1], (DEPTH, d), 0.05),
        "norm_ffn_g": 1.0 + nrm(ks[2], (DEPTH, d), 0.05),
        "norm_final_g": 1.0 + nrm(ks[3], (d,), 0.05),
        "cv_w_in": nrm(ks[4], (nc, d, 2 * d), d ** -0.5),
        "cv_b_in": nrm(ks[5], (nc, 2 * d), 0.02),
        "cv_w_dw": nrm(ks[6], (nc, CONV_KERNEL, d), CONV_KERNEL ** -0.5),
        "cv_b_dw": nrm(ks[7], (nc, d), 0.02),
        "cv_ln_g": 1.0 + nrm(ks[8], (nc, d), 0.05),
        "cv_ln_b": nrm(ks[9], (nc, d), 0.02),
        "cv_w_out": nrm(ks[10], (nc, d, d), d ** -0.5),
        "cv_b_out": nrm(ks[11], (nc, d), 0.02),
        "ssm_w_in": nrm(ks[12], (ns, d, SSM_IN_DIM), d ** -0.5),
        "ssm_w_conv": nrm(ks[13], (ns, SSM_CONV_KERNEL, SSM_CONV_DIM), SSM_CONV_KERNEL ** -0.5),
        "ssm_b_conv": nrm(ks[14], (ns, SSM_CONV_DIM), 0.02),
        "ssm_dt_bias": dt0 + jnp.log(-jnp.expm1(-dt0)),
        "ssm_a_log": jnp.log(jax.random.uniform(ks[17], (ns, SSM_N_HEADS), minval=1.0, maxval=16.0)),
        "ssm_d": 1.0 + nrm(ks[18], (ns, SSM_N_HEADS), 0.1),
        "ssm_norm_g": 1.0 + nrm(ks[19], (ns, SSM_D_INNER), 0.05),
        "ssm_w_out": nrm(ks[20], (ns, SSM_D_INNER, d), SSM_D_INNER ** -0.5),
        "ffn_w_up": nrm(ks[21], (DEPTH, d, 2 * f), d ** -0.5),
        "ffn_w_dw": nrm(ks[22], (DEPTH, FFN_CONV_KERNEL, 2 * f), FFN_CONV_KERNEL ** -0.5),
        "ffn_b_dw": nrm(ks[23], (DEPTH, 2 * f), 0.02),
        "ffn_w_down": nrm(ks[24], (DEPTH, f, d), f ** -0.5),
    }


def math_log(v):
    return float(np.log(v))


def _fwd_reference(x, norm_mix_g, norm_ffn_g, norm_final_g,
              cv_w_in, cv_b_in, cv_w_dw, cv_b_dw, cv_ln_g, cv_ln_b, cv_w_out, cv_b_out,
              ssm_w_in, ssm_w_conv, ssm_b_conv, ssm_dt_bias, ssm_a_log, ssm_d, ssm_norm_g, ssm_w_out,
              ffn_w_up, ffn_w_dw, ffn_b_dw, ffn_w_down):
    for i in range(DEPTH):
        h = rms_norm(x, norm_mix_g[i])
        j = i // N_MIXERS
        if i % N_MIXERS == 0:
            x = x + conformer_conv_module(h, cv_w_in[j], cv_b_in[j], cv_w_dw[j], cv_b_dw[j],
                                          cv_ln_g[j], cv_ln_b[j], cv_w_out[j], cv_b_out[j])
        else:
            x = x + mamba2_mixer(h, ssm_w_in[j], ssm_w_conv[j], ssm_b_conv[j], ssm_dt_bias[j],
                                 ssm_a_log[j], ssm_d[j], ssm_norm_g[j], ssm_w_out[j])
        x = x + conv_ffn(rms_norm(x, norm_ffn_g[i]), ffn_w_up[i], ffn_w_dw[i], ffn_b_dw[i], ffn_w_down[i])
    return rms_norm(x, norm_final_g)


import jax as _jax
import jax.numpy as _jnp

TWIN_FORMAT = 'train_step'
FWD_PARAMS = ['x', 'norm_mix_g', 'norm_ffn_g', 'norm_final_g', 'cv_w_in', 'cv_b_in', 'cv_w_dw', 'cv_b_dw', 'cv_ln_g', 'cv_ln_b', 'cv_w_out', 'cv_b_out', 'ssm_w_in', 'ssm_w_conv', 'ssm_b_conv', 'ssm_dt_bias', 'ssm_a_log', 'ssm_d', 'ssm_norm_g', 'ssm_w_out', 'ffn_w_up', 'ffn_w_dw', 'ffn_b_dw', 'ffn_w_down']
TWIN_WEIGHTS = ['norm_mix_g', 'norm_ffn_g', 'norm_final_g', 'cv_w_in', 'cv_b_in', 'cv_w_dw', 'cv_b_dw', 'cv_ln_g', 'cv_ln_b', 'cv_w_out', 'cv_b_out', 'ssm_w_in', 'ssm_w_conv', 'ssm_b_conv', 'ssm_dt_bias', 'ssm_a_log', 'ssm_d', 'ssm_norm_g', 'ssm_w_out', 'ffn_w_up', 'ffn_w_dw', 'ffn_b_dw', 'ffn_w_down']
TWIN_DIFF_INPUT = 'x'
TWIN_INPUTS = ['x', 'norm_mix_g', 'norm_ffn_g', 'norm_final_g', 'cv_w_in', 'cv_b_in', 'cv_w_dw', 'cv_b_dw', 'cv_ln_g', 'cv_ln_b', 'cv_w_out', 'cv_b_out', 'ssm_w_in', 'ssm_w_conv', 'ssm_b_conv', 'ssm_dt_bias', 'ssm_a_log', 'ssm_d', 'ssm_norm_g', 'ssm_w_out', 'ffn_w_up', 'ffn_w_dw', 'ffn_b_dw', 'ffn_w_down', 'loss_target', 'm_norm_mix_g', 'm_norm_ffn_g', 'm_norm_final_g', 'm_cv_w_in', 'm_cv_b_in', 'm_cv_w_dw', 'm_cv_b_dw', 'm_cv_ln_g', 'm_cv_ln_b', 'm_cv_w_out', 'm_cv_b_out', 'm_ssm_w_in', 'm_ssm_w_conv', 'm_ssm_b_conv', 'm_ssm_dt_bias', 'm_ssm_a_log', 'm_ssm_d', 'm_ssm_norm_g', 'm_ssm_w_out', 'm_ffn_w_up', 'm_ffn_w_dw', 'm_ffn_b_dw', 'm_ffn_w_down', 'v_norm_mix_g', 'v_norm_ffn_g', 'v_norm_final_g', 'v_cv_w_in', 'v_cv_b_in', 'v_cv_w_dw', 'v_cv_b_dw', 'v_cv_ln_g', 'v_cv_ln_b', 'v_cv_w_out', 'v_cv_b_out', 'v_ssm_w_in', 'v_ssm_w_conv', 'v_ssm_b_conv', 'v_ssm_dt_bias', 'v_ssm_a_log', 'v_ssm_d', 'v_ssm_norm_g', 'v_ssm_w_out', 'v_ffn_w_up', 'v_ffn_w_dw', 'v_ffn_b_dw', 'v_ffn_w_down']
TWIN_OUTPUTS = ['loss', 'grad_x', 'grad_norm_mix_g', 'grad_norm_ffn_g', 'grad_norm_final_g', 'grad_cv_w_in', 'grad_cv_b_in', 'grad_cv_w_dw', 'grad_cv_b_dw', 'grad_cv_ln_g', 'grad_cv_ln_b', 'grad_cv_w_out', 'grad_cv_b_out', 'grad_ssm_w_in', 'grad_ssm_w_conv', 'grad_ssm_b_conv', 'grad_ssm_dt_bias', 'grad_ssm_a_log', 'grad_ssm_d', 'grad_ssm_norm_g', 'grad_ssm_w_out', 'grad_ffn_w_up', 'grad_ffn_w_dw', 'grad_ffn_b_dw', 'grad_ffn_w_down', 'delta_norm_mix_g', 'delta_norm_ffn_g', 'delta_norm_final_g', 'delta_cv_w_in', 'delta_cv_b_in', 'delta_cv_w_dw', 'delta_cv_b_dw', 'delta_cv_ln_g', 'delta_cv_ln_b', 'delta_cv_w_out', 'delta_cv_b_out', 'delta_ssm_w_in', 'delta_ssm_w_conv', 'delta_ssm_b_conv', 'delta_ssm_dt_bias', 'delta_ssm_a_log', 'delta_ssm_d', 'delta_ssm_norm_g', 'delta_ssm_w_out', 'delta_ffn_w_up', 'delta_ffn_w_dw', 'delta_ffn_b_dw', 'delta_ffn_w_down', 'new_m_norm_mix_g', 'new_m_norm_ffn_g', 'new_m_norm_final_g', 'new_m_cv_w_in', 'new_m_cv_b_in', 'new_m_cv_w_dw', 'new_m_cv_b_dw', 'new_m_cv_ln_g', 'new_m_cv_ln_b', 'new_m_cv_w_out', 'new_m_cv_b_out', 'new_m_ssm_w_in', 'new_m_ssm_w_conv', 'new_m_ssm_b_conv', 'new_m_ssm_dt_bias', 'new_m_ssm_a_log', 'new_m_ssm_d', 'new_m_ssm_norm_g', 'new_m_ssm_w_out', 'new_m_ffn_w_up', 'new_m_ffn_w_dw', 'new_m_ffn_b_dw', 'new_m_ffn_w_down', 'new_v_norm_mix_g', 'new_v_norm_ffn_g', 'new_v_norm_final_g', 'new_v_cv_w_in', 'new_v_cv_b_in', 'new_v_cv_w_dw', 'new_v_cv_b_dw', 'new_v_cv_ln_g', 'new_v_cv_ln_b', 'new_v_cv_w_out', 'new_v_cv_b_out', 'new_v_ssm_w_in', 'new_v_ssm_w_conv', 'new_v_ssm_b_conv', 'new_v_ssm_dt_bias', 'new_v_ssm_a_log', 'new_v_ssm_d', 'new_v_ssm_norm_g', 'new_v_ssm_w_out', 'new_v_ffn_w_up', 'new_v_ffn_w_dw', 'new_v_ffn_b_dw', 'new_v_ffn_w_down']
TWIN_LEAF_KINDS = {'loss': 'loss', 'grad_x': 'grad_x', 'grad_norm_mix_g': 'grad_w', 'grad_norm_ffn_g': 'grad_w', 'grad_norm_final_g': 'grad_w', 'grad_cv_w_in': 'grad_w', 'grad_cv_b_in': 'grad_w', 'grad_cv_w_dw': 'grad_w', 'grad_cv_b_dw': 'grad_w', 'grad_cv_ln_g': 'grad_w', 'grad_cv_ln_b': 'grad_w', 'grad_cv_w_out': 'grad_w', 'grad_cv_b_out': 'grad_w', 'grad_ssm_w_in': 'grad_w', 'grad_ssm_w_conv': 'grad_w', 'grad_ssm_b_conv': 'grad_w', 'grad_ssm_dt_bias': 'grad_w', 'grad_ssm_a_log': 'grad_w', 'grad_ssm_d': 'grad_w', 'grad_ssm_norm_g': 'grad_w', 'grad_ssm_w_out': 'grad_w', 'grad_ffn_w_up': 'grad_w', 'grad_ffn_w_dw': 'grad_w', 'grad_ffn_b_dw': 'grad_w', 'grad_ffn_w_down': 'grad_w', 'delta_norm_mix_g': 'delta_w', 'delta_norm_ffn_g': 'delta_w', 'delta_norm_final_g': 'delta_w', 'delta_cv_w_in': 'delta_w', 'delta_cv_b_in': 'delta_w', 'delta_cv_w_dw': 'delta_w', 'delta_cv_b_dw': 'delta_w', 'delta_cv_ln_g': 'delta_w', 'delta_cv_ln_b': 'delta_w', 'delta_cv_w_out': 'delta_w', 'delta_cv_b_out': 'delta_w', 'delta_ssm_w_in': 'delta_w', 'delta_ssm_w_conv': 'delta_w', 'delta_ssm_b_conv': 'delta_w', 'delta_ssm_dt_bias': 'delta_w', 'delta_ssm_a_log': 'delta_w', 'delta_ssm_d': 'delta_w', 'delta_ssm_norm_g': 'delta_w', 'delta_ssm_w_out': 'delta_w', 'delta_ffn_w_up': 'delta_w', 'delta_ffn_w_dw': 'delta_w', 'delta_ffn_b_dw': 'delta_w', 'delta_ffn_w_down': 'delta_w', 'new_m_norm_mix_g': 'new_m', 'new_m_norm_ffn_g': 'new_m', 'new_m_norm_final_g': 'new_m', 'new_m_cv_w_in': 'new_m', 'new_m_cv_b_in': 'new_m', 'new_m_cv_w_dw': 'new_m', 'new_m_cv_b_dw': 'new_m', 'new_m_cv_ln_g': 'new_m', 'new_m_cv_ln_b': 'new_m', 'new_m_cv_w_out': 'new_m', 'new_m_cv_b_out': 'new_m', 'new_m_ssm_w_in': 'new_m', 'new_m_ssm_w_conv': 'new_m', 'new_m_ssm_b_conv': 'new_m', 'new_m_ssm_dt_bias': 'new_m', 'new_m_ssm_a_log': 'new_m', 'new_m_ssm_d': 'new_m', 'new_m_ssm_norm_g': 'new_m', 'new_m_ssm_w_out': 'new_m', 'new_m_ffn_w_up': 'new_m', 'new_m_ffn_w_dw': 'new_m', 'new_m_ffn_b_dw': 'new_m', 'new_m_ffn_w_down': 'new_m', 'new_v_norm_mix_g': 'new_v', 'new_v_norm_ffn_g': 'new_v', 'new_v_norm_final_g': 'new_v', 'new_v_cv_w_in': 'new_v', 'new_v_cv_b_in': 'new_v', 'new_v_cv_w_dw': 'new_v', 'new_v_cv_b_dw': 'new_v', 'new_v_cv_ln_g': 'new_v', 'new_v_cv_ln_b': 'new_v', 'new_v_cv_w_out': 'new_v', 'new_v_cv_b_out': 'new_v', 'new_v_ssm_w_in': 'new_v', 'new_v_ssm_w_conv': 'new_v', 'new_v_ssm_b_conv': 'new_v', 'new_v_ssm_dt_bias': 'new_v', 'new_v_ssm_a_log': 'new_v', 'new_v_ssm_d': 'new_v', 'new_v_ssm_norm_g': 'new_v', 'new_v_ssm_w_out': 'new_v', 'new_v_ffn_w_up': 'new_v', 'new_v_ffn_w_dw': 'new_v', 'new_v_ffn_b_dw': 'new_v', 'new_v_ffn_w_down': 'new_v'}


def _forward(args):
    return _fwd_reference(*[args[k] for k in FWD_PARAMS])


def _output_shape():
    def fwd():
        inp = _fwd_setup_inputs(0)
        return _fwd_reference(*[inp[k] for k in FWD_PARAMS])
    out = _jax.eval_shape(fwd)
    return out.shape, out.dtype

N_MICROBATCH = 1
ADAM_LR = 0.001
ADAM_B1 = 0.9
ADAM_B2 = 0.999
ADAM_EPS = 1e-08
ADAM_WD = 0.01
ADAM_STEP = 10
PER_EXAMPLE_BATCH_AXIS = {'x': 0, 'loss_target': 0}
SHARED_INPUTS = []
_WEIGHT_DTYPES = {'norm_mix_g': _jnp.float32, 'norm_ffn_g': _jnp.float32, 'norm_final_g': _jnp.float32, 'cv_w_in': _jnp.float32, 'cv_b_in': _jnp.float32, 'cv_w_dw': _jnp.float32, 'cv_b_dw': _jnp.float32, 'cv_ln_g': _jnp.float32, 'cv_ln_b': _jnp.float32, 'cv_w_out': _jnp.float32, 'cv_b_out': _jnp.float32, 'ssm_w_in': _jnp.float32, 'ssm_w_conv': _jnp.float32, 'ssm_b_conv': _jnp.float32, 'ssm_dt_bias': _jnp.float32, 'ssm_a_log': _jnp.float32, 'ssm_d': _jnp.float32, 'ssm_norm_g': _jnp.float32, 'ssm_w_out': _jnp.float32, 'ffn_w_up': _jnp.float32, 'ffn_w_dw': _jnp.float32, 'ffn_b_dw': _jnp.float32, 'ffn_w_down': _jnp.float32}
MOMENT_SCALE = {'norm_mix_g': 7.922872e-02, 'norm_ffn_g': 6.176221e-02, 'norm_final_g': 1.601577e+01, 'cv_w_in': 4.879705e-02, 'cv_b_in': 6.471793e-02, 'cv_w_dw': 6.457089e-02, 'cv_b_dw': 1.541773e-01, 'cv_ln_g': 8.523626e-02, 'cv_ln_b': 8.888311e-02, 'cv_w_out': 6.690571e-02, 'cv_b_out': 1.602218e-01, 'ssm_w_in': 3.909523e-02, 'ssm_w_conv': 3.654991e-02, 'ssm_b_conv': 5.494256e-02, 'ssm_dt_bias': 8.220059e-02, 'ssm_a_log': 2.107580e-01, 'ssm_d': 2.445540e-01, 'ssm_norm_g': 4.249288e-02, 'ssm_w_out': 6.160064e-02, 'ffn_w_up': 2.645274e-02, 'ffn_w_dw': 2.650595e-02, 'ffn_b_dw': 2.814280e-02, 'ffn_w_down': 4.335713e-02}


def _to_microbatches(a, axis):
    t = _jnp.moveaxis(a, axis, 0)
    t = t.reshape((N_MICROBATCH, t.shape[0] // N_MICROBATCH) + t.shape[1:])
    return _jnp.moveaxis(t, 1, axis + 1)


def setup_inputs(seed: int = 0) -> dict:
    inp = _fwd_setup_inputs(seed)
    key = _jax.random.fold_in(_jax.random.key(seed), 7919)
    shape, _ = _output_shape()
    out = dict(inp)
    out["loss_target"] = _jax.random.normal(_jax.random.fold_in(key, 0), shape, _jnp.float32)
    for i, name in enumerate(TWIN_WEIGHTS):
        w = inp[name].astype(_jnp.float32)
        if MOMENT_SCALE is None:
            s = _jnp.sqrt(_jnp.mean(_jnp.square(w)) + 1e-30)
        else:
            s = MOMENT_SCALE[name]
        km, kv = _jax.random.split(_jax.random.fold_in(key, i + 1))
        out[name] = w
        out["m_" + name] = s * _jax.random.normal(km, w.shape, _jnp.float32)
        out["v_" + name] = (s * s) * _jax.random.uniform(kv, w.shape, _jnp.float32, 0.5, 1.5)
    if N_MICROBATCH > 1:
        for name, axis in PER_EXAMPLE_BATCH_AXIS.items():
            out[name] = _to_microbatches(out[name], axis)
    return {'x': out['x'], 'norm_mix_g': out['norm_mix_g'], 'norm_ffn_g': out['norm_ffn_g'], 'norm_final_g': out['norm_final_g'], 'cv_w_in': out['cv_w_in'], 'cv_b_in': out['cv_b_in'], 'cv_w_dw': out['cv_w_dw'], 'cv_b_dw': out['cv_b_dw'], 'cv_ln_g': out['cv_ln_g'], 'cv_ln_b': out['cv_ln_b'], 'cv_w_out': out['cv_w_out'], 'cv_b_out': out['cv_b_out'], 'ssm_w_in': out['ssm_w_in'], 'ssm_w_conv': out['ssm_w_conv'], 'ssm_b_conv': out['ssm_b_conv'], 'ssm_dt_bias': out['ssm_dt_bias'], 'ssm_a_log': out['ssm_a_log'], 'ssm_d': out['ssm_d'], 'ssm_norm_g': out['ssm_norm_g'], 'ssm_w_out': out['ssm_w_out'], 'ffn_w_up': out['ffn_w_up'], 'ffn_w_dw': out['ffn_w_dw'], 'ffn_b_dw': out['ffn_b_dw'], 'ffn_w_down': out['ffn_w_down'], 'loss_target': out['loss_target'], 'm_norm_mix_g': out['m_norm_mix_g'], 'm_norm_ffn_g': out['m_norm_ffn_g'], 'm_norm_final_g': out['m_norm_final_g'], 'm_cv_w_in': out['m_cv_w_in'], 'm_cv_b_in': out['m_cv_b_in'], 'm_cv_w_dw': out['m_cv_w_dw'], 'm_cv_b_dw': out['m_cv_b_dw'], 'm_cv_ln_g': out['m_cv_ln_g'], 'm_cv_ln_b': out['m_cv_ln_b'], 'm_cv_w_out': out['m_cv_w_out'], 'm_cv_b_out': out['m_cv_b_out'], 'm_ssm_w_in': out['m_ssm_w_in'], 'm_ssm_w_conv': out['m_ssm_w_conv'], 'm_ssm_b_conv': out['m_ssm_b_conv'], 'm_ssm_dt_bias': out['m_ssm_dt_bias'], 'm_ssm_a_log': out['m_ssm_a_log'], 'm_ssm_d': out['m_ssm_d'], 'm_ssm_norm_g': out['m_ssm_norm_g'], 'm_ssm_w_out': out['m_ssm_w_out'], 'm_ffn_w_up': out['m_ffn_w_up'], 'm_ffn_w_dw': out['m_ffn_w_dw'], 'm_ffn_b_dw': out['m_ffn_b_dw'], 'm_ffn_w_down': out['m_ffn_w_down'], 'v_norm_mix_g': out['v_norm_mix_g'], 'v_norm_ffn_g': out['v_norm_ffn_g'], 'v_norm_final_g': out['v_norm_final_g'], 'v_cv_w_in': out['v_cv_w_in'], 'v_cv_b_in': out['v_cv_b_in'], 'v_cv_w_dw': out['v_cv_w_dw'], 'v_cv_b_dw': out['v_cv_b_dw'], 'v_cv_ln_g': out['v_cv_ln_g'], 'v_cv_ln_b': out['v_cv_ln_b'], 'v_cv_w_out': out['v_cv_w_out'], 'v_cv_b_out': out['v_cv_b_out'], 'v_ssm_w_in': out['v_ssm_w_in'], 'v_ssm_w_conv': out['v_ssm_w_conv'], 'v_ssm_b_conv': out['v_ssm_b_conv'], 'v_ssm_dt_bias': out['v_ssm_dt_bias'], 'v_ssm_a_log': out['v_ssm_a_log'], 'v_ssm_d': out['v_ssm_d'], 'v_ssm_norm_g': out['v_ssm_norm_g'], 'v_ssm_w_out': out['v_ssm_w_out'], 'v_ffn_w_up': out['v_ffn_w_up'], 'v_ffn_w_dw': out['v_ffn_w_dw'], 'v_ffn_b_dw': out['v_ffn_b_dw'], 'v_ffn_w_down': out['v_ffn_w_down']}


def _loss(weights, diff, rest, loss_target):
    with _jax.named_scope("forward"):
        args = {**rest, TWIN_DIFF_INPUT: diff, **{k: w.astype(_WEIGHT_DTYPES[k]) for k, w in weights.items()}}
        y = _forward(args)
    with _jax.named_scope("loss_head"):
        err = _jnp.square(y.astype(_jnp.float32) - loss_target)
        return 0.5 * _jnp.sum(_jnp.mean(err, axis=-1)) if err.ndim else 0.5 * err


def _adamw(w, g, m, v):
    m = ADAM_B1 * m + (1.0 - ADAM_B1) * g
    v = ADAM_B2 * v + (1.0 - ADAM_B2) * _jnp.square(g)
    m_hat = m / (1.0 - ADAM_B1 ** ADAM_STEP)
    v_hat = v / (1.0 - ADAM_B2 ** ADAM_STEP)
    delta = -ADAM_LR * (m_hat / (_jnp.sqrt(v_hat) + ADAM_EPS) + ADAM_WD * w)
    return delta, m, v


def reference(x, norm_mix_g, norm_ffn_g, norm_final_g, cv_w_in, cv_b_in, cv_w_dw, cv_b_dw, cv_ln_g, cv_ln_b, cv_w_out, cv_b_out, ssm_w_in, ssm_w_conv, ssm_b_conv, ssm_dt_bias, ssm_a_log, ssm_d, ssm_norm_g, ssm_w_out, ffn_w_up, ffn_w_dw, ffn_b_dw, ffn_w_down, loss_target, m_norm_mix_g, m_norm_ffn_g, m_norm_final_g, m_cv_w_in, m_cv_b_in, m_cv_w_dw, m_cv_b_dw, m_cv_ln_g, m_cv_ln_b, m_cv_w_out, m_cv_b_out, m_ssm_w_in, m_ssm_w_conv, m_ssm_b_conv, m_ssm_dt_bias, m_ssm_a_log, m_ssm_d, m_ssm_norm_g, m_ssm_w_out, m_ffn_w_up, m_ffn_w_dw, m_ffn_b_dw, m_ffn_w_down, v_norm_mix_g, v_norm_ffn_g, v_norm_final_g, v_cv_w_in, v_cv_b_in, v_cv_w_dw, v_cv_b_dw, v_cv_ln_g, v_cv_ln_b, v_cv_w_out, v_cv_b_out, v_ssm_w_in, v_ssm_w_conv, v_ssm_b_conv, v_ssm_dt_bias, v_ssm_a_log, v_ssm_d, v_ssm_norm_g, v_ssm_w_out, v_ffn_w_up, v_ffn_w_dw, v_ffn_b_dw, v_ffn_w_down):
    given = dict(x=x, norm_mix_g=norm_mix_g, norm_ffn_g=norm_ffn_g, norm_final_g=norm_final_g, cv_w_in=cv_w_in, cv_b_in=cv_b_in, cv_w_dw=cv_w_dw, cv_b_dw=cv_b_dw, cv_ln_g=cv_ln_g, cv_ln_b=cv_ln_b, cv_w_out=cv_w_out, cv_b_out=cv_b_out, ssm_w_in=ssm_w_in, ssm_w_conv=ssm_w_conv, ssm_b_conv=ssm_b_conv, ssm_dt_bias=ssm_dt_bias, ssm_a_log=ssm_a_log, ssm_d=ssm_d, ssm_norm_g=ssm_norm_g, ssm_w_out=ssm_w_out, ffn_w_up=ffn_w_up, ffn_w_dw=ffn_w_dw, ffn_b_dw=ffn_b_dw, ffn_w_down=ffn_w_down, loss_target=loss_target, m_norm_mix_g=m_norm_mix_g, m_norm_ffn_g=m_norm_ffn_g, m_norm_final_g=m_norm_final_g, m_cv_w_in=m_cv_w_in, m_cv_b_in=m_cv_b_in, m_cv_w_dw=m_cv_w_dw, m_cv_b_dw=m_cv_b_dw, m_cv_ln_g=m_cv_ln_g, m_cv_ln_b=m_cv_ln_b, m_cv_w_out=m_cv_w_out, m_cv_b_out=m_cv_b_out, m_ssm_w_in=m_ssm_w_in, m_ssm_w_conv=m_ssm_w_conv, m_ssm_b_conv=m_ssm_b_conv, m_ssm_dt_bias=m_ssm_dt_bias, m_ssm_a_log=m_ssm_a_log, m_ssm_d=m_ssm_d, m_ssm_norm_g=m_ssm_norm_g, m_ssm_w_out=m_ssm_w_out, m_ffn_w_up=m_ffn_w_up, m_ffn_w_dw=m_ffn_w_dw, m_ffn_b_dw=m_ffn_b_dw, m_ffn_w_down=m_ffn_w_down, v_norm_mix_g=v_norm_mix_g, v_norm_ffn_g=v_norm_ffn_g, v_norm_final_g=v_norm_final_g, v_cv_w_in=v_cv_w_in, v_cv_b_in=v_cv_b_in, v_cv_w_dw=v_cv_w_dw, v_cv_b_dw=v_cv_b_dw, v_cv_ln_g=v_cv_ln_g, v_cv_ln_b=v_cv_ln_b, v_cv_w_out=v_cv_w_out, v_cv_b_out=v_cv_b_out, v_ssm_w_in=v_ssm_w_in, v_ssm_w_conv=v_ssm_w_conv, v_ssm_b_conv=v_ssm_b_conv, v_ssm_dt_bias=v_ssm_dt_bias, v_ssm_a_log=v_ssm_a_log, v_ssm_d=v_ssm_d, v_ssm_norm_g=v_ssm_norm_g, v_ssm_w_out=v_ssm_w_out, v_ffn_w_up=v_ffn_w_up, v_ffn_w_dw=v_ffn_w_dw, v_ffn_b_dw=v_ffn_b_dw, v_ffn_w_down=v_ffn_w_down)
    weights = {n: given[n] for n in TWIN_WEIGHTS}
    shared = {n: given[n] for n in SHARED_INPUTS}
    per_example = {n: given[n] for n in ['x']}
    grad_fn = _jax.value_and_grad(_loss, argnums=(0, 1))

    def one_microbatch(ex, loss_target):
        ex = dict(ex)
        diff = ex.pop(TWIN_DIFF_INPUT)
        return grad_fn(weights, diff, {**shared, **ex}, loss_target)

    if N_MICROBATCH == 1:
        loss, (grad_w, grad_x) = one_microbatch(per_example, given["loss_target"])
    else:
        def body(carry, xs):
            loss_sum, grad_sum = carry
            l_k, (gw_k, gx_k) = one_microbatch(xs[0], xs[1])
            with _jax.named_scope("update"):
                return (loss_sum + l_k, _jax.tree.map(_jnp.add, grad_sum, gw_k)), gx_k

        init = (_jnp.zeros((), _jnp.float32), _jax.tree.map(_jnp.zeros_like, weights))
        (loss, grad_w), grad_x = _jax.lax.scan(body, init, (per_example, given["loss_target"]))
    with _jax.named_scope("update"):
        delta_w, new_m, new_v = {}, {}, {}
        for n in TWIN_WEIGHTS:
            delta_w[n], new_m[n], new_v[n] = _adamw(weights[n], grad_w[n], given["m_" + n], given["v_" + n])
    return (loss, grad_x, *[grad_w[n] for n in TWIN_WEIGHTS], *[delta_w[n] for n in TWIN_WEIGHTS],
            *[new_m[n] for n in TWIN_WEIGHTS], *[new_v[n] for n in TWIN_WEIGHTS])
```

```python
import functools

import jax
import jax.numpy as jnp
from jax import lax
from jax.experimental import pallas as pl
from jax.experimental.pallas import tpu as pltpu

F32, BF16 = jnp.float32, jnp.bfloat16
AXES = ("x", "y", "c")
N_DEV = 8
MESH_ID = pl.DeviceIdType.MESH

D_MODEL = 2048
DEPTH = 4
CHUNK = 64
CONV_K = 31
SSM_INNER = 4096
SSM_HEADS = 64
SSM_HEAD_DIM = 64
SSM_GROUPS = 8
SSM_GROUP_W = SSM_INNER // SSM_GROUPS
SSM_STATE = 128
SSM_CONV_K = 4
SSM_CONV_DIM = SSM_INNER + 2 * SSM_GROUPS * SSM_STATE
SSM_IN_DIM = SSM_INNER + SSM_CONV_DIM + SSM_HEADS
SSM_SLAB = SSM_IN_DIM // N_DEV
SSM_SLAB_PAD = 1408
SSM_COMPACT = 1280 * (N_DEV - 1) + SSM_SLAB_PAD
FFN_HIDDEN = 5632
FFN_K = 3
HEAD_LANES = 128
RMS_EPS = 1e-6
LN_EPS = 1e-5
ADAM_LR, ADAM_B1, ADAM_B2, ADAM_EPS, ADAM_WD, ADAM_STEP = 0.001, 0.9, 0.999, 1e-08, 0.01, 10

_DIMS = {
    "nn": (((1,), (0,)), ((), ())),
    "nt": (((1,), (1,)), ((), ())),
    "tn": (((0,), (0,)), ((), ())),
}


def _params(sem=None, vmem_mb=48):
    return pltpu.CompilerParams(dimension_semantics=sem, vmem_limit_bytes=vmem_mb << 20)


def _dot(a, b, mode="nn"):
    return lax.dot_general(a, b, _DIMS[mode], preferred_element_type=F32)


def _split3(x):
    hi = x.astype(BF16)
    r1 = x - hi.astype(F32)
    mid = r1.astype(BF16)
    lo = (r1 - mid.astype(F32)).astype(BF16)
    return hi, mid, lo


def _dot3_l(x, m, mode="nn"):
    hi, mid, lo = _split3(x)
    return _dot(hi, m, mode) + _dot(mid, m, mode) + _dot(lo, m, mode)


def _dot3_r(m, x, mode="nn"):
    hi, mid, lo = _split3(x)
    return _dot(m, hi, mode) + _dot(m, mid, mode) + _dot(m, lo, mode)


def _sigmoid(x):
    return jax.nn.sigmoid(x)


def _dsilu(x, sg):
    return sg * (1.0 + x * (1.0 - sg))


def _softplus(x):
    return jnp.maximum(x, 0.0) + jnp.log(1.0 + jnp.exp(-jnp.abs(x)))


def _tile(n, pref):
    return min(n, pref)


def _matmul(a, b, *, mode, grid, a_spec, b_spec, o_spec, o_block, out_shape, nk, name,
            extras=(), epilogue=None, vmem_mb=48):
    n_extra = len(extras)

    def body(a_ref, b_ref, *rest):
        extra_refs = rest[:n_extra]
        o_ref = rest[n_extra]
        part = _dot(a_ref[...].astype(BF16), b_ref[...].astype(BF16), mode)

        def finish(acc):
            if epilogue is not None:
                acc = epilogue(acc, *[r[...] for r in extra_refs])
            o_ref[...] = acc.astype(o_ref.dtype)

        if nk == 1:
            finish(part)
        else:
            acc_ref = rest[n_extra + 1]
            k = pl.program_id(len(grid) - 1)

            @pl.when(k == 0)
            def _():
                acc_ref[...] = part

            @pl.when(k > 0)
            def _():
                acc_ref[...] += part

            @pl.when(k == nk - 1)
            def _():
                finish(acc_ref[...])

    scratch = [] if nk == 1 else [pltpu.VMEM(o_block, F32)]
    sem = ("parallel",) * (len(grid) - 1) + ("arbitrary",)
    return pl.pallas_call(
        body, name=name, grid=grid,
        in_specs=[a_spec, b_spec] + [s for _, s in extras],
        out_specs=o_spec, out_shape=out_shape, scratch_shapes=scratch,
        compiler_params=_params(sem, vmem_mb),
    )(a, b, *[x for x, _ in extras])


def _mm_cols_nn(h, wg, j, *, name, bias=None):
    L, K = h.shape
    n = wg.shape[-1]
    tm = _tile(L, 512)
    extras, epi = (), None
    if bias is not None:
        extras = ((bias, pl.BlockSpec((1, n), lambda s, i: (0, s))),)
        epi = lambda acc, b: acc + b
    return _matmul(
        h, wg, mode="nn", grid=(N_DEV, L // tm),
        a_spec=pl.BlockSpec((tm, K), lambda s, i: (i, 0)),
        b_spec=pl.BlockSpec((None, None, K, n), lambda s, i: (j, s, 0, 0)),
        o_spec=pl.BlockSpec((tm, n), lambda s, i: (i, s)), o_block=(tm, n),
        out_shape=jax.ShapeDtypeStruct((L, N_DEV * n), F32), nk=1, name=name,
        extras=extras, epilogue=epi)


def _mm_cols_nt(du, wg, j, *, name):
    L = du.shape[0]
    K, n = wg.shape[-2:]
    tm = _tile(L, 512)
    return _matmul(
        du, wg, mode="nt", grid=(L // tm, N_DEV),
        a_spec=pl.BlockSpec((tm, n), lambda i, s: (i, s)),
        b_spec=pl.BlockSpec((None, None, K, n), lambda i, s: (j, s, 0, 0)),
        o_spec=pl.BlockSpec((tm, K), lambda i, s: (i, 0)), o_block=(tm, K),
        out_shape=jax.ShapeDtypeStruct((L, K), F32), nk=N_DEV, name=name)


def _mm_cols_tn(h, du, *, name):
    L, K = h.shape
    n = du.shape[1] // N_DEV
    tm = _tile(L, 512)
    kh = K // 2
    return _matmul(
        h, du, mode="tn", grid=(N_DEV, 2, L // tm),
        a_spec=pl.BlockSpec((tm, kh), lambda s, q, t: (t, q)),
        b_spec=pl.BlockSpec((tm, n), lambda s, q, t: (t, s)),
        o_spec=pl.BlockSpec((None, kh, n), lambda s, q, t: (s, q, 0)), o_block=(kh, n),
        out_shape=jax.ShapeDtypeStruct((N_DEV, K, n), BF16), nk=L // tm, name=name)


def _mm_rows_nn(a, wg, j, *, res, name, bias=None):
    L, Kw = a.shape
    N = wg.shape[-1]
    tm, tk = _tile(L, 512), 512
    extras = [(res, pl.BlockSpec((tm, N), lambda i, k: (i, 0)))]
    if bias is not None:
        extras.append((bias, pl.BlockSpec((1, N), lambda i, k: (0, 0))))
        epi = lambda acc, r, b: acc + r + b
    else:
        epi = lambda acc, r: acc + r
    return _matmul(
        a, wg, mode="nn", grid=(L // tm, Kw // tk),
        a_spec=pl.BlockSpec((tm, tk), lambda i, k: (i, k)),
        b_spec=pl.BlockSpec((None, tk, N), lambda i, k: (j, k, 0)),
        o_spec=pl.BlockSpec((tm, N), lambda i, k: (i, 0)), o_block=(tm, N),
        out_shape=jax.ShapeDtypeStruct((L, N), F32), nk=Kw // tk, name=name,
        extras=tuple(extras), epilogue=epi)


def _mm_rows_nt(dy, wg, j, *, name):
    L, N = dy.shape
    Kw = wg.shape[-2]
    tm, tn = _tile(L, 512), 512
    return _matmul(
        dy, wg, mode="nt", grid=(L // tm, Kw // tn),
        a_spec=pl.BlockSpec((tm, N), lambda i, q: (i, 0)),
        b_spec=pl.BlockSpec((None, tn, N), lambda i, q: (j, q, 0)),
        o_spec=pl.BlockSpec((tm, tn), lambda i, q: (i, q)), o_block=(tm, tn),
        out_shape=jax.ShapeDtypeStruct((L, Kw), F32), nk=1, name=name)


def _mm_rows_tn(a, dy, *, name):
    L, Kw = a.shape
    N = dy.shape[1]
    tm, tq = _tile(L, 512), 512
    return _matmul(
        a, dy, mode="tn", grid=(Kw // tq, L // tm),
        a_spec=pl.BlockSpec((tm, tq), lambda q, t: (t, q)),
        b_spec=pl.BlockSpec((tm, N), lambda q, t: (t, 0)),
        o_spec=pl.BlockSpec((tq, N), lambda q, t: (q, 0)), o_block=(tq, N),
        out_shape=jax.ShapeDtypeStruct((Kw, N), BF16), nk=L // tm, name=name)


def _rms_fwd(x, g, *, name):
    L, Dm = x.shape
    tm = _tile(L, 256)

    def body(x_ref, g_ref, h_ref):
        xv = x_ref[...]
        r = lax.rsqrt(jnp.mean(xv * xv, axis=-1, keepdims=True) + RMS_EPS)
        h_ref[...] = (xv * r * g_ref[...]).astype(BF16)

    return pl.pallas_call(
        body, name=name, grid=(L // tm,),
        in_specs=[pl.BlockSpec((tm, Dm), lambda i: (i, 0)), pl.BlockSpec((1, Dm), lambda i: (0, 0))],
        out_specs=pl.BlockSpec((tm, Dm), lambda i: (i, 0)),
        out_shape=jax.ShapeDtypeStruct((L, Dm), BF16),
        compiler_params=_params(("parallel",)),
    )(x, g)


def _rms_bwd(x, g, dh, dres, *, name):
    L, Dm = x.shape
    tm = _tile(L, 256)

    def body(x_ref, g_ref, dh_ref, dres_ref, dx_ref, dg_ref, cs_ref):
        i = pl.program_id(0)
        xv = x_ref[...]
        r = lax.rsqrt(jnp.mean(xv * xv, axis=-1, keepdims=True) + RMS_EPS)
        xh = xv * r
        dh = dh_ref[...]
        dxh = dh * g_ref[...]
        dx = dres_ref[...] + r * (dxh - xh * jnp.mean(dxh * xh, axis=-1, keepdims=True))
        dx_ref[...] = dx

        @pl.when(i == 0)
        def _():
            dg_ref[...] = jnp.zeros_like(dg_ref)
            cs_ref[...] = jnp.zeros_like(cs_ref)

        dg_ref[...] += jnp.sum(dh * xh, axis=0, keepdims=True)
        cs_ref[...] += jnp.sum(dx, axis=0, keepdims=True)

    row = pl.BlockSpec((tm, Dm), lambda i: (i, 0))
    vec = pl.BlockSpec((1, Dm), lambda i: (0, 0))
    return pl.pallas_call(
        body, name=name, grid=(L // tm,),
        in_specs=[row, vec, row, row], out_specs=[row, vec, vec],
        out_shape=[jax.ShapeDtypeStruct((L, Dm), F32), jax.ShapeDtypeStruct((1, Dm), F32),
                   jax.ShapeDtypeStruct((1, Dm), F32)],
        compiler_params=_params(("arbitrary",)),
    )(x, g, dh, dres)


def _final_loss(x, g, target, *, name):
    L, Dm = x.shape
    tm = _tile(L, 256)

    def body(x_ref, g_ref, t_ref, dx_ref, dg_ref, loss_ref):
        i = pl.program_id(0)
        xv = x_ref[...]
        gv = g_ref[...]
        r = lax.rsqrt(jnp.mean(xv * xv, axis=-1, keepdims=True) + RMS_EPS)
        xh = xv * r
        err = xh * gv - t_ref[...]
        dy = err * (1.0 / Dm)
        dxh = dy * gv
        dx_ref[...] = r * (dxh - xh * jnp.mean(dxh * xh, axis=-1, keepdims=True))

        @pl.when(i == 0)
        def _():
            dg_ref[...] = jnp.zeros_like(dg_ref)
            loss_ref[...] = jnp.zeros_like(loss_ref)

        dg_ref[...] += jnp.sum(dy * xh, axis=0, keepdims=True)
        loss_ref[...] += 0.5 * jnp.sum(jnp.mean(err * err, axis=-1, keepdims=True), axis=0, keepdims=True)

    row = pl.BlockSpec((tm, Dm), lambda i: (i, 0))
    vec = pl.BlockSpec((1, Dm), lambda i: (0, 0))
    return pl.pallas_call(
        body, name=name, grid=(L // tm,),
        in_specs=[row, vec, row],
        out_specs=[row, vec, pl.BlockSpec((1, 128), lambda i: (0, 0))],
        out_shape=[jax.ShapeDtypeStruct((L, Dm), F32), jax.ShapeDtypeStruct((1, Dm), F32),
                   jax.ShapeDtypeStruct((1, 128), F32)],
        compiler_params=_params(("arbitrary",)),
    )(x, g, target)


def _conv_from_ext(ext_ref, w_ref, bias, taps, halo, rows):
    acc = jnp.broadcast_to(bias, (rows, ext_ref.shape[1]))
    for j in range(taps):
        acc = acc + w_ref[taps - 1 - j:taps - j, :] * ext_ref[halo - j:halo - j + rows, :]
    return acc


def _conv_back(dext_ref, x_cur, w_ref, dw_ref, taps, rows):
    dx = jnp.zeros((rows, dext_ref.shape[1]), F32)
    for j in range(taps):
        sh = dext_ref[j:j + rows, :]
        dx = dx + w_ref[taps - 1 - j:taps - j, :] * sh
        dw_ref[taps - 1 - j:taps - j, :] += jnp.sum(x_cur * sh, axis=0, keepdims=True)
    return dx


def _prev_blk(i, r):
    return jnp.maximum(i * r - 1, 0)


def _next_blk(i, r, nblk):
    return jnp.minimum((i + 1) * r, nblk - 1)


CV_HALO = 32


def _cv_mid_fwd(u, w_dw, b_dw, ln_g, ln_b, *, name):
    L = u.shape[0]
    Dm = D_MODEL
    tm = _tile(L, 256)
    r = tm // CV_HALO

    def body(a_ref, g_ref, ah_ref, gh_ref, w_ref, bdw_ref, lg_ref, lb_ref, s_ref, c_ref, ext_ref):
        i = pl.program_id(0)
        keep = (i > 0).astype(F32)
        ext_ref[0:CV_HALO, :] = ah_ref[...] * _sigmoid(gh_ref[...]) * keep
        ext_ref[CV_HALO:CV_HALO + tm, :] = a_ref[...] * _sigmoid(g_ref[...])
        c = _conv_from_ext(ext_ref, w_ref, bdw_ref[...], CONV_K, CV_HALO, tm)
        c_ref[...] = c
        mu = jnp.mean(c, axis=-1, keepdims=True)
        xc = c - mu
        rstd = lax.rsqrt(jnp.mean(xc * xc, axis=-1, keepdims=True) + LN_EPS)
        l = xc * rstd * lg_ref[...] + lb_ref[...]
        s_ref[...] = (l * _sigmoid(l)).astype(BF16)

    vec = pl.BlockSpec((1, Dm), lambda i: (0, 0))
    return pl.pallas_call(
        body, name=name, grid=(L // tm,),
        in_specs=[pl.BlockSpec((tm, Dm), lambda i: (i, 0)), pl.BlockSpec((tm, Dm), lambda i: (i, 1)),
                  pl.BlockSpec((CV_HALO, Dm), lambda i: (_prev_blk(i, r), 0)),
                  pl.BlockSpec((CV_HALO, Dm), lambda i: (_prev_blk(i, r), 1)),
                  pl.BlockSpec((CONV_K, Dm), lambda i: (0, 0)), vec, vec, vec],
        out_specs=[pl.BlockSpec((tm, Dm), lambda i: (i, 0)), pl.BlockSpec((tm, Dm), lambda i: (i, 0))],
        out_shape=[jax.ShapeDtypeStruct((L, Dm), BF16), jax.ShapeDtypeStruct((L, Dm), F32)],
        scratch_shapes=[pltpu.VMEM((CV_HALO + tm, Dm), F32)],
        compiler_params=_params(("parallel",)),
    )(u, u, u, u, w_dw, b_dw, ln_g, ln_b)


def _cv_bwd(ds, c, u, w_dw, ln_g, ln_b, *, name):
    L = u.shape[0]
    Dm = D_MODEL
    tm = _tile(L, 128)
    r = tm // CV_HALO
    nt = L // tm
    nblk = L // CV_HALO
    ext_rows = tm + CV_HALO

    def body(ds_ref, dsn_ref, c_ref, cn_ref, a_ref, g_ref, w_ref, lg_ref, lb_ref,
             du_ref, dw_ref, dbdw_ref, dlg_ref, dlb_ref, dbin_ref, dsx_ref, cx_ref, dext_ref):
        i = pl.program_id(0)

        @pl.when(i == 0)
        def _():
            for ref in (dw_ref, dbdw_ref, dlg_ref, dlb_ref, dbin_ref):
                ref[...] = jnp.zeros_like(ref)

        keep = (i < nt - 1).astype(F32)
        dsx_ref[0:tm, :] = ds_ref[...]
        dsx_ref[tm:ext_rows, :] = dsn_ref[...] * keep
        cx_ref[0:tm, :] = c_ref[...]
        cx_ref[tm:ext_rows, :] = cn_ref[...]
        cv = cx_ref[...]
        mu = jnp.mean(cv, axis=-1, keepdims=True)
        xc = cv - mu
        rstd = lax.rsqrt(jnp.mean(xc * xc, axis=-1, keepdims=True) + LN_EPS)
        nrm = xc * rstd
        lg = lg_ref[...]
        l = nrm * lg + lb_ref[...]
        dl = dsx_ref[...] * _dsilu(l, _sigmoid(l))
        dn = dl * lg
        dc = rstd * (dn - jnp.mean(dn, axis=-1, keepdims=True)
                     - nrm * jnp.mean(dn * nrm, axis=-1, keepdims=True))
        dext_ref[...] = dc
        dlg_ref[...] += jnp.sum((dl * nrm)[0:tm], axis=0, keepdims=True)
        dlb_ref[...] += jnp.sum(dl[0:tm], axis=0, keepdims=True)
        dbdw_ref[...] += jnp.sum(dc[0:tm], axis=0, keepdims=True)

        av, gv = a_ref[...], g_ref[...]
        sg = _sigmoid(gv)
        dv = _conv_back(dext_ref, av * sg, w_ref, dw_ref, CONV_K, tm)
        da = dv * sg
        dgate = dv * av * sg * (1.0 - sg)
        du_ref[:, 0:Dm] = da.astype(BF16)
        du_ref[:, Dm:2 * Dm] = dgate.astype(BF16)
        dbin_ref[:, 0:Dm] += jnp.sum(da, axis=0, keepdims=True)
        dbin_ref[:, Dm:2 * Dm] += jnp.sum(dgate, axis=0, keepdims=True)

    row = pl.BlockSpec((tm, Dm), lambda i: (i, 0))
    nxt = pl.BlockSpec((CV_HALO, Dm), lambda i: (_next_blk(i, r, nblk), 0))
    vec = pl.BlockSpec((1, Dm), lambda i: (0, 0))
    return pl.pallas_call(
        body, name=name, grid=(nt,),
        in_specs=[row, nxt, row, nxt, row, pl.BlockSpec((tm, Dm), lambda i: (i, 1)),
                  pl.BlockSpec((CONV_K, Dm), lambda i: (0, 0)), vec, vec],
        out_specs=[pl.BlockSpec((tm, 2 * Dm), lambda i: (i, 0)), pl.BlockSpec((CONV_K, Dm), lambda i: (0, 0)),
                   vec, vec, vec, pl.BlockSpec((1, 2 * Dm), lambda i: (0, 0))],
        out_shape=[jax.ShapeDtypeStruct((L, 2 * Dm), BF16), jax.ShapeDtypeStruct((CONV_K, Dm), F32),
                   jax.ShapeDtypeStruct((1, Dm), F32), jax.ShapeDtypeStruct((1, Dm), F32),
                   jax.ShapeDtypeStruct((1, Dm), F32), jax.ShapeDtypeStruct((1, 2 * Dm), F32)],
        scratch_shapes=[pltpu.VMEM((ext_rows, Dm), F32)] * 3,
        compiler_params=_params(("arbitrary",)),
    )(ds, ds, c, c, u, u, w_dw, ln_g, ln_b)


FFN_TC = 512
FFN_NJ = FFN_HIDDEN // FFN_TC
HALO8 = 8


def _ffn_mid_fwd(u0, w_dw, b_dw, *, name):
    L = u0.shape[0]
    tm = _tile(L, 256)
    r = tm // HALO8
    tc, nj = FFN_TC, FFN_NJ

    def body(g_ref, v_ref, gh_ref, vh_ref, wg_ref, wv_ref, bg_ref, bv_ref, o_ref, gx_ref, vx_ref):
        keep = (pl.program_id(0) > 0).astype(F32)
        gx_ref[0:HALO8, :] = gh_ref[...] * keep
        gx_ref[HALO8:HALO8 + tm, :] = g_ref[...]
        vx_ref[0:HALO8, :] = vh_ref[...] * keep
        vx_ref[HALO8:HALO8 + tm, :] = v_ref[...]
        ug = _conv_from_ext(gx_ref, wg_ref, bg_ref[...], FFN_K, HALO8, tm)
        uv = _conv_from_ext(vx_ref, wv_ref, bv_ref[...], FFN_K, HALO8, tm)
        o_ref[...] = (ug * _sigmoid(ug) * uv).astype(BF16)

    return pl.pallas_call(
        body, name=name, grid=(L // tm, nj),
        in_specs=[pl.BlockSpec((tm, tc), lambda i, j: (i, j)), pl.BlockSpec((tm, tc), lambda i, j: (i, j + nj)),
                  pl.BlockSpec((HALO8, tc), lambda i, j: (_prev_blk(i, r), j)),
                  pl.BlockSpec((HALO8, tc), lambda i, j: (_prev_blk(i, r), j + nj)),
                  pl.BlockSpec((FFN_K, tc), lambda i, j: (0, j)), pl.BlockSpec((FFN_K, tc), lambda i, j: (0, j + nj)),
                  pl.BlockSpec((1, tc), lambda i, j: (0, j)), pl.BlockSpec((1, tc), lambda i, j: (0, j + nj))],
        out_specs=pl.BlockSpec((tm, tc), lambda i, j: (i, j)),
        out_shape=jax.ShapeDtypeStruct((L, FFN_HIDDEN), BF16),
        scratch_shapes=[pltpu.VMEM((HALO8 + tm, tc), F32)] * 2,
        compiler_params=_params(("parallel", "parallel")),
    )(u0, u0, u0, u0, w_dw, w_dw, b_dw, b_dw)


def _ffn_bwd(dg, u0, w_dw, b_dw, *, name):
    L = u0.shape[0]
    tm = _tile(L, 256)
    r = tm // HALO8
    nt = L // tm
    nblk = L // HALO8
    tc, nj = FFN_TC, FFN_NJ
    rows = tm + HALO8

    def body(tp_ref, tc_ref, tn_ref, pp_ref, pc_ref, pn_ref, dg_ref, dgn_ref, wt_ref, wp_ref, bt_ref, bp_ref,
             du0_ref, dw_ref, db_ref, tx_ref, px_ref, dext_ref):
        jc, i = pl.program_id(0), pl.program_id(1)

        @pl.when(i == 0)
        def _():
            dw_ref[...] = jnp.zeros_like(dw_ref)
            db_ref[...] = jnp.zeros_like(db_ref)

        keep_prev = (i > 0).astype(F32)
        keep_next = (i < nt - 1).astype(F32)
        for x_ref, p, cur, n in ((tx_ref, tp_ref, tc_ref, tn_ref), (px_ref, pp_ref, pc_ref, pn_ref)):
            x_ref[0:HALO8, :] = p[...] * keep_prev
            x_ref[HALO8:HALO8 + tm, :] = cur[...]
            x_ref[HALO8 + tm:HALO8 + rows, :] = n[...]
        ut = _conv_from_ext(tx_ref, wt_ref, bt_ref[...], FFN_K, HALO8, rows)
        up = _conv_from_ext(px_ref, wp_ref, bp_ref[...], FFN_K, HALO8, rows)
        dgx = jnp.concatenate([dg_ref[...], dgn_ref[...] * keep_next], axis=0)
        sgt = _sigmoid(ut)
        as_gate = dgx * up * _dsilu(ut, sgt)
        as_val = dgx * (up * _sigmoid(up))
        du = jnp.where(jc < nj, as_gate, as_val)
        dext_ref[...] = du
        db_ref[...] += jnp.sum(du[0:tm], axis=0, keepdims=True)
        du0_ref[...] = _conv_back(dext_ref, tc_ref[...], wt_ref, dw_ref, FFN_K, tm).astype(BF16)

    def partner(jc):
        return (jc + nj) % (2 * nj)

    cur = lambda f: pl.BlockSpec((tm, tc), lambda jc, i: (i, f(jc)))
    prv = lambda f: pl.BlockSpec((HALO8, tc), lambda jc, i: (_prev_blk(i, r), f(jc)))
    nxt = lambda f: pl.BlockSpec((HALO8, tc), lambda jc, i: (_next_blk(i, r, nblk), f(jc)))
    ident = lambda jc: jc
    half = lambda jc: jc % nj
    return pl.pallas_call(
        body, name=name, grid=(2 * nj, nt),
        in_specs=[prv(ident), cur(ident), nxt(ident), prv(partner), cur(partner), nxt(partner),
                  cur(half), nxt(half),
                  pl.BlockSpec((FFN_K, tc), lambda jc, i: (0, jc)), pl.BlockSpec((FFN_K, tc), lambda jc, i: (0, partner(jc))),
                  pl.BlockSpec((1, tc), lambda jc, i: (0, jc)), pl.BlockSpec((1, tc), lambda jc, i: (0, partner(jc)))],
        out_specs=[pl.BlockSpec((tm, tc), lambda jc, i: (i, jc)), pl.BlockSpec((FFN_K, tc), lambda jc, i: (0, jc)),
                   pl.BlockSpec((1, tc), lambda jc, i: (0, jc))],
        out_shape=[jax.ShapeDtypeStruct((L, 2 * FFN_HIDDEN), BF16), jax.ShapeDtypeStruct((FFN_K, 2 * FFN_HIDDEN), F32),
                   jax.ShapeDtypeStruct((1, 2 * FFN_HIDDEN), F32)],
        scratch_shapes=[pltpu.VMEM((HALO8 + rows, tc), F32)] * 2 + [pltpu.VMEM((rows, tc), F32)],
        compiler_params=_params(("parallel", "arbitrary")),
    )(u0, u0, u0, u0, u0, u0, dg, dg, w_dw, w_dw, b_dw, b_dw)


def _ssm_unpad(zx_pad, *, name):
    L = zx_pad.shape[0]
    tm = _tile(L, 128)

    def body(p_ref, z_ref, xbc_ref, dt_ref, comp_ref):
        comp_ref[...] = jnp.zeros_like(comp_ref)
        for d in range(N_DEV):
            slab = p_ref[:, d * SSM_SLAB_PAD:(d + 1) * SSM_SLAB_PAD]
            if d:
                slab = pltpu.roll(slab, 8 * d, axis=1)
            comp_ref[:, 1280 * d:1280 * d + SSM_SLAB_PAD] += slab
        z_ref[...] = comp_ref[:, 0:SSM_INNER]
        xbc_ref[...] = comp_ref[:, SSM_INNER:SSM_INNER + SSM_CONV_DIM]
        dt_ref[...] = comp_ref[:, SSM_INNER + SSM_CONV_DIM:SSM_COMPACT]

    return pl.pallas_call(
        body, name=name, grid=(L // tm,),
        in_specs=[pl.BlockSpec((tm, N_DEV * SSM_SLAB_PAD), lambda i: (i, 0))],
        out_specs=[pl.BlockSpec((tm, SSM_INNER), lambda i: (i, 0)), pl.BlockSpec((tm, SSM_CONV_DIM), lambda i: (i, 0)),
                   pl.BlockSpec((tm, HEAD_LANES), lambda i: (i, 0))],
        out_shape=[jax.ShapeDtypeStruct((L, SSM_INNER), F32), jax.ShapeDtypeStruct((L, SSM_CONV_DIM), F32),
                   jax.ShapeDtypeStruct((L, HEAD_LANES), F32)],
        scratch_shapes=[pltpu.VMEM((tm, SSM_COMPACT), F32)],
        compiler_params=_params(("parallel",)),
    )(zx_pad)


def _ssm_pad(dz, dxbc, ddt, *, name):
    L = dz.shape[0]
    tm = _tile(L, 128)

    def body(dz_ref, dx_ref, dt_ref, p_ref, comp_ref):
        comp_ref[:, 0:SSM_INNER] = dz_ref[...]
        comp_ref[:, SSM_INNER:SSM_INNER + SSM_CONV_DIM] = dx_ref[...]
        lane = lax.broadcasted_iota(jnp.int32, (tm, HEAD_LANES), 1)
        comp_ref[:, SSM_INNER + SSM_CONV_DIM:SSM_COMPACT] = jnp.where(lane < SSM_HEADS, dt_ref[...], 0.0)
        col = lax.broadcasted_iota(jnp.int32, (tm, SSM_SLAB_PAD), 1)
        for d in range(N_DEV):
            win = comp_ref[:, 1280 * d:1280 * d + SSM_SLAB_PAD]
            if d:
                win = pltpu.roll(win, SSM_SLAB_PAD - 8 * d, axis=1)
            p_ref[:, d * SSM_SLAB_PAD:(d + 1) * SSM_SLAB_PAD] = jnp.where(col < SSM_SLAB, win, 0.0).astype(BF16)

    return pl.pallas_call(
        body, name=name, grid=(L // tm,),
        in_specs=[pl.BlockSpec((tm, SSM_INNER), lambda i: (i, 0)), pl.BlockSpec((tm, SSM_CONV_DIM), lambda i: (i, 0)),
                  pl.BlockSpec((tm, HEAD_LANES), lambda i: (i, 0))],
        out_specs=pl.BlockSpec((tm, N_DEV * SSM_SLAB_PAD), lambda i: (i, 0)),
        out_shape=jax.ShapeDtypeStruct((L, N_DEV * SSM_SLAB_PAD), BF16),
        scratch_shapes=[pltpu.VMEM((tm, SSM_COMPACT), F32)],
        compiler_params=_params(("parallel",)),
    )(dz, dxbc, ddt)


SSM_TC = 1024


def _ssm_conv_fwd(xpre, w, b, *, name):
    L, C = xpre.shape
    tm = _tile(L, 256)
    r = tm // HALO8
    tc = SSM_TC

    def body(x_ref, xh_ref, w_ref, b_ref, o_ref, ext_ref):
        keep = (pl.program_id(0) > 0).astype(F32)
        ext_ref[0:HALO8, :] = xh_ref[...] * keep
        ext_ref[HALO8:HALO8 + tm, :] = x_ref[...]
        pre = _conv_from_ext(ext_ref, w_ref, b_ref[...], SSM_CONV_K, HALO8, tm)
        o_ref[...] = pre * _sigmoid(pre)

    return pl.pallas_call(
        body, name=name, grid=(L // tm, C // tc),
        in_specs=[pl.BlockSpec((tm, tc), lambda i, j: (i, j)), pl.BlockSpec((HALO8, tc), lambda i, j: (_prev_blk(i, r), j)),
                  pl.BlockSpec((SSM_CONV_K, tc), lambda i, j: (0, j)), pl.BlockSpec((1, tc), lambda i, j: (0, j))],
        out_specs=pl.BlockSpec((tm, tc), lambda i, j: (i, j)),
        out_shape=jax.ShapeDtypeStruct((L, C), F32),
        scratch_shapes=[pltpu.VMEM((HALO8 + tm, tc), F32)],
        compiler_params=_params(("parallel", "parallel")),
    )(xpre, xpre, w, b)


def _ssm_conv_bwd(dy, xpre, w, b, *, name):
    L, C = xpre.shape
    tm = _tile(L, 256)
    r = tm // HALO8
    nt = L // tm
    nblk = L // HALO8
    tc = SSM_TC
    rows = tm + HALO8

    def body(xp_ref, xc_ref, xn_ref, dy_ref, dyn_ref, w_ref, b_ref, dx_ref, dw_ref, db_ref, ext_ref, dext_ref):
        i = pl.program_id(1)

        @pl.when(i == 0)
        def _():
            dw_ref[...] = jnp.zeros_like(dw_ref)
            db_ref[...] = jnp.zeros_like(db_ref)

        ext_ref[0:HALO8, :] = xp_ref[...] * (i > 0).astype(F32)
        ext_ref[HALO8:HALO8 + tm, :] = xc_ref[...]
        ext_ref[HALO8 + tm:HALO8 + rows, :] = xn_ref[...]
        pre = _conv_from_ext(ext_ref, w_ref, b_ref[...], SSM_CONV_K, HALO8, rows)
        dyx = jnp.concatenate([dy_ref[...], dyn_ref[...] * (i < nt - 1).astype(F32)], axis=0)
        dpre = dyx * _dsilu(pre, _sigmoid(pre))
        dext_ref[...] = dpre
        db_ref[...] += jnp.sum(dpre[0:tm], axis=0, keepdims=True)
        dx_ref[...] = _conv_back(dext_ref, xc_ref[...], w_ref, dw_ref, SSM_CONV_K, tm)

    cur = pl.BlockSpec((tm, tc), lambda j, i: (i, j))
    prv = pl.BlockSpec((HALO8, tc), lambda j, i: (_prev_blk(i, r), j))
    nxt = pl.BlockSpec((HALO8, tc), lambda j, i: (_next_blk(i, r, nblk), j))
    return pl.pallas_call(
        body, name=name, grid=(C // tc, nt),
        in_specs=[prv, cur, nxt, cur, nxt, pl.BlockSpec((SSM_CONV_K, tc), lambda j, i: (0, j)),
                  pl.BlockSpec((1, tc), lambda j, i: (0, j))],
        out_specs=[cur, pl.BlockSpec((SSM_CONV_K, tc), lambda j, i: (0, j)), pl.BlockSpec((1, tc), lambda j, i: (0, j))],
        out_shape=[jax.ShapeDtypeStruct((L, C), F32), jax.ShapeDtypeStruct((SSM_CONV_K, C), F32),
                   jax.ShapeDtypeStruct((1, C), F32)],
        scratch_shapes=[pltpu.VMEM((HALO8 + rows, tc), F32), pltpu.VMEM((rows, tc), F32)],
        compiler_params=_params(("parallel", "arbitrary")),
    )(xpre, xpre, xpre, dy, dy, w, b)


def _gnorm_fwd(y, z, g, *, name):
    L, C = y.shape
    tm = _tile(L, 256)
    gw = SSM_GROUP_W

    def body(y_ref, z_ref, g_ref, o_ref):
        for k in range(SSM_GROUPS):
            sl = slice(k * gw, (k + 1) * gw)
            zv = z_ref[:, sl]
            yz = y_ref[:, sl] * zv * _sigmoid(zv)
            r = lax.rsqrt(jnp.mean(yz * yz, axis=-1, keepdims=True) + RMS_EPS)
            o_ref[:, sl] = (yz * r * g_ref[:, sl]).astype(BF16)

    row = pl.BlockSpec((tm, C), lambda i: (i, 0))
    return pl.pallas_call(
        body, name=name, grid=(L // tm,),
        in_specs=[row, row, pl.BlockSpec((1, C), lambda i: (0, 0))], out_specs=row,
        out_shape=jax.ShapeDtypeStruct((L, C), BF16),
        compiler_params=_params(("parallel",)),
    )(y, z, g)


def _gnorm_bwd(dn, y, z, g, *, name):
    L, C = y.shape
    tm = _tile(L, 256)
    gw = SSM_GROUP_W

    def body(dn_ref, y_ref, z_ref, g_ref, dy_ref, dz_ref, dg_ref):
        @pl.when(pl.program_id(0) == 0)
        def _():
            dg_ref[...] = jnp.zeros_like(dg_ref)

        for k in range(SSM_GROUPS):
            sl = slice(k * gw, (k + 1) * gw)
            zv, yv = z_ref[:, sl], y_ref[:, sl]
            sz = _sigmoid(zv)
            silu = zv * sz
            yz = yv * silu
            r = lax.rsqrt(jnp.mean(yz * yz, axis=-1, keepdims=True) + RMS_EPS)
            nrm = yz * r
            dnv = dn_ref[:, sl]
            dg_ref[:, sl] += jnp.sum(dnv * nrm, axis=0, keepdims=True)
            dh = dnv * g_ref[:, sl]
            dyz = r * (dh - nrm * jnp.mean(dh * nrm, axis=-1, keepdims=True))
            dy_ref[:, sl] = dyz * silu
            dz_ref[:, sl] = dyz * yv * _dsilu(zv, sz)

    row = pl.BlockSpec((tm, C), lambda i: (i, 0))
    vec = pl.BlockSpec((1, C), lambda i: (0, 0))
    return pl.pallas_call(
        body, name=name, grid=(L // tm,),
        in_specs=[row, row, row, vec], out_specs=[row, row, vec],
        out_shape=[jax.ShapeDtypeStruct((L, C), F32), jax.ShapeDtypeStruct((L, C), F32),
                   jax.ShapeDtypeStruct((1, C), F32)],
        compiler_params=_params(("arbitrary",)),
    )(dn, y, z, g)


def _ssd_consts():
    q, gw = CHUNK, SSM_GROUP_W
    sub = lax.broadcasted_iota(jnp.int32, (q, gw), 0)
    lane_in = lax.broadcasted_iota(jnp.int32, (q, gw), 1) % q
    sub2 = lax.broadcasted_iota(jnp.int32, (gw, gw), 0) // q
    lane2 = lax.broadcasted_iota(jnp.int32, (gw, gw), 1) // q
    t = lax.broadcasted_iota(jnp.int32, (q, q), 0)
    u = lax.broadcasted_iota(jnp.int32, (q, q), 1)
    return dict(
        diag=(lane_in == sub), low=(lane_in <= sub), up=(sub <= lane_in), block=(sub2 == lane2),
        tri=(u <= t).astype(BF16), trit=(u >= t).astype(BF16), ones=jnp.ones((q, q), BF16),
        last=(lax.broadcasted_iota(jnp.int32, (q, HEAD_LANES), 0) == q - 1))


def _ssd_chunk_terms(dt_raw, bias, a_log, e, et, k):
    dt = _softplus(dt_raw + bias)
    a_neg = -jnp.exp(a_log)
    cs = _dot3_r(k["tri"], dt * a_neg)
    cs_last = cs[CHUNK - 1:CHUNK, :]
    ecs = jnp.exp(cs)
    dte = jnp.exp(cs_last - cs)
    ecl_hn = _dot3_l(ecs, k["last"].astype(BF16), "tn")
    rows = _dot3_r(et, ecl_hn)
    return dict(dt=dt, a_neg=a_neg, cs=cs, ecs=ecs, dte=dte, rows=rows,
                dtx=_dot3_l(dt, e), csx=_dot3_l(cs, e), ecsx=_dot3_l(ecs, e), dtex=_dot3_l(dte, e))


def _tile8(x):
    return jnp.concatenate([x] * 8, axis=0)


def _ssd_fwd(xbc, dt_raw, dt_bias, a_log, d_x, e, et, *, name):
    L = xbc.shape[0]
    nc = L // CHUNK
    q, gw, ns = CHUNK, SSM_GROUP_W, SSM_STATE

    def body(xbc_ref, dt_ref, bias_ref, alog_ref, dx_ref, e_ref, et_ref, y_ref, st_ref, s_ref):
        @pl.when(pl.program_id(0) == 0)
        def _():
            s_ref[...] = jnp.zeros_like(s_ref)

        k = _ssd_consts()
        t = _ssd_chunk_terms(dt_ref[...], bias_ref[...], alog_ref[...], e_ref[...], et_ref[...], k)
        st_ref[...] = s_ref[...]
        for g in range(SSM_GROUPS):
            ch = slice(g * gw, (g + 1) * gw)
            xs = xbc_ref[:, ch]
            bm = xbc_ref[:, SSM_INNER + g * ns:SSM_INNER + (g + 1) * ns].astype(BF16)
            cm = xbc_ref[:, SSM_INNER + (SSM_GROUPS + g) * ns:SSM_INNER + (SSM_GROUPS + g + 1) * ns].astype(BF16)
            xd = xs * t["dtx"][:, ch]
            csx = t["csx"][:, ch]
            csrow = _dot3_r(k["ones"], jnp.where(k["diag"], csx, 0.0))
            lcat = jnp.where(k["low"], jnp.exp(jnp.minimum(csx - csrow, 0.0)), 0.0)
            mcat = _dot(cm, _tile8(bm), "nt") * lcat
            xdbd = jnp.where(k["block"], _tile8(xd), 0.0).astype(BF16)
            sg = s_ref[ch, :]
            y = _dot(mcat.astype(BF16), xdbd)
            y = y + _dot(cm, sg.astype(BF16), "nt") * t["ecsx"][:, ch]
            y_ref[:, ch] = y + dx_ref[:, ch] * xs
            s_ref[ch, :] = sg * t["rows"][ch, :] + _dot((xd * t["dtex"][:, ch]).astype(BF16), bm, "tn")

    hv = pl.BlockSpec((1, HEAD_LANES), lambda c: (0, 0))
    return pl.pallas_call(
        body, name=name, grid=(nc,),
        in_specs=[pl.BlockSpec((q, SSM_CONV_DIM), lambda c: (c, 0)), pl.BlockSpec((q, HEAD_LANES), lambda c: (c, 0)),
                  hv, hv, pl.BlockSpec((1, SSM_INNER), lambda c: (0, 0)),
                  pl.BlockSpec((HEAD_LANES, SSM_INNER), lambda c: (0, 0)),
                  pl.BlockSpec((SSM_INNER, HEAD_LANES), lambda c: (0, 0))],
        out_specs=[pl.BlockSpec((q, SSM_INNER), lambda c: (c, 0)),
                   pl.BlockSpec((None, SSM_INNER, ns), lambda c: (c, 0, 0))],
        out_shape=[jax.ShapeDtypeStruct((L, SSM_INNER), F32), jax.ShapeDtypeStruct((nc, SSM_INNER, ns), F32)],
        scratch_shapes=[pltpu.VMEM((SSM_INNER, ns), F32)],
        compiler_params=_params(("arbitrary",)),
    )(xbc, dt_raw, dt_bias, a_log, d_x, e, et)


def _ssd_bwd(dy, xbc, dt_raw, states, dt_bias, a_log, d_x, e, et, *, name):
    L = xbc.shape[0]
    nc = L // CHUNK
    q, gw, ns = CHUNK, SSM_GROUP_W, SSM_STATE

    def body(dy_ref, xbc_ref, dt_ref, st_ref, bias_ref, alog_ref, dx_ref, e_ref, et_ref,
             dxbc_ref, ddt_ref, dbias_ref, dalog_ref, dd_ref, ds_ref, f1_ref, f2_ref, f3_ref, f4_ref, fs_ref):
        @pl.when(pl.program_id(0) == 0)
        def _():
            ds_ref[...] = jnp.zeros_like(ds_ref)
            for ref in (dbias_ref, dalog_ref, dd_ref):
                ref[...] = jnp.zeros_like(ref)

        k = _ssd_consts()
        ev = e_ref[...]
        t = _ssd_chunk_terms(dt_ref[...], bias_ref[...], alog_ref[...], ev, et_ref[...], k)
        ones8 = jnp.ones((8, ns), BF16)
        for g in range(SSM_GROUPS):
            ch = slice(g * gw, (g + 1) * gw)
            bsl = slice(SSM_INNER + g * ns, SSM_INNER + (g + 1) * ns)
            csl = slice(SSM_INNER + (SSM_GROUPS + g) * ns, SSM_INNER + (SSM_GROUPS + g + 1) * ns)
            xs = xbc_ref[:, ch]
            bm = xbc_ref[:, bsl].astype(BF16)
            cm = xbc_ref[:, csl].astype(BF16)
            dyv = dy_ref[:, ch]
            dtx, ecsx, dtex, csx = t["dtx"][:, ch], t["ecsx"][:, ch], t["dtex"][:, ch], t["csx"][:, ch]
            xd = xs * dtx
            csrow = _dot3_r(k["ones"], jnp.where(k["diag"], csx, 0.0))
            lcat = jnp.where(k["low"], jnp.exp(jnp.minimum(csx - csrow, 0.0)), 0.0)
            ltcat = jnp.where(k["up"], jnp.exp(jnp.minimum(csrow - csx, 0.0)), 0.0)
            mcat = _dot(cm, _tile8(bm), "nt") * lcat
            mtcat = _dot(bm, _tile8(cm), "nt") * ltcat
            xdbd = jnp.where(k["block"], _tile8(xd), 0.0).astype(BF16)
            dybd = jnp.where(k["block"], _tile8(dyv), 0.0).astype(BF16)
            gq = _dot(dyv.astype(BF16), xdbd, "nt") * mcat
            gt = _dot(xd.astype(BF16), dybd, "nt")
            gqt = gt * mtcat
            dcbt = _dot((gt * ltcat).astype(BF16), jnp.where(k["diag"], 1.0, 0.0).astype(BF16), "nt")
            dcbt = dcbt.astype(BF16)
            sg = st_ref[ch, :]
            dsg = ds_ref[ch, :]
            sgb, dsgb = sg.astype(BF16), dsg.astype(BF16)
            yoff = _dot(cm, sgb, "nt") * ecsx
            dye = (dyv * ecsx).astype(BF16)
            xdd = xd * dtex
            dxbc_ref[:, csl] = _dot(dcbt, bm, "tn") + _dot(dye, sgb)
            dxbc_ref[:, bsl] = _dot(dcbt, cm) + _dot(xdd.astype(BF16), dsgb)
            bds = _dot(bm, dsgb, "nt")
            dxd = _dot(mtcat.astype(BF16), dybd) + dtex * bds
            wx = xdd * bds
            ds_ref[ch, :] = t["rows"][ch, :] * dsg + _dot(dye, cm, "tn")
            f1_ref[:, ch] = gq - gqt + dyv * yoff - wx
            f2_ref[:, ch] = wx
            f3_ref[:, ch] = dxd * xs
            f4_ref[:, ch] = dyv * xs
            dxbc_ref[:, ch] = dxd * dtx + dx_ref[:, ch] * dyv
            fs_ref[:, ch] = _dot3_r(ones8, dsg * sg, "nt")
        fold = lambda v: _dot3_l(v, ev, "nt")
        f2 = fold(f2_ref[...])
        last_row = jnp.sum(f2, axis=0, keepdims=True) + t["ecs"][q - 1:q, :] * fold(fs_ref[...])[0:1, :]
        dcs = fold(f1_ref[...]) + jnp.where(k["last"], last_row, 0.0)
        da = _dot3_r(k["trit"], dcs)
        ddt = da * t["a_neg"] + fold(f3_ref[...])
        ddt_raw = ddt * _sigmoid(dt_ref[...] + bias_ref[...])
        ddt_ref[...] = ddt_raw
        dbias_ref[...] += jnp.sum(ddt_raw, axis=0, keepdims=True)
        dalog_ref[...] += jnp.sum(da * t["dt"], axis=0, keepdims=True) * t["a_neg"]
        dd_ref[...] += jnp.sum(fold(f4_ref[...]), axis=0, keepdims=True)

    rev = lambda c: (nc - 1 - c, 0)
    hv = pl.BlockSpec((1, HEAD_LANES), lambda c: (0, 0))
    return pl.pallas_call(
        body, name=name, grid=(nc,),
        in_specs=[pl.BlockSpec((q, SSM_INNER), rev), pl.BlockSpec((q, SSM_CONV_DIM), rev),
                  pl.BlockSpec((q, HEAD_LANES), rev),
                  pl.BlockSpec((None, SSM_INNER, ns), lambda c: (nc - 1 - c, 0, 0)),
                  hv, hv, pl.BlockSpec((1, SSM_INNER), lambda c: (0, 0)),
                  pl.BlockSpec((HEAD_LANES, SSM_INNER), lambda c: (0, 0)),
                  pl.BlockSpec((SSM_INNER, HEAD_LANES), lambda c: (0, 0))],
        out_specs=[pl.BlockSpec((q, SSM_CONV_DIM), rev), pl.BlockSpec((q, HEAD_LANES), rev), hv, hv, hv],
        out_shape=[jax.ShapeDtypeStruct((L, SSM_CONV_DIM), F32), jax.ShapeDtypeStruct((L, HEAD_LANES), F32)]
        + [jax.ShapeDtypeStruct((1, HEAD_LANES), F32)] * 3,
        scratch_shapes=[pltpu.VMEM((SSM_INNER, ns), F32)] + [pltpu.VMEM((q, SSM_INNER), F32)] * 4
        + [pltpu.VMEM((8, SSM_INNER), F32)],
        compiler_params=_params(("arbitrary",)),
    )(dy, xbc, dt_raw, states, dt_bias, a_log, d_x, e, et)


def _my_index():
    return 4 * lax.axis_index("x") + 2 * lax.axis_index("y") + lax.axis_index("c")


def _peer(k):
    return (lax.axis_index("x") ^ ((k >> 2) & 1), lax.axis_index("y") ^ ((k >> 1) & 1), lax.axis_index("c") ^ (k & 1))


_HBM = pl.BlockSpec(memory_space=pltpu.HBM)


def _all_gather(shard, *, by_rows, name):
    nl, a, b = shard.shape
    out_shape = (nl, N_DEV * a, b) if by_rows else (nl, N_DEV, a, b)

    def body(src_ref, out_ref, send_sems, recv_sems, local_sems):
        me = _my_index()

        def mine(j):
            if by_rows:
                return out_ref.at[j, pl.ds(pl.multiple_of(me * a, 16), a), :]
            return out_ref.at[j, me]

        local = [pltpu.make_async_copy(src_ref.at[j], mine(j), local_sems.at[j]) for j in range(nl)]
        for cp in local:
            cp.start()
        copies = []
        for j in range(nl):
            for k in range(1, N_DEV):
                cp = pltpu.make_async_remote_copy(
                    src_ref=src_ref.at[j], dst_ref=mine(j), send_sem=send_sems.at[j, k - 1],
                    recv_sem=recv_sems.at[j, k - 1], device_id=_peer(k), device_id_type=MESH_ID)
                cp.start()
                copies.append(cp)
        for cp in copies:
            cp.wait()
        for cp in local:
            cp.wait()

    return pl.pallas_call(
        body, name=name, in_specs=[_HBM], out_specs=_HBM,
        out_shape=jax.ShapeDtypeStruct(out_shape, shard.dtype),
        scratch_shapes=[pltpu.SemaphoreType.DMA((nl, N_DEV - 1)), pltpu.SemaphoreType.DMA((nl, N_DEV - 1)),
                        pltpu.SemaphoreType.DMA((nl,))],
    )(shard)


def _grad_exchange(dws, *, by_rows, name):
    nl = len(dws)
    if by_rows:
        a, b = dws[0].shape[0] // N_DEV, dws[0].shape[1]
    else:
        a, b = dws[0].shape[1:]

    def body(*refs):
        src_refs, out_ref = refs[:nl], refs[nl]
        send_sems, recv_sems, local_sems = refs[nl + 1:]
        me = _my_index()

        def slab(j, idx):
            if by_rows:
                return src_refs[j].at[pl.ds(pl.multiple_of(idx * a, 16), a), :]
            return src_refs[j].at[idx]

        local = [pltpu.make_async_copy(slab(j, me), out_ref.at[me, j], local_sems.at[j]) for j in range(nl)]
        for cp in local:
            cp.start()
        copies = []
        for j in range(nl):
            for k in range(1, N_DEV):
                cp = pltpu.make_async_remote_copy(
                    src_ref=slab(j, me ^ k), dst_ref=out_ref.at[me, j], send_sem=send_sems.at[j, k - 1],
                    recv_sem=recv_sems.at[j, k - 1], device_id=_peer(k), device_id_type=MESH_ID)
                cp.start()
                copies.append(cp)
        for cp in copies:
            cp.wait()
        for cp in local:
            cp.wait()

    return pl.pallas_call(
        body, name=name, in_specs=[_HBM] * nl, out_specs=_HBM,
        out_shape=jax.ShapeDtypeStruct((N_DEV, nl, a, b), dws[0].dtype),
        scratch_shapes=[pltpu.SemaphoreType.DMA((nl, N_DEV - 1)), pltpu.SemaphoreType.DMA((nl, N_DEV - 1)),
                        pltpu.SemaphoreType.DMA((nl,))],
    )(*dws)


def _adamw_math(w, g, m, v):
    m = ADAM_B1 * m + (1.0 - ADAM_B1) * g
    v = ADAM_B2 * v + (1.0 - ADAM_B2) * (g * g)
    m_hat = m / (1.0 - ADAM_B1 ** ADAM_STEP)
    v_hat = v / (1.0 - ADAM_B2 ** ADAM_STEP)
    delta = -ADAM_LR * (m_hat / (jnp.sqrt(v_hat) + ADAM_EPS) + ADAM_WD * w)
    return delta, m, v


def _adamw(recv, w, m, v, *, name):
    R, C = w.shape
    tr = 128 if C > 1024 else 512
    tr = _tile(R, tr)

    def body(r_ref, w_ref, m_ref, v_ref, g_ref, d_ref, nm_ref, nv_ref):
        g = r_ref[0].astype(F32)
        for s in range(1, N_DEV):
            g = g + r_ref[s].astype(F32)
        delta, nm, nv = _adamw_math(w_ref[...], g, m_ref[...], v_ref[...])
        g_ref[...] = g
        d_ref[...] = delta
        nm_ref[...] = nm
        nv_ref[...] = nv

    row = pl.BlockSpec((tr, C), lambda i: (i, 0))
    return pl.pallas_call(
        body, name=name, grid=(R // tr,),
        in_specs=[pl.BlockSpec((N_DEV, tr, C), lambda i: (0, i, 0)), row, row, row],
        out_specs=[row] * 4, out_shape=[jax.ShapeDtypeStruct((R, C), F32)] * 4,
        compiler_params=_params(("parallel",)),
    )(recv, w, m, v)


def _sum_slots(g8, *, name):
    _, P, C = g8.shape

    def body(r_ref, o_ref):
        g = r_ref[0]
        for s in range(1, N_DEV):
            g = g + r_ref[s]
        o_ref[...] = g

    return pl.pallas_call(
        body, name=name, grid=(1,),
        in_specs=[pl.BlockSpec((N_DEV, P, C), lambda i: (0, 0, 0))],
        out_specs=pl.BlockSpec((P, C), lambda i: (0, 0)),
        out_shape=jax.ShapeDtypeStruct((P, C), F32),
        compiler_params=_params(("arbitrary",)),
    )(g8)


def _adamw_small(g, w, m, v, *, name):
    P, C = w.shape

    def body(g_ref, w_ref, m_ref, v_ref, d_ref, nm_ref, nv_ref):
        delta, nm, nv = _adamw_math(w_ref[...], g_ref[...], m_ref[...], v_ref[...])
        d_ref[...] = delta
        nm_ref[...] = nm
        nv_ref[...] = nv

    full = pl.BlockSpec((P, C), lambda i: (0, 0))
    return pl.pallas_call(
        body, name=name, grid=(1,), in_specs=[full] * 4, out_specs=[full] * 3,
        out_shape=[jax.ShapeDtypeStruct((P, C), F32)] * 3,
        compiler_params=_params(("arbitrary",)),
    )(g, w, m, v)


def _pack(arrays):
    flat = jnp.concatenate([a.reshape(-1) for a in arrays])
    pad = (-flat.shape[0]) % (8 * 128)
    return jnp.pad(flat, (0, pad)).reshape(-1, 128)


def _unpack(buf, shapes):
    flat = buf.reshape(-1)
    out, off = [], 0
    for shp in shapes:
        n = 1
        for s in shp:
            n *= s
        out.append(flat[off:off + n].reshape(shp))
        off += n
    return out


def _expand_matrices():
    h = lax.broadcasted_iota(jnp.int32, (HEAD_LANES, SSM_INNER), 0)
    col = lax.broadcasted_iota(jnp.int32, (HEAD_LANES, SSM_INNER), 1) // SSM_HEAD_DIM
    e = (h == col).astype(BF16)
    return e, e.T


def _pad_heads(v):
    return jnp.pad(v.reshape(1, -1), ((0, 0), (0, HEAD_LANES - v.shape[-1])))


SHARDED_SMALL = ("cv_w_dw", "ssm_w_conv", "ssm_b_conv", "ssm_norm_g", "ffn_w_dw")
REPLICATED_SMALL = ("norm_mix_g", "norm_ffn_g", "norm_final_g", "cv_b_in", "cv_b_dw", "cv_ln_g", "cv_ln_b",
                    "cv_b_out", "ssm_dt_bias", "ssm_a_log", "ssm_d", "ffn_b_dw")
SMALL = REPLICATED_SMALL + SHARDED_SMALL
BIG_COLS = ("cv_w_in", "ssm_w_in", "ffn_w_up")
BIG_ROWS = ("cv_w_out", "ssm_w_out", "ffn_w_down")
WEIGHTS = ("norm_mix_g", "norm_ffn_g", "norm_final_g", "cv_w_in", "cv_b_in", "cv_w_dw", "cv_b_dw", "cv_ln_g",
           "cv_ln_b", "cv_w_out", "cv_b_out", "ssm_w_in", "ssm_w_conv", "ssm_b_conv", "ssm_dt_bias", "ssm_a_log",
           "ssm_d", "ssm_norm_g", "ssm_w_out", "ffn_w_up", "ffn_w_dw", "ffn_b_dw", "ffn_w_down")


def _gather_weights(w):
    gathered = {}
    for n in BIG_COLS:
        shard = w[n].astype(BF16)
        if n == "ssm_w_in":
            shard = jnp.pad(shard, ((0, 0), (0, 0), (0, SSM_SLAB_PAD - SSM_SLAB)))
        gathered[n] = _all_gather(shard, by_rows=False, name="gather_" + n)
    for n in BIG_ROWS:
        gathered[n] = _all_gather(w[n].astype(BF16), by_rows=True, name="gather_" + n)
    small_local = _pack([w[n] for n in SHARDED_SMALL])
    small8 = _all_gather(small_local[None], by_rows=False, name="gather_small")[0]
    full = {}
    per_dev = [_unpack(small8[d], [w[n].shape for n in SHARDED_SMALL]) for d in range(N_DEV)]
    for idx, n in enumerate(SHARDED_SMALL):
        full[n] = jnp.concatenate([per_dev[d][idx] for d in range(N_DEV)], axis=-1)
    for n in REPLICATED_SMALL:
        full[n] = w[n]
    return gathered, full


def _forward_backward(x, target, gathered, full):
    e, et = _expand_matrices()
    grads = {}
    dbig = {n: [] for n in BIG_COLS + BIG_ROWS}

    saved = []
    for i in range(DEPTH):
        j = i // 2
        s = {"x_in": x}
        h1 = _rms_fwd(x, full["norm_mix_g"][i][None], name=f"rms_mix_{i}")
        s["h1"] = h1
        if i % 2 == 0:
            u = _mm_cols_nn(h1, gathered["cv_w_in"], j, bias=full["cv_b_in"][j][None], name=f"cv_in_{i}")
            sact, c = _cv_mid_fwd(u, full["cv_w_dw"][j], full["cv_b_dw"][j][None], full["cv_ln_g"][j][None],
                                  full["cv_ln_b"][j][None], name=f"cv_mid_{i}")
            x = _mm_rows_nn(sact, gathered["cv_w_out"], j, res=x, bias=full["cv_b_out"][j][None], name=f"cv_out_{i}")
            s.update(u=u, c=c, sact=sact)
        else:
            zx = _mm_cols_nn(h1, gathered["ssm_w_in"], j, name=f"ssm_in_{i}")
            z, xpre, dt_raw = _ssm_unpad(zx, name=f"ssm_unpad_{i}")
            xbc = _ssm_conv_fwd(xpre, full["ssm_w_conv"][j], full["ssm_b_conv"][j][None], name=f"ssm_conv_{i}")
            hp = dict(dt_bias=_pad_heads(full["ssm_dt_bias"][j]), a_log=_pad_heads(full["ssm_a_log"][j]),
                      d_x=jnp.repeat(full["ssm_d"][j], SSM_HEAD_DIM)[None])
            y, states = _ssd_fwd(xbc, dt_raw, hp["dt_bias"], hp["a_log"], hp["d_x"], e, et, name=f"ssd_fwd_{i}")
            yn = _gnorm_fwd(y, z, full["ssm_norm_g"][j][None], name=f"gnorm_{i}")
            x = _mm_rows_nn(yn, gathered["ssm_w_out"], j, res=x, name=f"ssm_out_{i}")
            s.update(z=z, xpre=xpre, dt_raw=dt_raw, xbc=xbc, y=y, states=states, yn=yn, hp=hp)
        s["x_mid"] = x
        h2 = _rms_fwd(x, full["norm_ffn_g"][i][None], name=f"rms_ffn_{i}")
        u0 = _mm_cols_nn(h2, gathered["ffn_w_up"], i, name=f"ffn_up_{i}")
        gact = _ffn_mid_fwd(u0, full["ffn_w_dw"][i], full["ffn_b_dw"][i][None], name=f"ffn_mid_{i}")
        x = _mm_rows_nn(gact, gathered["ffn_w_down"], i, res=x, name=f"ffn_down_{i}")
        s.update(h2=h2, u0=u0, gact=gact)
        saved.append(s)

    dx, dg_final, loss = _final_loss(x, full["norm_final_g"][None], target, name="final_loss")
    grads["norm_final_g"] = dg_final[0]

    small_layers = {n: [None] * full[n].shape[0] for n in SMALL if n != "norm_final_g"}
    for i in reversed(range(DEPTH)):
        j = i // 2
        s = saved[i]
        dgact = _mm_rows_nt(dx, gathered["ffn_w_down"], i, name=f"ffn_down_dx_{i}")
        dbig["ffn_w_down"].insert(0, _mm_rows_tn(s["gact"], dx, name=f"ffn_down_dw_{i}"))
        du0, dw_dw, db_dw = _ffn_bwd(dgact, s["u0"], full["ffn_w_dw"][i], full["ffn_b_dw"][i][None], name=f"ffn_bwd_{i}")
        small_layers["ffn_w_dw"][i] = dw_dw
        small_layers["ffn_b_dw"][i] = db_dw[0]
        dh2 = _mm_cols_nt(du0, gathered["ffn_w_up"], i, name=f"ffn_up_dx_{i}")
        dbig["ffn_w_up"].insert(0, _mm_cols_tn(s["h2"], du0, name=f"ffn_up_dw_{i}"))
        dx, dg, colsum = _rms_bwd(s["x_mid"], full["norm_ffn_g"][i][None], dh2, dx, name=f"rms_ffn_bwd_{i}")
        small_layers["norm_ffn_g"][i] = dg[0]
        if i % 2 == 0:
            small_layers["cv_b_out"][j] = colsum[0]
            dsact = _mm_rows_nt(dx, gathered["cv_w_out"], j, name=f"cv_out_dx_{i}")
            dbig["cv_w_out"].insert(0, _mm_rows_tn(s["sact"], dx, name=f"cv_out_dw_{i}"))
            du, dw_dw, db_dw, dlg, dlb, db_in = _cv_bwd(dsact, s["c"], s["u"], full["cv_w_dw"][j], full["cv_ln_g"][j][None],
                                                        full["cv_ln_b"][j][None], name=f"cv_bwd_{i}")
            small_layers["cv_w_dw"][j] = dw_dw
            small_layers["cv_b_dw"][j] = db_dw[0]
            small_layers["cv_ln_g"][j] = dlg[0]
            small_layers["cv_ln_b"][j] = dlb[0]
            small_layers["cv_b_in"][j] = db_in[0]
            dh1 = _mm_cols_nt(du, gathered["cv_w_in"], j, name=f"cv_in_dx_{i}")
            dbig["cv_w_in"].insert(0, _mm_cols_tn(s["h1"], du, name=f"cv_in_dw_{i}"))
        else:
            hp = s["hp"]
            dyn = _mm_rows_nt(dx, gathered["ssm_w_out"], j, name=f"ssm_out_dx_{i}")
            dbig["ssm_w_out"].insert(0, _mm_rows_tn(s["yn"], dx, name=f"ssm_out_dw_{i}"))
            dy, dz, dng = _gnorm_bwd(dyn, s["y"], s["z"], full["ssm_norm_g"][j][None], name=f"gnorm_bwd_{i}")
            small_layers["ssm_norm_g"][j] = dng[0]
            dxbc, ddt_raw, dbias, dalog, dd = _ssd_bwd(dy, s["xbc"], s["dt_raw"], s["states"], hp["dt_bias"], hp["a_log"],
                                                      hp["d_x"], e, et, name=f"ssd_bwd_{i}")
            small_layers["ssm_dt_bias"][j] = dbias[0, :SSM_HEADS]
            small_layers["ssm_a_log"][j] = dalog[0, :SSM_HEADS]
            small_layers["ssm_d"][j] = dd[0, :SSM_HEADS]
            dxpre, dw_conv, db_conv = _ssm_conv_bwd(dxbc, s["xpre"], full["ssm_w_conv"][j], full["ssm_b_conv"][j][None],
                                                    name=f"ssm_conv_bwd_{i}")
            small_layers["ssm_w_conv"][j] = dw_conv
            small_layers["ssm_b_conv"][j] = db_conv[0]
            dzx = _ssm_pad(dz, dxpre, ddt_raw, name=f"ssm_pad_{i}")
            dh1 = _mm_cols_nt(dzx, gathered["ssm_w_in"], j, name=f"ssm_in_dx_{i}")
            dbig["ssm_w_in"].insert(0, _mm_cols_tn(s["h1"], dzx, name=f"ssm_in_dw_{i}"))
        dx, dg, _ = _rms_bwd(s["x_in"], full["norm_mix_g"][i][None], dh1, dx, name=f"rms_mix_bwd_{i}")
        small_layers["norm_mix_g"][i] = dg[0]
    for n, layers in small_layers.items():
        grads[n] = jnp.stack(layers)
    return loss, dx, dbig, grads


def _exchange_and_update(dbig, grads, w, m, v):
    me = _my_index()
    out = {}
    for n in BIG_COLS + BIG_ROWS:
        recv = _grad_exchange(dbig[n], by_rows=n in BIG_ROWS, name="exchange_" + n)
        nl, a, b = recv.shape[1:]
        pad = ((0, 0), (0, 0), (0, b - w[n].shape[-1]))
        args = [jnp.pad(t, pad).reshape(nl * a, b) for t in (w[n], m[n], v[n])]
        res = _adamw(recv.reshape(N_DEV, nl * a, b), *args, name="adamw_" + n)
        out[n] = [r.reshape(nl, a, b)[..., :w[n].shape[-1]] for r in res]

    small_part = _pack([grads[n] for n in SMALL])
    small_all = _all_gather(small_part[None], by_rows=False, name="gather_small_grads")[0]
    small_sum = _sum_slots(small_all, name="sum_small_grads")
    gfull = dict(zip(SMALL, _unpack(small_sum, [grads[n].shape for n in SMALL])))
    glocal = []
    for n in SMALL:
        g = gfull[n]
        if n in SHARDED_SMALL:
            cols = w[n].shape[-1]
            g = lax.dynamic_slice_in_dim(g, me * cols, cols, axis=g.ndim - 1)
        glocal.append(g)
    packed = [_pack(glocal)] + [_pack([t[n] for n in SMALL]) for t in (w, m, v)]
    res = _adamw_small(*packed, name="adamw_small")
    shapes = [w[n].shape for n in SMALL]
    unpacked = [_unpack(r, shapes) for r in res]
    for idx, n in enumerate(SMALL):
        out[n] = [glocal[idx]] + [u[idx] for u in unpacked]
    return out


def _train_step(x, target, w, m, v):
    gathered, full = _gather_weights(w)
    loss, dx, dbig, grads = _forward_backward(x, target, gathered, full)
    out = _exchange_and_update(dbig, grads, w, m, v)
    return lax.psum(loss[0, 0], AXES), dx, out


def kernel(x, norm_mix_g, norm_ffn_g, norm_final_g, cv_w_in, cv_b_in, cv_w_dw, cv_b_dw, cv_ln_g, cv_ln_b, cv_w_out, cv_b_out, ssm_w_in, ssm_w_conv, ssm_b_conv, ssm_dt_bias, ssm_a_log, ssm_d, ssm_norm_g, ssm_w_out, ffn_w_up, ffn_w_dw, ffn_b_dw, ffn_w_down, loss_target, m_norm_mix_g, m_norm_ffn_g, m_norm_final_g, m_cv_w_in, m_cv_b_in, m_cv_w_dw, m_cv_b_dw, m_cv_ln_g, m_cv_ln_b, m_cv_w_out, m_cv_b_out, m_ssm_w_in, m_ssm_w_conv, m_ssm_b_conv, m_ssm_dt_bias, m_ssm_a_log, m_ssm_d, m_ssm_norm_g, m_ssm_w_out, m_ffn_w_up, m_ffn_w_dw, m_ffn_b_dw, m_ffn_w_down, v_norm_mix_g, v_norm_ffn_g, v_norm_final_g, v_cv_w_in, v_cv_b_in, v_cv_w_dw, v_cv_b_dw, v_cv_ln_g, v_cv_ln_b, v_cv_w_out, v_cv_b_out, v_ssm_w_in, v_ssm_w_conv, v_ssm_b_conv, v_ssm_dt_bias, v_ssm_a_log, v_ssm_d, v_ssm_norm_g, v_ssm_w_out, v_ffn_w_up, v_ffn_w_dw, v_ffn_b_dw, v_ffn_w_down):
    args = locals()
    w = {n: args[n] for n in WEIGHTS}
    m = {n: args["m_" + n] for n in WEIGHTS}
    v = {n: args["v_" + n] for n in WEIGHTS}
    loss, dx, out = _train_step(x[0], loss_target[0], w, m, v)
    return (loss, dx[None], *[out[n][0] for n in WEIGHTS], *[out[n][1] for n in WEIGHTS],
            *[out[n][2] for n in WEIGHTS], *[out[n][3] for n in WEIGHTS])
```

```python
import functools

import jax
import jax.numpy as jnp
from jax import lax
from jax.experimental import pallas as pl
from jax.experimental.pallas import tpu as pltpu

F32, BF16 = jnp.float32, jnp.bfloat16
AXES = ("x", "y", "c")
N_DEV = 8
MESH_ID = pl.DeviceIdType.MESH

D_MODEL = 2048
DEPTH = 4
CHUNK = 64
CONV_K = 31
SSM_INNER = 4096
SSM_HEADS = 64
SSM_HEAD_DIM = 64
SSM_GROUPS = 8
SSM_GROUP_W = SSM_INNER // SSM_GROUPS
SSM_STATE = 128
SSM_CONV_K = 4
SSM_CONV_DIM = SSM_INNER + 2 * SSM_GROUPS * SSM_STATE
SSM_IN_DIM = SSM_INNER + SSM_CONV_DIM + SSM_HEADS
SSM_SLAB = SSM_IN_DIM // N_DEV
SSM_SLAB_PAD = 1408
SSM_COMPACT = 1280 * (N_DEV - 1) + SSM_SLAB_PAD
FFN_HIDDEN = 5632
FFN_K = 3
HEAD_LANES = 128
RMS_EPS = 1e-6
LN_EPS = 1e-5
ADAM_LR, ADAM_B1, ADAM_B2, ADAM_EPS, ADAM_WD, ADAM_STEP = 0.001, 0.9, 0.999, 1e-08, 0.01, 10

_DIMS = {
    "nn": (((1,), (0,)), ((), ())),
    "nt": (((1,), (1,)), ((), ())),
    "tn": (((0,), (0,)), ((), ())),
}


def _params(sem=None, vmem_mb=48):
    return pltpu.CompilerParams(dimension_semantics=sem, vmem_limit_bytes=vmem_mb << 20)


_HBM = pl.BlockSpec(memory_space=pltpu.HBM)
_ANY = pl.BlockSpec(memory_space=pl.ANY)
_SEM = pl.BlockSpec(memory_space=pltpu.SEMAPHORE)


def _dot(a, b, mode="nn"):
    return lax.dot_general(a, b, _DIMS[mode], preferred_element_type=F32)


def _split3(x):
    hi = x.astype(BF16)
    r1 = x - hi.astype(F32)
    mid = r1.astype(BF16)
    lo = (r1 - mid.astype(F32)).astype(BF16)
    return hi, mid, lo


def _dot3_l(x, m, mode="nn"):
    hi, mid, lo = _split3(x)
    return _dot(hi, m, mode) + _dot(mid, m, mode) + _dot(lo, m, mode)


def _dot3_r(m, x, mode="nn"):
    hi, mid, lo = _split3(x)
    return _dot(m, hi, mode) + _dot(m, mid, mode) + _dot(m, lo, mode)


def _sigmoid(x):
    return jax.nn.sigmoid(x)


def _dsilu(x, sg):
    return sg * (1.0 + x * (1.0 - sg))


def _softplus(x):
    return jnp.maximum(x, 0.0) + jnp.log(1.0 + jnp.exp(-jnp.abs(x)))


def _tile(n, pref):
    return min(n, pref)


def _matmul(a, b, *, mode, grid, a_spec, b_spec, o_spec, o_block, out_shape, nk, name,
            extras=(), epilogue=None, after=None, vmem_mb=48):
    n_extra = len(extras)
    if after is not None:
        extras = tuple(extras) + ((after, _ANY),)
    n_in = len(extras)

    def body(a_ref, b_ref, *rest):
        extra_refs = rest[:n_extra]
        o_ref = rest[n_in]
        part = _dot(a_ref[...].astype(BF16), b_ref[...].astype(BF16), mode)

        def finish(acc):
            if epilogue is not None:
                acc = epilogue(acc, *[r[...] for r in extra_refs])
            o_ref[...] = acc.astype(o_ref.dtype)

        if nk == 1:
            finish(part)
        else:
            acc_ref = rest[n_in + 1]
            k = pl.program_id(len(grid) - 1)

            @pl.when(k == 0)
            def _():
                acc_ref[...] = part

            @pl.when(k > 0)
            def _():
                acc_ref[...] += part

            @pl.when(k == nk - 1)
            def _():
                finish(acc_ref[...])

    scratch = [] if nk == 1 else [pltpu.VMEM(o_block, F32)]
    sem = ("parallel",) * (len(grid) - 1) + ("arbitrary",)
    return pl.pallas_call(
        body, name=name, grid=grid,
        in_specs=[a_spec, b_spec] + [s for _, s in extras],
        out_specs=o_spec, out_shape=out_shape, scratch_shapes=scratch,
        compiler_params=_params(sem, vmem_mb),
    )(a, b, *[x for x, _ in extras])


def _mm_cols_nn(h, wg, *, name, bias=None, after=None):
    L, K = h.shape
    n = wg.shape[-1]
    tm = _tile(L, 512)
    extras, epi = (), None
    if bias is not None:
        extras = ((bias, pl.BlockSpec((1, n), lambda s, i: (0, s))),)
        epi = lambda acc, b: acc + b
    return _matmul(
        h, wg, mode="nn", grid=(N_DEV, L // tm),
        a_spec=pl.BlockSpec((tm, K), lambda s, i: (i, 0)),
        b_spec=pl.BlockSpec((None, K, n), lambda s, i: (s, 0, 0)),
        o_spec=pl.BlockSpec((tm, n), lambda s, i: (i, s)), o_block=(tm, n),
        out_shape=jax.ShapeDtypeStruct((L, N_DEV * n), F32), nk=1, name=name,
        extras=extras, epilogue=epi, after=after)


def _mm_cols_nt(du, wg, *, name):
    L = du.shape[0]
    K, n = wg.shape[-2:]
    tm = _tile(L, 512)
    return _matmul(
        du, wg, mode="nt", grid=(L // tm, N_DEV),
        a_spec=pl.BlockSpec((tm, n), lambda i, s: (i, s)),
        b_spec=pl.BlockSpec((None, K, n), lambda i, s: (s, 0, 0)),
        o_spec=pl.BlockSpec((tm, K), lambda i, s: (i, 0)), o_block=(tm, K),
        out_shape=jax.ShapeDtypeStruct((L, K), F32), nk=N_DEV, name=name)


def _mm_cols_tn(h, du, *, name):
    L, K = h.shape
    n = du.shape[1] // N_DEV
    tm = _tile(L, 512)
    kh = K // 2
    return _matmul(
        h, du, mode="tn", grid=(N_DEV, 2, L // tm),
        a_spec=pl.BlockSpec((tm, kh), lambda s, q, t: (t, q)),
        b_spec=pl.BlockSpec((tm, n), lambda s, q, t: (t, s)),
        o_spec=pl.BlockSpec((None, kh, n), lambda s, q, t: (s, q, 0)), o_block=(kh, n),
        out_shape=jax.ShapeDtypeStruct((N_DEV, K, n), BF16), nk=L // tm, name=name)


def _mm_rows_nn(a, wg, *, res, name, bias=None):
    L, Kw = a.shape
    N = wg.shape[-1]
    tm, tk = _tile(L, 512), 512
    extras = [(res, pl.BlockSpec((tm, N), lambda i, k: (i, 0)))]
    if bias is not None:
        extras.append((bias, pl.BlockSpec((1, N), lambda i, k: (0, 0))))
        epi = lambda acc, r, b: acc + r + b
    else:
        epi = lambda acc, r: acc + r
    return _matmul(
        a, wg, mode="nn", grid=(L // tm, Kw // tk),
        a_spec=pl.BlockSpec((tm, tk), lambda i, k: (i, k)),
        b_spec=pl.BlockSpec((tk, N), lambda i, k: (k, 0)),
        o_spec=pl.BlockSpec((tm, N), lambda i, k: (i, 0)), o_block=(tm, N),
        out_shape=jax.ShapeDtypeStruct((L, N), F32), nk=Kw // tk, name=name,
        extras=tuple(extras), epilogue=epi)


def _mm_rows_nt(dy, wg, *, name, after=None):
    L, N = dy.shape
    Kw = wg.shape[-2]
    tm, tn = _tile(L, 512), 512
    return _matmul(
        dy, wg, mode="nt", grid=(L // tm, Kw // tn),
        a_spec=pl.BlockSpec((tm, N), lambda i, q: (i, 0)),
        b_spec=pl.BlockSpec((tn, N), lambda i, q: (q, 0)),
        o_spec=pl.BlockSpec((tm, tn), lambda i, q: (i, q)), o_block=(tm, tn),
        out_shape=jax.ShapeDtypeStruct((L, Kw), F32), nk=1, name=name, after=after)


def _mm_rows_tn(a, dy, *, name):
    L, Kw = a.shape
    N = dy.shape[1]
    tm, tq = _tile(L, 512), 512
    return _matmul(
        a, dy, mode="tn", grid=(Kw // tq, L // tm),
        a_spec=pl.BlockSpec((tm, tq), lambda q, t: (t, q)),
        b_spec=pl.BlockSpec((tm, N), lambda q, t: (t, 0)),
        o_spec=pl.BlockSpec((tq, N), lambda q, t: (q, 0)), o_block=(tq, N),
        out_shape=jax.ShapeDtypeStruct((Kw, N), BF16), nk=L // tm, name=name)


def _rms_fwd(x, g, *, name):
    L, Dm = x.shape
    tm = _tile(L, 256)

    def body(x_ref, g_ref, h_ref):
        xv = x_ref[...]
        r = lax.rsqrt(jnp.mean(xv * xv, axis=-1, keepdims=True) + RMS_EPS)
        h_ref[...] = (xv * r * g_ref[...]).astype(BF16)

    return pl.pallas_call(
        body, name=name, grid=(L // tm,),
        in_specs=[pl.BlockSpec((tm, Dm), lambda i: (i, 0)), pl.BlockSpec((1, Dm), lambda i: (0, 0))],
        out_specs=pl.BlockSpec((tm, Dm), lambda i: (i, 0)),
        out_shape=jax.ShapeDtypeStruct((L, Dm), BF16),
        compiler_params=_params(("parallel",)),
    )(x, g)


def _rms_bwd(x, g, dh, dres, *, name):
    L, Dm = x.shape
    tm = _tile(L, 256)

    def body(x_ref, g_ref, dh_ref, dres_ref, dx_ref, dg_ref, cs_ref):
        i = pl.program_id(0)
        xv = x_ref[...]
        r = lax.rsqrt(jnp.mean(xv * xv, axis=-1, keepdims=True) + RMS_EPS)
        xh = xv * r
        dh = dh_ref[...]
        dxh = dh * g_ref[...]
        dx = dres_ref[...] + r * (dxh - xh * jnp.mean(dxh * xh, axis=-1, keepdims=True))
        dx_ref[...] = dx

        @pl.when(i == 0)
        def _():
            dg_ref[...] = jnp.zeros_like(dg_ref)
            cs_ref[...] = jnp.zeros_like(cs_ref)

        dg_ref[...] += jnp.sum(dh * xh, axis=0, keepdims=True)
        cs_ref[...] += jnp.sum(dx, axis=0, keepdims=True)

    row = pl.BlockSpec((tm, Dm), lambda i: (i, 0))
    vec = pl.BlockSpec((1, Dm), lambda i: (0, 0))
    return pl.pallas_call(
        body, name=name, grid=(L // tm,),
        in_specs=[row, vec, row, row], out_specs=[row, vec, vec],
        out_shape=[jax.ShapeDtypeStruct((L, Dm), F32), jax.ShapeDtypeStruct((1, Dm), F32),
                   jax.ShapeDtypeStruct((1, Dm), F32)],
        compiler_params=_params(("arbitrary",)),
    )(x, g, dh, dres)


def _final_loss(x, g, target, *, name):
    L, Dm = x.shape
    tm = _tile(L, 256)

    def body(x_ref, g_ref, t_ref, dx_ref, dg_ref, loss_ref):
        i = pl.program_id(0)
        xv = x_ref[...]
        gv = g_ref[...]
        r = lax.rsqrt(jnp.mean(xv * xv, axis=-1, keepdims=True) + RMS_EPS)
        xh = xv * r
        err = xh * gv - t_ref[...]
        dy = err * (1.0 / Dm)
        dxh = dy * gv
        dx_ref[...] = r * (dxh - xh * jnp.mean(dxh * xh, axis=-1, keepdims=True))

        @pl.when(i == 0)
        def _():
            dg_ref[...] = jnp.zeros_like(dg_ref)
            loss_ref[...] = jnp.zeros_like(loss_ref)

        dg_ref[...] += jnp.sum(dy * xh, axis=0, keepdims=True)
        loss_ref[...] += 0.5 * jnp.sum(jnp.mean(err * err, axis=-1, keepdims=True), axis=0, keepdims=True)

    row = pl.BlockSpec((tm, Dm), lambda i: (i, 0))
    vec = pl.BlockSpec((1, Dm), lambda i: (0, 0))
    return pl.pallas_call(
        body, name=name, grid=(L // tm,),
        in_specs=[row, vec, row],
        out_specs=[row, vec, pl.BlockSpec((1, 128), lambda i: (0, 0))],
        out_shape=[jax.ShapeDtypeStruct((L, Dm), F32), jax.ShapeDtypeStruct((1, Dm), F32),
                   jax.ShapeDtypeStruct((1, 128), F32)],
        compiler_params=_params(("arbitrary",)),
    )(x, g, target)


def _conv_from_ext(ext_ref, w_ref, bias, taps, halo, rows):
    acc = jnp.broadcast_to(bias, (rows, ext_ref.shape[1]))
    for j in range(taps):
        acc = acc + w_ref[taps - 1 - j:taps - j, :] * ext_ref[halo - j:halo - j + rows, :]
    return acc


def _conv_back(dext_ref, x_cur, w_ref, dw_ref, taps, rows):
    dx = jnp.zeros((rows, dext_ref.shape[1]), F32)
    for j in range(taps):
        sh = dext_ref[j:j + rows, :]
        dx = dx + w_ref[taps - 1 - j:taps - j, :] * sh
        dw_ref[taps - 1 - j:taps - j, :] += jnp.sum(x_cur * sh, axis=0, keepdims=True)
    return dx


def _prev_blk(i, r):
    return jnp.maximum(i * r - 1, 0)


def _next_blk(i, r, nblk):
    return jnp.minimum((i + 1) * r, nblk - 1)


CV_HALO = 32


def _cv_mid_fwd(u, w_dw, b_dw, ln_g, ln_b, *, name):
    L = u.shape[0]
    Dm = D_MODEL
    tm = _tile(L, 256)
    r = tm // CV_HALO

    def body(a_ref, g_ref, ah_ref, gh_ref, w_ref, bdw_ref, lg_ref, lb_ref, s_ref, c_ref, ext_ref):
        i = pl.program_id(0)
        keep = (i > 0).astype(F32)
        ext_ref[0:CV_HALO, :] = ah_ref[...] * _sigmoid(gh_ref[...]) * keep
        ext_ref[CV_HALO:CV_HALO + tm, :] = a_ref[...] * _sigmoid(g_ref[...])
        c = _conv_from_ext(ext_ref, w_ref, bdw_ref[...], CONV_K, CV_HALO, tm)
        c_ref[...] = c
        mu = jnp.mean(c, axis=-1, keepdims=True)
        xc = c - mu
        rstd = lax.rsqrt(jnp.mean(xc * xc, axis=-1, keepdims=True) + LN_EPS)
        l = xc * rstd * lg_ref[...] + lb_ref[...]
        s_ref[...] = (l * _sigmoid(l)).astype(BF16)

    vec = pl.BlockSpec((1, Dm), lambda i: (0, 0))
    return pl.pallas_call(
        body, name=name, grid=(L // tm,),
        in_specs=[pl.BlockSpec((tm, Dm), lambda i: (i, 0)), pl.BlockSpec((tm, Dm), lambda i: (i, 1)),
                  pl.BlockSpec((CV_HALO, Dm), lambda i: (_prev_blk(i, r), 0)),
                  pl.BlockSpec((CV_HALO, Dm), lambda i: (_prev_blk(i, r), 1)),
                  pl.BlockSpec((CONV_K, Dm), lambda i: (0, 0)), vec, vec, vec],
        out_specs=[pl.BlockSpec((tm, Dm), lambda i: (i, 0)), pl.BlockSpec((tm, Dm), lambda i: (i, 0))],
        out_shape=[jax.ShapeDtypeStruct((L, Dm), BF16), jax.ShapeDtypeStruct((L, Dm), F32)],
        scratch_shapes=[pltpu.VMEM((CV_HALO + tm, Dm), F32)],
        compiler_params=_params(("parallel",)),
    )(u, u, u, u, w_dw, b_dw, ln_g, ln_b)


def _cv_bwd(ds, c, u, w_dw, ln_g, ln_b, *, name):
    L = u.shape[0]
    Dm = D_MODEL
    tm = _tile(L, 128)
    r = tm // CV_HALO
    nt = L // tm
    nblk = L // CV_HALO
    ext_rows = tm + CV_HALO

    def body(ds_ref, dsn_ref, c_ref, cn_ref, a_ref, g_ref, w_ref, lg_ref, lb_ref,
             du_ref, dw_ref, dbdw_ref, dlg_ref, dlb_ref, dbin_ref, dsx_ref, cx_ref, dext_ref):
        i = pl.program_id(0)

        @pl.when(i == 0)
        def _():
            for ref in (dw_ref, dbdw_ref, dlg_ref, dlb_ref, dbin_ref):
                ref[...] = jnp.zeros_like(ref)

        keep = (i < nt - 1).astype(F32)
        dsx_ref[0:tm, :] = ds_ref[...]
        dsx_ref[tm:ext_rows, :] = dsn_ref[...] * keep
        cx_ref[0:tm, :] = c_ref[...]
        cx_ref[tm:ext_rows, :] = cn_ref[...]
        cv = cx_ref[...]
        mu = jnp.mean(cv, axis=-1, keepdims=True)
        xc = cv - mu
        rstd = lax.rsqrt(jnp.mean(xc * xc, axis=-1, keepdims=True) + LN_EPS)
        nrm = xc * rstd
        lg = lg_ref[...]
        l = nrm * lg + lb_ref[...]
        dl = dsx_ref[...] * _dsilu(l, _sigmoid(l))
        dn = dl * lg
        dc = rstd * (dn - jnp.mean(dn, axis=-1, keepdims=True)
                     - nrm * jnp.mean(dn * nrm, axis=-1, keepdims=True))
        dext_ref[...] = dc
        dlg_ref[...] += jnp.sum((dl * nrm)[0:tm], axis=0, keepdims=True)
        dlb_ref[...] += jnp.sum(dl[0:tm], axis=0, keepdims=True)
        dbdw_ref[...] += jnp.sum(dc[0:tm], axis=0, keepdims=True)

        av, gv = a_ref[...], g_ref[...]
        sg = _sigmoid(gv)
        dv = _conv_back(dext_ref, av * sg, w_ref, dw_ref, CONV_K, tm)
        da = dv * sg
        dgate = dv * av * sg * (1.0 - sg)
        du_ref[:, 0:Dm] = da.astype(BF16)
        du_ref[:, Dm:2 * Dm] = dgate.astype(BF16)
        dbin_ref[:, 0:Dm] += jnp.sum(da, axis=0, keepdims=True)
        dbin_ref[:, Dm:2 * Dm] += jnp.sum(dgate, axis=0, keepdims=True)

    row = pl.BlockSpec((tm, Dm), lambda i: (i, 0))
    nxt = pl.BlockSpec((CV_HALO, Dm), lambda i: (_next_blk(i, r, nblk), 0))
    vec = pl.BlockSpec((1, Dm), lambda i: (0, 0))
    return pl.pallas_call(
        body, name=name, grid=(nt,),
        in_specs=[row, nxt, row, nxt, row, pl.BlockSpec((tm, Dm), lambda i: (i, 1)),
                  pl.BlockSpec((CONV_K, Dm), lambda i: (0, 0)), vec, vec],
        out_specs=[pl.BlockSpec((tm, 2 * Dm), lambda i: (i, 0)), pl.BlockSpec((CONV_K, Dm), lambda i: (0, 0)),
                   vec, vec, vec, pl.BlockSpec((1, 2 * Dm), lambda i: (0, 0))],
        out_shape=[jax.ShapeDtypeStruct((L, 2 * Dm), BF16), jax.ShapeDtypeStruct((CONV_K, Dm), F32),
                   jax.ShapeDtypeStruct((1, Dm), F32), jax.ShapeDtypeStruct((1, Dm), F32),
                   jax.ShapeDtypeStruct((1, Dm), F32), jax.ShapeDtypeStruct((1, 2 * Dm), F32)],
        scratch_shapes=[pltpu.VMEM((ext_rows, Dm), F32)] * 3,
        compiler_params=_params(("arbitrary",)),
    )(ds, ds, c, c, u, u, w_dw, ln_g, ln_b)


FFN_TC = 512
FFN_NJ = FFN_HIDDEN // FFN_TC
HALO8 = 8


def _ffn_mid_fwd(u0, w_dw, b_dw, *, name):
    L = u0.shape[0]
    tm = _tile(L, 256)
    r = tm // HALO8
    tc, nj = FFN_TC, FFN_NJ

    def body(g_ref, v_ref, gh_ref, vh_ref, wg_ref, wv_ref, bg_ref, bv_ref, o_ref, gx_ref, vx_ref):
        keep = (pl.program_id(0) > 0).astype(F32)
        gx_ref[0:HALO8, :] = gh_ref[...] * keep
        gx_ref[HALO8:HALO8 + tm, :] = g_ref[...]
        vx_ref[0:HALO8, :] = vh_ref[...] * keep
        vx_ref[HALO8:HALO8 + tm, :] = v_ref[...]
        ug = _conv_from_ext(gx_ref, wg_ref, bg_ref[...], FFN_K, HALO8, tm)
        uv = _conv_from_ext(vx_ref, wv_ref, bv_ref[...], FFN_K, HALO8, tm)
        o_ref[...] = (ug * _sigmoid(ug) * uv).astype(BF16)

    return pl.pallas_call(
        body, name=name, grid=(L // tm, nj),
        in_specs=[pl.BlockSpec((tm, tc), lambda i, j: (i, j)), pl.BlockSpec((tm, tc), lambda i, j: (i, j + nj)),
                  pl.BlockSpec((HALO8, tc), lambda i, j: (_prev_blk(i, r), j)),
                  pl.BlockSpec((HALO8, tc), lambda i, j: (_prev_blk(i, r), j + nj)),
                  pl.BlockSpec((FFN_K, tc), lambda i, j: (0, j)), pl.BlockSpec((FFN_K, tc), lambda i, j: (0, j + nj)),
                  pl.BlockSpec((1, tc), lambda i, j: (0, j)), pl.BlockSpec((1, tc), lambda i, j: (0, j + nj))],
        out_specs=pl.BlockSpec((tm, tc), lambda i, j: (i, j)),
        out_shape=jax.ShapeDtypeStruct((L, FFN_HIDDEN), BF16),
        scratch_shapes=[pltpu.VMEM((HALO8 + tm, tc), F32)] * 2,
        compiler_params=_params(("parallel", "parallel")),
    )(u0, u0, u0, u0, w_dw, w_dw, b_dw, b_dw)


def _ffn_bwd(dg, u0, w_dw, b_dw, *, name):
    L = u0.shape[0]
    tm = _tile(L, 256)
    r = tm // HALO8
    nt = L // tm
    nblk = L // HALO8
    tc, nj = FFN_TC, FFN_NJ
    rows = tm + HALO8

    def body(tp_ref, tc_ref, tn_ref, pp_ref, pc_ref, pn_ref, dg_ref, dgn_ref, wt_ref, wp_ref, bt_ref, bp_ref,
             du0_ref, dw_ref, db_ref, tx_ref, px_ref, dext_ref):
        jc, i = pl.program_id(0), pl.program_id(1)

        @pl.when(i == 0)
        def _():
            dw_ref[...] = jnp.zeros_like(dw_ref)
            db_ref[...] = jnp.zeros_like(db_ref)

        keep_prev = (i > 0).astype(F32)
        keep_next = (i < nt - 1).astype(F32)
        for x_ref, p, cur, n in ((tx_ref, tp_ref, tc_ref, tn_ref), (px_ref, pp_ref, pc_ref, pn_ref)):
            x_ref[0:HALO8, :] = p[...] * keep_prev
            x_ref[HALO8:HALO8 + tm, :] = cur[...]
            x_ref[HALO8 + tm:HALO8 + rows, :] = n[...]
        ut = _conv_from_ext(tx_ref, wt_ref, bt_ref[...], FFN_K, HALO8, rows)
        up = _conv_from_ext(px_ref, wp_ref, bp_ref[...], FFN_K, HALO8, rows)
        dgx = jnp.concatenate([dg_ref[...], dgn_ref[...] * keep_next], axis=0)
        sgt = _sigmoid(ut)
        as_gate = dgx * up * _dsilu(ut, sgt)
        as_val = dgx * (up * _sigmoid(up))
        du = jnp.where(jc < nj, as_gate, as_val)
        dext_ref[...] = du
        db_ref[...] += jnp.sum(du[0:tm], axis=0, keepdims=True)
        du0_ref[...] = _conv_back(dext_ref, tc_ref[...], wt_ref, dw_ref, FFN_K, tm).astype(BF16)

    def partner(jc):
        return (jc + nj) % (2 * nj)

    cur = lambda f: pl.BlockSpec((tm, tc), lambda jc, i: (i, f(jc)))
    prv = lambda f: pl.BlockSpec((HALO8, tc), lambda jc, i: (_prev_blk(i, r), f(jc)))
    nxt = lambda f: pl.BlockSpec((HALO8, tc), lambda jc, i: (_next_blk(i, r, nblk), f(jc)))
    ident = lambda jc: jc
    half = lambda jc: jc % nj
    return pl.pallas_call(
        body, name=name, grid=(2 * nj, nt),
        in_specs=[prv(ident), cur(ident), nxt(ident), prv(partner), cur(partner), nxt(partner),
                  cur(half), nxt(half),
                  pl.BlockSpec((FFN_K, tc), lambda jc, i: (0, jc)), pl.BlockSpec((FFN_K, tc), lambda jc, i: (0, partner(jc))),
                  pl.BlockSpec((1, tc), lambda jc, i: (0, jc)), pl.BlockSpec((1, tc), lambda jc, i: (0, partner(jc)))],
        out_specs=[pl.BlockSpec((tm, tc), lambda jc, i: (i, jc)), pl.BlockSpec((FFN_K, tc), lambda jc, i: (0, jc)),
                   pl.BlockSpec((1, tc), lambda jc, i: (0, jc))],
        out_shape=[jax.ShapeDtypeStruct((L, 2 * FFN_HIDDEN), BF16), jax.ShapeDtypeStruct((FFN_K, 2 * FFN_HIDDEN), F32),
                   jax.ShapeDtypeStruct((1, 2 * FFN_HIDDEN), F32)],
        scratch_shapes=[pltpu.VMEM((HALO8 + rows, tc), F32)] * 2 + [pltpu.VMEM((rows, tc), F32)],
        compiler_params=_params(("parallel", "arbitrary")),
    )(u0, u0, u0, u0, u0, u0, dg, dg, w_dw, w_dw, b_dw, b_dw)


def _ssm_unpad(zx_pad, *, name):
    L = zx_pad.shape[0]
    tm = _tile(L, 128)

    def body(p_ref, z_ref, xbc_ref, dt_ref, comp_ref):
        comp_ref[...] = jnp.zeros_like(comp_ref)
        for d in range(N_DEV):
            slab = p_ref[:, d * SSM_SLAB_PAD:(d + 1) * SSM_SLAB_PAD]
            if d:
                slab = pltpu.roll(slab, 8 * d, axis=1)
            comp_ref[:, 1280 * d:1280 * d + SSM_SLAB_PAD] += slab
        z_ref[...] = comp_ref[:, 0:SSM_INNER]
        xbc_ref[...] = comp_ref[:, SSM_INNER:SSM_INNER + SSM_CONV_DIM]
        dt_ref[...] = comp_ref[:, SSM_INNER + SSM_CONV_DIM:SSM_COMPACT]

    return pl.pallas_call(
        body, name=name, grid=(L // tm,),
        in_specs=[pl.BlockSpec((tm, N_DEV * SSM_SLAB_PAD), lambda i: (i, 0))],
        out_specs=[pl.BlockSpec((tm, SSM_INNER), lambda i: (i, 0)), pl.BlockSpec((tm, SSM_CONV_DIM), lambda i: (i, 0)),
                   pl.BlockSpec((tm, HEAD_LANES), lambda i: (i, 0))],
        out_shape=[jax.ShapeDtypeStruct((L, SSM_INNER), F32), jax.ShapeDtypeStruct((L, SSM_CONV_DIM), F32),
                   jax.ShapeDtypeStruct((L, HEAD_LANES), F32)],
        scratch_shapes=[pltpu.VMEM((tm, SSM_COMPACT), F32)],
        compiler_params=_params(("parallel",)),
    )(zx_pad)


def _ssm_pad(dz, dxbc, ddt, *, name):
    L = dz.shape[0]
    tm = _tile(L, 128)

    def body(dz_ref, dx_ref, dt_ref, p_ref, comp_ref):
        comp_ref[:, 0:SSM_INNER] = dz_ref[...]
        comp_ref[:, SSM_INNER:SSM_INNER + SSM_CONV_DIM] = dx_ref[...]
        lane = lax.broadcasted_iota(jnp.int32, (tm, HEAD_LANES), 1)
        comp_ref[:, SSM_INNER + SSM_CONV_DIM:SSM_COMPACT] = jnp.where(lane < SSM_HEADS, dt_ref[...], 0.0)
        col = lax.broadcasted_iota(jnp.int32, (tm, SSM_SLAB_PAD), 1)
        for d in range(N_DEV):
            win = comp_ref[:, 1280 * d:1280 * d + SSM_SLAB_PAD]
            if d:
                win = pltpu.roll(win, SSM_SLAB_PAD - 8 * d, axis=1)
            p_ref[:, d * SSM_SLAB_PAD:(d + 1) * SSM_SLAB_PAD] = jnp.where(col < SSM_SLAB, win, 0.0).astype(BF16)

    return pl.pallas_call(
        body, name=name, grid=(L // tm,),
        in_specs=[pl.BlockSpec((tm, SSM_INNER), lambda i: (i, 0)), pl.BlockSpec((tm, SSM_CONV_DIM), lambda i: (i, 0)),
                  pl.BlockSpec((tm, HEAD_LANES), lambda i: (i, 0))],
        out_specs=pl.BlockSpec((tm, N_DEV * SSM_SLAB_PAD), lambda i: (i, 0)),
        out_shape=jax.ShapeDtypeStruct((L, N_DEV * SSM_SLAB_PAD), BF16),
        scratch_shapes=[pltpu.VMEM((tm, SSM_COMPACT), F32)],
        compiler_params=_params(("parallel",)),
    )(dz, dxbc, ddt)


SSM_TC = 1024


def _ssm_conv_fwd(xpre, w, b, *, name):
    L, C = xpre.shape
    tm = _tile(L, 256)
    r = tm // HALO8
    tc = SSM_TC

    def body(x_ref, xh_ref, w_ref, b_ref, o_ref, ext_ref):
        keep = (pl.program_id(0) > 0).astype(F32)
        ext_ref[0:HALO8, :] = xh_ref[...] * keep
        ext_ref[HALO8:HALO8 + tm, :] = x_ref[...]
        pre = _conv_from_ext(ext_ref, w_ref, b_ref[...], SSM_CONV_K, HALO8, tm)
        o_ref[...] = pre * _sigmoid(pre)

    return pl.pallas_call(
        body, name=name, grid=(L // tm, C // tc),
        in_specs=[pl.BlockSpec((tm, tc), lambda i, j: (i, j)), pl.BlockSpec((HALO8, tc), lambda i, j: (_prev_blk(i, r), j)),
                  pl.BlockSpec((SSM_CONV_K, tc), lambda i, j: (0, j)), pl.BlockSpec((1, tc), lambda i, j: (0, j))],
        out_specs=pl.BlockSpec((tm, tc), lambda i, j: (i, j)),
        out_shape=jax.ShapeDtypeStruct((L, C), F32),
        scratch_shapes=[pltpu.VMEM((HALO8 + tm, tc), F32)],
        compiler_params=_params(("parallel", "parallel")),
    )(xpre, xpre, w, b)


def _ssm_conv_bwd(dy, xpre, w, b, *, name):
    L, C = xpre.shape
    tm = _tile(L, 256)
    r = tm // HALO8
    nt = L // tm
    nblk = L // HALO8
    tc = SSM_TC
    rows = tm + HALO8

    def body(xp_ref, xc_ref, xn_ref, dy_ref, dyn_ref, w_ref, b_ref, dx_ref, dw_ref, db_ref, ext_ref, dext_ref):
        i = pl.program_id(1)

        @pl.when(i == 0)
        def _():
            dw_ref[...] = jnp.zeros_like(dw_ref)
            db_ref[...] = jnp.zeros_like(db_ref)

        ext_ref[0:HALO8, :] = xp_ref[...] * (i > 0).astype(F32)
        ext_ref[HALO8:HALO8 + tm, :] = xc_ref[...]
        ext_ref[HALO8 + tm:HALO8 + rows, :] = xn_ref[...]
        pre = _conv_from_ext(ext_ref, w_ref, b_ref[...], SSM_CONV_K, HALO8, rows)
        dyx = jnp.concatenate([dy_ref[...], dyn_ref[...] * (i < nt - 1).astype(F32)], axis=0)
        dpre = dyx * _dsilu(pre, _sigmoid(pre))
        dext_ref[...] = dpre
        db_ref[...] += jnp.sum(dpre[0:tm], axis=0, keepdims=True)
        dx_ref[...] = _conv_back(dext_ref, xc_ref[...], w_ref, dw_ref, SSM_CONV_K, tm)

    cur = pl.BlockSpec((tm, tc), lambda j, i: (i, j))
    prv = pl.BlockSpec((HALO8, tc), lambda j, i: (_prev_blk(i, r), j))
    nxt = pl.BlockSpec((HALO8, tc), lambda j, i: (_next_blk(i, r, nblk), j))
    return pl.pallas_call(
        body, name=name, grid=(C // tc, nt),
        in_specs=[prv, cur, nxt, cur, nxt, pl.BlockSpec((SSM_CONV_K, tc), lambda j, i: (0, j)),
                  pl.BlockSpec((1, tc), lambda j, i: (0, j))],
        out_specs=[cur, pl.BlockSpec((SSM_CONV_K, tc), lambda j, i: (0, j)), pl.BlockSpec((1, tc), lambda j, i: (0, j))],
        out_shape=[jax.ShapeDtypeStruct((L, C), F32), jax.ShapeDtypeStruct((SSM_CONV_K, C), F32),
                   jax.ShapeDtypeStruct((1, C), F32)],
        scratch_shapes=[pltpu.VMEM((HALO8 + rows, tc), F32), pltpu.VMEM((rows, tc), F32)],
        compiler_params=_params(("parallel", "arbitrary")),
    )(xpre, xpre, xpre, dy, dy, w, b)


def _gnorm_fwd(y, z, g, *, name):
    L, C = y.shape
    tm = _tile(L, 256)
    gw = SSM_GROUP_W

    def body(y_ref, z_ref, g_ref, o_ref):
        for k in range(SSM_GROUPS):
            sl = slice(k * gw, (k + 1) * gw)
            zv = z_ref[:, sl]
            yz = y_ref[:, sl] * zv * _sigmoid(zv)
            r = lax.rsqrt(jnp.mean(yz * yz, axis=-1, keepdims=True) + RMS_EPS)
            o_ref[:, sl] = (yz * r * g_ref[:, sl]).astype(BF16)

    row = pl.BlockSpec((tm, C), lambda i: (i, 0))
    return pl.pallas_call(
        body, name=name, grid=(L // tm,),
        in_specs=[row, row, pl.BlockSpec((1, C), lambda i: (0, 0))], out_specs=row,
        out_shape=jax.ShapeDtypeStruct((L, C), BF16),
        compiler_params=_params(("parallel",)),
    )(y, z, g)


def _gnorm_bwd(dn, y, z, g, *, name):
    L, C = y.shape
    tm = _tile(L, 256)
    gw = SSM_GROUP_W

    def body(dn_ref, y_ref, z_ref, g_ref, dy_ref, dz_ref, dg_ref):
        @pl.when(pl.program_id(0) == 0)
        def _():
            dg_ref[...] = jnp.zeros_like(dg_ref)

        for k in range(SSM_GROUPS):
            sl = slice(k * gw, (k + 1) * gw)
            zv, yv = z_ref[:, sl], y_ref[:, sl]
            sz = _sigmoid(zv)
            silu = zv * sz
            yz = yv * silu
            r = lax.rsqrt(jnp.mean(yz * yz, axis=-1, keepdims=True) + RMS_EPS)
            nrm = yz * r
            dnv = dn_ref[:, sl]
            dg_ref[:, sl] += jnp.sum(dnv * nrm, axis=0, keepdims=True)
            dh = dnv * g_ref[:, sl]
            dyz = r * (dh - nrm * jnp.mean(dh * nrm, axis=-1, keepdims=True))
            dy_ref[:, sl] = dyz * silu
            dz_ref[:, sl] = dyz * yv * _dsilu(zv, sz)

    row = pl.BlockSpec((tm, C), lambda i: (i, 0))
    vec = pl.BlockSpec((1, C), lambda i: (0, 0))
    return pl.pallas_call(
        body, name=name, grid=(L // tm,),
        in_specs=[row, row, row, vec], out_specs=[row, row, vec],
        out_shape=[jax.ShapeDtypeStruct((L, C), F32), jax.ShapeDtypeStruct((L, C), F32),
                   jax.ShapeDtypeStruct((1, C), F32)],
        compiler_params=_params(("arbitrary",)),
    )(dn, y, z, g)


def _ssd_consts():
    q, gw = CHUNK, SSM_GROUP_W
    sub = lax.broadcasted_iota(jnp.int32, (q, gw), 0)
    lane_in = lax.broadcasted_iota(jnp.int32, (q, gw), 1) % q
    sub2 = lax.broadcasted_iota(jnp.int32, (gw, gw), 0) // q
    lane2 = lax.broadcasted_iota(jnp.int32, (gw, gw), 1) // q
    t = lax.broadcasted_iota(jnp.int32, (q, q), 0)
    u = lax.broadcasted_iota(jnp.int32, (q, q), 1)
    return dict(
        diag=(lane_in == sub), low=(lane_in <= sub), up=(sub <= lane_in), block=(sub2 == lane2),
        tri=(u <= t).astype(BF16), trit=(u >= t).astype(BF16), ones=jnp.ones((q, q), BF16),
        last=(lax.broadcasted_iota(jnp.int32, (q, HEAD_LANES), 0) == q - 1))


def _ssd_chunk_terms(dt_raw, bias, a_log, e, et, k):
    dt = _softplus(dt_raw + bias)
    a_neg = -jnp.exp(a_log)
    cs = _dot3_r(k["tri"], dt * a_neg)
    cs_last = cs[CHUNK - 1:CHUNK, :]
    ecs = jnp.exp(cs)
    dte = jnp.exp(cs_last - cs)
    ecl_hn = _dot3_l(ecs, k["last"].astype(BF16), "tn")
    rows = _dot3_r(et, ecl_hn)
    return dict(dt=dt, a_neg=a_neg, cs=cs, ecs=ecs, dte=dte, rows=rows,
                dtx=_dot3_l(dt, e), csx=_dot3_l(cs, e), ecsx=_dot3_l(ecs, e), dtex=_dot3_l(dte, e))


def _tile8(x):
    return jnp.concatenate([x] * 8, axis=0)


def _ssd_fwd(xbc, dt_raw, dt_bias, a_log, d_x, e, et, *, name):
    L = xbc.shape[0]
    nc = L // CHUNK
    q, gw, ns = CHUNK, SSM_GROUP_W, SSM_STATE

    def body(xbc_ref, dt_ref, bias_ref, alog_ref, dx_ref, e_ref, et_ref, y_ref, st_ref, s_ref):
        @pl.when(pl.program_id(0) == 0)
        def _():
            s_ref[...] = jnp.zeros_like(s_ref)

        k = _ssd_consts()
        t = _ssd_chunk_terms(dt_ref[...], bias_ref[...], alog_ref[...], e_ref[...], et_ref[...], k)
        st_ref[...] = s_ref[...]
        for g in range(SSM_GROUPS):
            ch = slice(g * gw, (g + 1) * gw)
            xs = xbc_ref[:, ch]
            bm = xbc_ref[:, SSM_INNER + g * ns:SSM_INNER + (g + 1) * ns].astype(BF16)
            cm = xbc_ref[:, SSM_INNER + (SSM_GROUPS + g) * ns:SSM_INNER + (SSM_GROUPS + g + 1) * ns].astype(BF16)
            xd = xs * t["dtx"][:, ch]
            csx = t["csx"][:, ch]
            csrow = _dot3_r(k["ones"], jnp.where(k["diag"], csx, 0.0))
            lcat = jnp.where(k["low"], jnp.exp(jnp.minimum(csx - csrow, 0.0)), 0.0)
            mcat = _dot(cm, _tile8(bm), "nt") * lcat
            xdbd = jnp.where(k["block"], _tile8(xd), 0.0).astype(BF16)
            sg = s_ref[ch, :]
            y = _dot(mcat.astype(BF16), xdbd)
            y = y + _dot(cm, sg.astype(BF16), "nt") * t["ecsx"][:, ch]
            y_ref[:, ch] = y + dx_ref[:, ch] * xs
            s_ref[ch, :] = sg * t["rows"][ch, :] + _dot((xd * t["dtex"][:, ch]).astype(BF16), bm, "tn")

    hv = pl.BlockSpec((1, HEAD_LANES), lambda c: (0, 0))
    return pl.pallas_call(
        body, name=name, grid=(nc,),
        in_specs=[pl.BlockSpec((q, SSM_CONV_DIM), lambda c: (c, 0)), pl.BlockSpec((q, HEAD_LANES), lambda c: (c, 0)),
                  hv, hv, pl.BlockSpec((1, SSM_INNER), lambda c: (0, 0)),
                  pl.BlockSpec((HEAD_LANES, SSM_INNER), lambda c: (0, 0)),
                  pl.BlockSpec((SSM_INNER, HEAD_LANES), lambda c: (0, 0))],
        out_specs=[pl.BlockSpec((q, SSM_INNER), lambda c: (c, 0)),
                   pl.BlockSpec((None, SSM_INNER, ns), lambda c: (c, 0, 0))],
        out_shape=[jax.ShapeDtypeStruct((L, SSM_INNER), F32), jax.ShapeDtypeStruct((nc, SSM_INNER, ns), F32)],
        scratch_shapes=[pltpu.VMEM((SSM_INNER, ns), F32)],
        compiler_params=_params(("arbitrary",)),
    )(xbc, dt_raw, dt_bias, a_log, d_x, e, et)


def _ssd_bwd(dy, xbc, dt_raw, states, dt_bias, a_log, d_x, e, et, *, name):
    L = xbc.shape[0]
    nc = L // CHUNK
    q, gw, ns = CHUNK, SSM_GROUP_W, SSM_STATE

    def body(dy_ref, xbc_ref, dt_ref, st_ref, bias_ref, alog_ref, dx_ref, e_ref, et_ref,
             dxbc_ref, ddt_ref, dbias_ref, dalog_ref, dd_ref, ds_ref, f1_ref, f2_ref, f3_ref, f4_ref, fs_ref):
        @pl.when(pl.program_id(0) == 0)
        def _():
            ds_ref[...] = jnp.zeros_like(ds_ref)
            for ref in (dbias_ref, dalog_ref, dd_ref):
                ref[...] = jnp.zeros_like(ref)

        k = _ssd_consts()
        ev = e_ref[...]
        t = _ssd_chunk_terms(dt_ref[...], bias_ref[...], alog_ref[...], ev, et_ref[...], k)
        ones8 = jnp.ones((8, ns), BF16)
        for g in range(SSM_GROUPS):
            ch = slice(g * gw, (g + 1) * gw)
            bsl = slice(SSM_INNER + g * ns, SSM_INNER + (g + 1) * ns)
            csl = slice(SSM_INNER + (SSM_GROUPS + g) * ns, SSM_INNER + (SSM_GROUPS + g + 1) * ns)
            xs = xbc_ref[:, ch]
            bm = xbc_ref[:, bsl].astype(BF16)
            cm = xbc_ref[:, csl].astype(BF16)
            dyv = dy_ref[:, ch]
            dtx, ecsx, dtex, csx = t["dtx"][:, ch], t["ecsx"][:, ch], t["dtex"][:, ch], t["csx"][:, ch]
            xd = xs * dtx
            csrow = _dot3_r(k["ones"], jnp.where(k["diag"], csx, 0.0))
            lcat = jnp.where(k["low"], jnp.exp(jnp.minimum(csx - csrow, 0.0)), 0.0)
            ltcat = jnp.where(k["up"], jnp.exp(jnp.minimum(csrow - csx, 0.0)), 0.0)
            mcat = _dot(cm, _tile8(bm), "nt") * lcat
            mtcat = _dot(bm, _tile8(cm), "nt") * ltcat
            xdbd = jnp.where(k["block"], _tile8(xd), 0.0).astype(BF16)
            dybd = jnp.where(k["block"], _tile8(dyv), 0.0).astype(BF16)
            gq = _dot(dyv.astype(BF16), xdbd, "nt") * mcat
            gt = _dot(xd.astype(BF16), dybd, "nt")
            gqt = gt * mtcat
            dcbt = _dot((gt * ltcat).astype(BF16), jnp.where(k["diag"], 1.0, 0.0).astype(BF16), "nt")
            dcbt = dcbt.astype(BF16)
            sg = st_ref[ch, :]
            dsg = ds_ref[ch, :]
            sgb, dsgb = sg.astype(BF16), dsg.astype(BF16)
            yoff = _dot(cm, sgb, "nt") * ecsx
            dye = (dyv * ecsx).astype(BF16)
            xdd = xd * dtex
            dxbc_ref[:, csl] = _dot(dcbt, bm, "tn") + _dot(dye, sgb)
            dxbc_ref[:, bsl] = _dot(dcbt, cm) + _dot(xdd.astype(BF16), dsgb)
            bds = _dot(bm, dsgb, "nt")
            dxd = _dot(mtcat.astype(BF16), dybd) + dtex * bds
            wx = xdd * bds
            ds_ref[ch, :] = t["rows"][ch, :] * dsg + _dot(dye, cm, "tn")
            f1_ref[:, ch] = gq - gqt + dyv * yoff - wx
            f2_ref[:, ch] = wx
            f3_ref[:, ch] = dxd * xs
            f4_ref[:, ch] = dyv * xs
            dxbc_ref[:, ch] = dxd * dtx + dx_ref[:, ch] * dyv
            fs_ref[:, ch] = _dot3_r(ones8, dsg * sg, "nt")
        fold = lambda v: _dot3_l(v, ev, "nt")
        f2 = fold(f2_ref[...])
        last_row = jnp.sum(f2, axis=0, keepdims=True) + t["ecs"][q - 1:q, :] * fold(fs_ref[...])[0:1, :]
        dcs = fold(f1_ref[...]) + jnp.where(k["last"], last_row, 0.0)
        da = _dot3_r(k["trit"], dcs)
        ddt = da * t["a_neg"] + fold(f3_ref[...])
        ddt_raw = ddt * _sigmoid(dt_ref[...] + bias_ref[...])
        ddt_ref[...] = ddt_raw
        dbias_ref[...] += jnp.sum(ddt_raw, axis=0, keepdims=True)
        dalog_ref[...] += jnp.sum(da * t["dt"], axis=0, keepdims=True) * t["a_neg"]
        dd_ref[...] += jnp.sum(fold(f4_ref[...]), axis=0, keepdims=True)

    rev = lambda c: (nc - 1 - c, 0)
    hv = pl.BlockSpec((1, HEAD_LANES), lambda c: (0, 0))
    return pl.pallas_call(
        body, name=name, grid=(nc,),
        in_specs=[pl.BlockSpec((q, SSM_INNER), rev), pl.BlockSpec((q, SSM_CONV_DIM), rev),
                  pl.BlockSpec((q, HEAD_LANES), rev),
                  pl.BlockSpec((None, SSM_INNER, ns), lambda c: (nc - 1 - c, 0, 0)),
                  hv, hv, pl.BlockSpec((1, SSM_INNER), lambda c: (0, 0)),
                  pl.BlockSpec((HEAD_LANES, SSM_INNER), lambda c: (0, 0)),
                  pl.BlockSpec((SSM_INNER, HEAD_LANES), lambda c: (0, 0))],
        out_specs=[pl.BlockSpec((q, SSM_CONV_DIM), rev), pl.BlockSpec((q, HEAD_LANES), rev), hv, hv, hv],
        out_shape=[jax.ShapeDtypeStruct((L, SSM_CONV_DIM), F32), jax.ShapeDtypeStruct((L, HEAD_LANES), F32)]
        + [jax.ShapeDtypeStruct((1, HEAD_LANES), F32)] * 3,
        scratch_shapes=[pltpu.VMEM((SSM_INNER, ns), F32)] + [pltpu.VMEM((q, SSM_INNER), F32)] * 4
        + [pltpu.VMEM((8, SSM_INNER), F32)],
        compiler_params=_params(("arbitrary",)),
    )(dy, xbc, dt_raw, states, dt_bias, a_log, d_x, e, et)


def _my_index():
    return 4 * lax.axis_index("x") + 2 * lax.axis_index("y") + lax.axis_index("c")


def _peer(k):
    return (lax.axis_index("x") ^ ((k >> 2) & 1), lax.axis_index("y") ^ ((k >> 1) & 1), lax.axis_index("c") ^ (k & 1))


def _all_gather(shard, *, by_rows, name):
    nl, a, b = shard.shape
    out_shape = (nl, N_DEV * a, b) if by_rows else (nl, N_DEV, a, b)

    def body(src_ref, out_ref, send_sems, recv_sems, local_sems):
        me = _my_index()

        def mine(j):
            if by_rows:
                return out_ref.at[j, pl.ds(pl.multiple_of(me * a, 16), a), :]
            return out_ref.at[j, me]

        local = [pltpu.make_async_copy(src_ref.at[j], mine(j), local_sems.at[j]) for j in range(nl)]
        for cp in local:
            cp.start()
        copies = []
        for j in range(nl):
            for k in range(1, N_DEV):
                cp = pltpu.make_async_remote_copy(
                    src_ref=src_ref.at[j], dst_ref=mine(j), send_sem=send_sems.at[j, k - 1],
                    recv_sem=recv_sems.at[j, k - 1], device_id=_peer(k), device_id_type=MESH_ID)
                cp.start()
                copies.append(cp)
        for cp in copies:
            cp.wait()
        for cp in local:
            cp.wait()

    return pl.pallas_call(
        body, name=name, in_specs=[_HBM], out_specs=_HBM,
        out_shape=jax.ShapeDtypeStruct(out_shape, shard.dtype),
        scratch_shapes=[pltpu.SemaphoreType.DMA((nl, N_DEV - 1)), pltpu.SemaphoreType.DMA((nl, N_DEV - 1)),
                        pltpu.SemaphoreType.DMA((nl,))],
    )(shard)


_EFFECT = pltpu.SideEffectType.DATAFLOW_SIDE_EFFECTING


def _blk(ref, i, rows):
    if len(ref.shape) == 3:
        return ref.at[i]
    return ref.at[pl.ds(pl.multiple_of(i * rows, 16), rows), :]


def _remote_copies(scatter, src_ref, land_ref, send_sems, recv_sems, idx):
    me = _my_index()
    out = []
    for k in range(1, N_DEV):
        if scatter:
            src, dst = _blk(src_ref, me ^ k, land_ref.shape[1]), land_ref.at[me]
        else:
            src, dst = src_ref, _blk(land_ref, me, src_ref.shape[0])
        sem = idx * (N_DEV - 1) + k - 1
        out.append(pltpu.make_async_remote_copy(
            src_ref=src, dst_ref=dst, send_sem=send_sems.at[sem], recv_sem=recv_sems.at[sem],
            device_id=_peer(k), device_id_type=MESH_ID))
    return out


def _place_own(src, land_shape, *, scatter, name):
    def body(src_ref, thru_ref, land_ref, sem):
        me = _my_index()
        if scatter:
            s, d = _blk(src_ref, me, land_shape[1]), land_ref.at[me]
        else:
            s, d = src_ref, _blk(land_ref, me, src_ref.shape[0])
        cp = pltpu.make_async_copy(s, d, sem)
        cp.start()
        cp.wait()

    return pl.pallas_call(
        body, name=name, in_specs=[_HBM], out_specs=[_HBM, _HBM],
        out_shape=[jax.ShapeDtypeStruct(src.shape, src.dtype), jax.ShapeDtypeStruct(land_shape, src.dtype)],
        scratch_shapes=[pltpu.SemaphoreType.DMA(())], input_output_aliases={0: 0},
    )(src)


def _copies_start(srcs, lands, *, scatter, after, name):
    n = len(srcs)
    n_after = len(after)

    def body(*refs):
        src_refs, land_refs = refs[:n], refs[n:2 * n]
        send_sems, recv_sems = refs[2 * n + n_after], refs[2 * n + n_after + 1]
        for i in range(n):
            for cp in _remote_copies(scatter, src_refs[i], land_refs[i], send_sems, recv_sems, i):
                cp.start()
        refs[-1][...] = jnp.zeros_like(refs[-1])

    arrays = [pltpu.with_memory_space_constraint(t, pltpu.HBM) for t in list(srcs) + list(lands)]
    sems = pltpu.SemaphoreType.DMA((n * (N_DEV - 1),))
    res = pl.pallas_call(
        body, name=name,
        out_shape=(sems, sems, *[pltpu.HBM(t.shape, t.dtype) for t in arrays], jax.ShapeDtypeStruct((8, 128), F32)),
        in_specs=[_HBM] * (2 * n) + [_ANY] * n_after,
        out_specs=(_SEM, _SEM, *[_HBM] * (2 * n), pl.BlockSpec(memory_space=pltpu.VMEM)),
        input_output_aliases={i: 2 + i for i in range(2 * n)},
        compiler_params=pltpu.CompilerParams(has_side_effects=_EFFECT),
    )(*arrays, *after)
    return res[0], res[1], res[2:2 + n], res[2 + n:2 + 2 * n], res[-1]


def _copies_wait(send_sems, recv_sems, srcs, lands, *, scatter, after, name):
    n = len(srcs)

    def body(*refs):
        src_refs, land_refs = refs[:n], refs[n:2 * n]
        s_sems, r_sems = refs[2 * n], refs[2 * n + 1]
        for i in range(n):
            for cp in _remote_copies(scatter, src_refs[i], land_refs[i], s_sems, r_sems, i):
                cp.wait_send()
                cp.wait_recv()

    arrays = list(srcs) + list(lands)
    res = pl.pallas_call(
        body, name=name,
        out_shape=tuple(pltpu.HBM(t.shape, t.dtype) for t in arrays),
        in_specs=[_HBM] * (2 * n) + [_SEM, _SEM] + [_ANY] * len(after), out_specs=tuple([_HBM] * (2 * n)),
        input_output_aliases={i: i for i in range(2 * n)},
        compiler_params=pltpu.CompilerParams(has_side_effects=_EFFECT),
    )(*arrays, send_sems, recv_sems, *after)
    return res[n:]


def _adamw_math(w, g, m, v):
    m = ADAM_B1 * m + (1.0 - ADAM_B1) * g
    v = ADAM_B2 * v + (1.0 - ADAM_B2) * (g * g)
    m_hat = m / (1.0 - ADAM_B1 ** ADAM_STEP)
    v_hat = v / (1.0 - ADAM_B2 ** ADAM_STEP)
    delta = -ADAM_LR * (m_hat / (jnp.sqrt(v_hat) + ADAM_EPS) + ADAM_WD * w)
    return delta, m, v


def _adamw(recvs, w, m, v, *, name):
    nl, R, C = w.shape
    tr = max(t for t in range(8, (128 if C > 1024 else 512) + 1, 8) if R % t == 0)

    def body(*refs):
        r_refs = refs[:nl]
        w_ref, m_ref, v_ref, g_ref, d_ref, nm_ref, nv_ref = refs[nl:]
        for layer in range(nl):
            @pl.when(pl.program_id(0) == layer)
            def _(r_ref=r_refs[layer]):
                g = r_ref[0].astype(F32)
                for s in range(1, N_DEV):
                    g = g + r_ref[s].astype(F32)
                delta, nm, nv = _adamw_math(w_ref[...], g, m_ref[...], v_ref[...])
                g_ref[...] = g
                d_ref[...] = delta
                nm_ref[...] = nm
                nv_ref[...] = nv

    def recv_spec(layer):
        return pl.BlockSpec((N_DEV, tr, C), lambda j, i: (0, jnp.where(j == layer, i, 0), 0))

    row = pl.BlockSpec((None, tr, C), lambda j, i: (j, i, 0))
    return pl.pallas_call(
        body, name=name, grid=(nl, R // tr),
        in_specs=[recv_spec(layer) for layer in range(nl)] + [row, row, row],
        out_specs=[row] * 4, out_shape=[jax.ShapeDtypeStruct((nl, R, C), F32)] * 4,
        compiler_params=_params(("parallel", "parallel")),
    )(*recvs, w, m, v)


def _sum_slots(g8, *, name):
    _, P, C = g8.shape

    def body(r_ref, o_ref):
        g = r_ref[0]
        for s in range(1, N_DEV):
            g = g + r_ref[s]
        o_ref[...] = g

    return pl.pallas_call(
        body, name=name, grid=(1,),
        in_specs=[pl.BlockSpec((N_DEV, P, C), lambda i: (0, 0, 0))],
        out_specs=pl.BlockSpec((P, C), lambda i: (0, 0)),
        out_shape=jax.ShapeDtypeStruct((P, C), F32),
        compiler_params=_params(("arbitrary",)),
    )(g8)


def _adamw_small(g, w, m, v, *, name):
    P, C = w.shape

    def body(g_ref, w_ref, m_ref, v_ref, d_ref, nm_ref, nv_ref):
        delta, nm, nv = _adamw_math(w_ref[...], g_ref[...], m_ref[...], v_ref[...])
        d_ref[...] = delta
        nm_ref[...] = nm
        nv_ref[...] = nv

    full = pl.BlockSpec((P, C), lambda i: (0, 0))
    return pl.pallas_call(
        body, name=name, grid=(1,), in_specs=[full] * 4, out_specs=[full] * 3,
        out_shape=[jax.ShapeDtypeStruct((P, C), F32)] * 3,
        compiler_params=_params(("arbitrary",)),
    )(g, w, m, v)


def _pack(arrays):
    flat = jnp.concatenate([a.reshape(-1) for a in arrays])
    pad = (-flat.shape[0]) % (8 * 128)
    return jnp.pad(flat, (0, pad)).reshape(-1, 128)


def _unpack(buf, shapes):
    flat = buf.reshape(-1)
    out, off = [], 0
    for shp in shapes:
        n = 1
        for s in shp:
            n *= s
        out.append(flat[off:off + n].reshape(shp))
        off += n
    return out


def _expand_matrices():
    h = lax.broadcasted_iota(jnp.int32, (HEAD_LANES, SSM_INNER), 0)
    col = lax.broadcasted_iota(jnp.int32, (HEAD_LANES, SSM_INNER), 1) // SSM_HEAD_DIM
    e = (h == col).astype(BF16)
    return e, e.T


def _pad_heads(v):
    return jnp.pad(v.reshape(1, -1), ((0, 0), (0, HEAD_LANES - v.shape[-1])))


SHARDED_SMALL = ("cv_w_dw", "ssm_w_conv", "ssm_b_conv", "ssm_norm_g", "ffn_w_dw")
REPLICATED_SMALL = ("norm_mix_g", "norm_ffn_g", "norm_final_g", "cv_b_in", "cv_b_dw", "cv_ln_g", "cv_ln_b",
                    "cv_b_out", "ssm_dt_bias", "ssm_a_log", "ssm_d", "ffn_b_dw")
SMALL = REPLICATED_SMALL + SHARDED_SMALL
BIG_COLS = ("cv_w_in", "ssm_w_in", "ffn_w_up")
BIG_ROWS = ("cv_w_out", "ssm_w_out", "ffn_w_down")
WEIGHTS = ("norm_mix_g", "norm_ffn_g", "norm_final_g", "cv_w_in", "cv_b_in", "cv_w_dw", "cv_b_dw", "cv_ln_g",
           "cv_ln_b", "cv_w_out", "cv_b_out", "ssm_w_in", "ssm_w_conv", "ssm_b_conv", "ssm_dt_bias", "ssm_a_log",
           "ssm_d", "ssm_norm_g", "ssm_w_out", "ffn_w_up", "ffn_w_dw", "ffn_b_dw", "ffn_w_down")


N_STAGES = 2 * DEPTH


def _stage_weights(si):
    i = si // 2
    if si % 2:
        return "ffn_w_up", "ffn_w_down", i
    return ("cv_w_in", "cv_w_out", i // 2) if i % 2 == 0 else ("ssm_w_in", "ssm_w_out", i // 2)


def _gather_small_weights(w):
    small_local = _pack([w[n] for n in SHARDED_SMALL])
    small8 = _all_gather(small_local[None], by_rows=False, name="gather_small")[0]
    full = {}
    per_dev = [_unpack(small8[d], [w[n].shape for n in SHARDED_SMALL]) for d in range(N_DEV)]
    for idx, n in enumerate(SHARDED_SMALL):
        full[n] = jnp.concatenate([per_dev[d][idx] for d in range(N_DEV)], axis=-1)
    for n in REPLICATED_SMALL:
        full[n] = w[n]
    return full


def _forward_backward(x, target, weights_of, full, emit):
    e, et = _expand_matrices()
    grads = {}

    saved = []
    for i in range(DEPTH):
        j = i // 2
        s = {"x_in": x}
        wc, wr, behind = weights_of(2 * i, x)
        h1 = _rms_fwd(x, full["norm_mix_g"][i][None], name=f"rms_mix_{i}")
        s["h1"] = h1
        if i % 2 == 0:
            u = _mm_cols_nn(h1, wc, bias=full["cv_b_in"][j][None], after=behind, name=f"cv_in_{i}")
            sact, c = _cv_mid_fwd(u, full["cv_w_dw"][j], full["cv_b_dw"][j][None], full["cv_ln_g"][j][None],
                                  full["cv_ln_b"][j][None], name=f"cv_mid_{i}")
            x = _mm_rows_nn(sact, wr, res=x, bias=full["cv_b_out"][j][None], name=f"cv_out_{i}")
            s.update(u=u, c=c, sact=sact)
        else:
            zx = _mm_cols_nn(h1, wc, after=behind, name=f"ssm_in_{i}")
            z, xpre, dt_raw = _ssm_unpad(zx, name=f"ssm_unpad_{i}")
            xbc = _ssm_conv_fwd(xpre, full["ssm_w_conv"][j], full["ssm_b_conv"][j][None], name=f"ssm_conv_{i}")
            hp = dict(dt_bias=_pad_heads(full["ssm_dt_bias"][j]), a_log=_pad_heads(full["ssm_a_log"][j]),
                      d_x=jnp.repeat(full["ssm_d"][j], SSM_HEAD_DIM)[None])
            y, states = _ssd_fwd(xbc, dt_raw, hp["dt_bias"], hp["a_log"], hp["d_x"], e, et, name=f"ssd_fwd_{i}")
            yn = _gnorm_fwd(y, z, full["ssm_norm_g"][j][None], name=f"gnorm_{i}")
            x = _mm_rows_nn(yn, wr, res=x, name=f"ssm_out_{i}")
            s.update(z=z, xpre=xpre, dt_raw=dt_raw, xbc=xbc, y=y, states=states, yn=yn, hp=hp)
        s["x_mid"] = x
        wc2, wr2, behind = weights_of(2 * i + 1, x)
        h2 = _rms_fwd(x, full["norm_ffn_g"][i][None], name=f"rms_ffn_{i}")
        u0 = _mm_cols_nn(h2, wc2, after=behind, name=f"ffn_up_{i}")
        gact = _ffn_mid_fwd(u0, full["ffn_w_dw"][i], full["ffn_b_dw"][i][None], name=f"ffn_mid_{i}")
        x = _mm_rows_nn(gact, wr2, res=x, name=f"ffn_down_{i}")
        s.update(h2=h2, u0=u0, gact=gact, weights=(wc, wr, wc2, wr2))
        saved.append(s)

    dx, dg_final, loss = _final_loss(x, full["norm_final_g"][None], target, name="final_loss")
    grads["norm_final_g"] = dg_final[0]

    small_layers = {n: [None] * full[n].shape[0] for n in SMALL if n != "norm_final_g"}
    behind = None
    for i in reversed(range(DEPTH)):
        j = i // 2
        s = saved[i]
        wc, wr, wc2, wr2 = s["weights"]
        dgact = _mm_rows_nt(dx, wr2, after=behind, name=f"ffn_down_dx_{i}")
        dwr2 = _mm_rows_tn(s["gact"], dx, name=f"ffn_down_dw_{i}")
        du0, dw_dw, db_dw = _ffn_bwd(dgact, s["u0"], full["ffn_w_dw"][i], full["ffn_b_dw"][i][None], name=f"ffn_bwd_{i}")
        small_layers["ffn_w_dw"][i] = dw_dw
        small_layers["ffn_b_dw"][i] = db_dw[0]
        dh2 = _mm_cols_nt(du0, wc2, name=f"ffn_up_dx_{i}")
        dwc2 = _mm_cols_tn(s["h2"], du0, name=f"ffn_up_dw_{i}")
        dx, dg, colsum = _rms_bwd(s["x_mid"], full["norm_ffn_g"][i][None], dh2, dx, name=f"rms_ffn_bwd_{i}")
        small_layers["norm_ffn_g"][i] = dg[0]
        behind = emit(2 * i + 1, dwc2, dwr2)
        if i % 2 == 0:
            small_layers["cv_b_out"][j] = colsum[0]
            dsact = _mm_rows_nt(dx, wr, after=behind, name=f"cv_out_dx_{i}")
            dwr = _mm_rows_tn(s["sact"], dx, name=f"cv_out_dw_{i}")
            du, dw_dw, db_dw, dlg, dlb, db_in = _cv_bwd(dsact, s["c"], s["u"], full["cv_w_dw"][j], full["cv_ln_g"][j][None],
                                                        full["cv_ln_b"][j][None], name=f"cv_bwd_{i}")
            small_layers["cv_w_dw"][j] = dw_dw
            small_layers["cv_b_dw"][j] = db_dw[0]
            small_layers["cv_ln_g"][j] = dlg[0]
            small_layers["cv_ln_b"][j] = dlb[0]
            small_layers["cv_b_in"][j] = db_in[0]
            dh1 = _mm_cols_nt(du, wc, name=f"cv_in_dx_{i}")
            dwc = _mm_cols_tn(s["h1"], du, name=f"cv_in_dw_{i}")
        else:
            hp = s["hp"]
            dyn = _mm_rows_nt(dx, wr, after=behind, name=f"ssm_out_dx_{i}")
            dwr = _mm_rows_tn(s["yn"], dx, name=f"ssm_out_dw_{i}")
            dy, dz, dng = _gnorm_bwd(dyn, s["y"], s["z"], full["ssm_norm_g"][j][None], name=f"gnorm_bwd_{i}")
            small_layers["ssm_norm_g"][j] = dng[0]
            dxbc, ddt_raw, dbias, dalog, dd = _ssd_bwd(dy, s["xbc"], s["dt_raw"], s["states"], hp["dt_bias"], hp["a_log"],
                                                      hp["d_x"], e, et, name=f"ssd_bwd_{i}")
            small_layers["ssm_dt_bias"][j] = dbias[0, :SSM_HEADS]
            small_layers["ssm_a_log"][j] = dalog[0, :SSM_HEADS]
            small_layers["ssm_d"][j] = dd[0, :SSM_HEADS]
            dxpre, dw_conv, db_conv = _ssm_conv_bwd(dxbc, s["xpre"], full["ssm_w_conv"][j], full["ssm_b_conv"][j][None],
                                                    name=f"ssm_conv_bwd_{i}")
            small_layers["ssm_w_conv"][j] = dw_conv
            small_layers["ssm_b_conv"][j] = db_conv[0]
            dzx = _ssm_pad(dz, dxpre, ddt_raw, name=f"ssm_pad_{i}")
            dh1 = _mm_cols_nt(dzx, wc, name=f"ssm_in_dx_{i}")
            dwc = _mm_cols_tn(s["h1"], dzx, name=f"ssm_in_dw_{i}")
        dx, dg, _ = _rms_bwd(s["x_in"], full["norm_mix_g"][i][None], dh1, dx, name=f"rms_mix_bwd_{i}")
        small_layers["norm_mix_g"][i] = dg[0]
        behind = emit(2 * i, dwc, dwr)
    for n, layers in small_layers.items():
        grads[n] = jnp.stack(layers)
    return loss, dx, grads


def _update(recvs, grads, w, m, v):
    me = _my_index()
    out = {}
    for n in BIG_COLS + BIG_ROWS:
        b = recvs[n][0].shape[-1]
        pad = ((0, 0), (0, 0), (0, b - w[n].shape[-1]))
        res = _adamw(recvs[n], *[jnp.pad(t, pad) for t in (w[n], m[n], v[n])], name="adamw_" + n)
        out[n] = [r[..., :w[n].shape[-1]] for r in res]

    small_part = _pack([grads[n] for n in SMALL])
    small_all = _all_gather(small_part[None], by_rows=False, name="gather_small_grads")[0]
    small_sum = _sum_slots(small_all, name="sum_small_grads")
    gfull = dict(zip(SMALL, _unpack(small_sum, [grads[n].shape for n in SMALL])))
    glocal = []
    for n in SMALL:
        g = gfull[n]
        if n in SHARDED_SMALL:
            cols = w[n].shape[-1]
            g = lax.dynamic_slice_in_dim(g, me * cols, cols, axis=g.ndim - 1)
        glocal.append(g)
    packed = [_pack(glocal)] + [_pack([t[n] for n in SMALL]) for t in (w, m, v)]
    res = _adamw_small(*packed, name="adamw_small")
    shapes = [w[n].shape for n in SMALL]
    unpacked = [_unpack(r, shapes) for r in res]
    for idx, n in enumerate(SMALL):
        out[n] = [glocal[idx]] + [u[idx] for u in unpacked]
    return out


GATHER_AHEAD = 3


def _train_step(x, target, w, m, v):
    full = _gather_small_weights(w)
    gathers, exchanges = {}, {}
    last_start = [jnp.zeros((8, 128), F32)]

    def start_gather(si, after):
        cn, rn, j = _stage_weights(si)
        wc, wr = w[cn][j].astype(BF16), w[rn][j].astype(BF16)
        if cn == "ssm_w_in":
            wc = jnp.pad(wc, ((0, 0), (0, SSM_SLAB_PAD - SSM_SLAB)))
        wc, lc = _place_own(wc, (N_DEV,) + wc.shape, scatter=False, name=f"gather_place_cols_{si}")
        wr, lr = _place_own(wr, (N_DEV * wr.shape[0], wr.shape[1]), scatter=False, name=f"gather_place_rows_{si}")
        *gathers[si], last_start[0] = _copies_start([wc, wr], [lc, lr], scatter=False, after=after,
                                                    name=f"gather_start_{si}")

    for si in range(GATHER_AHEAD):
        start_gather(si, [last_start[0]])

    def weights_of(si, x_act):
        lc, lr = _copies_wait(*gathers.pop(si), scatter=False, after=[x_act, last_start[0]], name=f"gather_wait_{si}")
        behind = None
        if si + GATHER_AHEAD < N_STAGES:
            start_gather(si + GATHER_AHEAD, [last_start[0], lc])
            behind = last_start[0]
        return lc, lr, behind

    def emit(si, dwc, dwr):
        dwc, lc = _place_own(dwc, dwc.shape, scatter=True, name=f"exchange_place_cols_{si}")
        dwr, lr = _place_own(dwr, (N_DEV, dwr.shape[0] // N_DEV, dwr.shape[1]), scatter=True,
                             name=f"exchange_place_rows_{si}")
        *exchanges[si], token = _copies_start([dwc, dwr], [lc, lr], scatter=True, after=[lc],
                                              name=f"exchange_start_{si}")
        return token

    loss, dx, grads = _forward_backward(x, target, weights_of, full, emit)

    recvs = {n: [None] * w[n].shape[0] for n in BIG_COLS + BIG_ROWS}
    behind = dx
    for si in reversed(range(N_STAGES)):
        cn, rn, j = _stage_weights(si)
        recvs[cn][j], recvs[rn][j] = _copies_wait(*exchanges.pop(si), scatter=True, after=[behind],
                                                  name=f"exchange_wait_{si}")
        behind = recvs[cn][j]
    out = _update(recvs, grads, w, m, v)
    return lax.psum(loss[0, 0], AXES), dx, out


def kernel(x, norm_mix_g, norm_ffn_g, norm_final_g, cv_w_in, cv_b_in, cv_w_dw, cv_b_dw, cv_ln_g, cv_ln_b, cv_w_out, cv_b_out, ssm_w_in, ssm_w_conv, ssm_b_conv, ssm_dt_bias, ssm_a_log, ssm_d, ssm_norm_g, ssm_w_out, ffn_w_up, ffn_w_dw, ffn_b_dw, ffn_w_down, loss_target, m_norm_mix_g, m_norm_ffn_g, m_norm_final_g, m_cv_w_in, m_cv_b_in, m_cv_w_dw, m_cv_b_dw, m_cv_ln_g, m_cv_ln_b, m_cv_w_out, m_cv_b_out, m_ssm_w_in, m_ssm_w_conv, m_ssm_b_conv, m_ssm_dt_bias, m_ssm_a_log, m_ssm_d, m_ssm_norm_g, m_ssm_w_out, m_ffn_w_up, m_ffn_w_dw, m_ffn_b_dw, m_ffn_w_down, v_norm_mix_g, v_norm_ffn_g, v_norm_final_g, v_cv_w_in, v_cv_b_in, v_cv_w_dw, v_cv_b_dw, v_cv_ln_g, v_cv_ln_b, v_cv_w_out, v_cv_b_out, v_ssm_w_in, v_ssm_w_conv, v_ssm_b_conv, v_ssm_dt_bias, v_ssm_a_log, v_ssm_d, v_ssm_norm_g, v_ssm_w_out, v_ffn_w_up, v_ffn_w_dw, v_ffn_b_dw, v_ffn_w_down):
    args = locals()
    w = {n: args[n] for n in WEIGHTS}
    m = {n: args["m_" + n] for n in WEIGHTS}
    v = {n: args["v_" + n] for n in WEIGHTS}
    loss, dx, out = _train_step(x[0], loss_target[0], w, m, v)
    return (loss, dx[None], *[out[n][0] for n in WEIGHTS], *[out[n][1] for n in WEIGHTS],
            *[out[n][2] for n in WEIGHTS], *[out[n][3] for n in WEIGHTS])
```

```python
import functools

import jax
import jax.numpy as jnp
from jax import lax
from jax.experimental import pallas as pl
from jax.experimental.pallas import tpu as pltpu

F32, BF16 = jnp.float32, jnp.bfloat16
AXES = ("x", "y", "c")
N_DEV = 8
MESH_ID = pl.DeviceIdType.MESH

D_MODEL = 2048
DEPTH = 4
CHUNK = 64
CONV_K = 31
SSM_INNER = 4096
SSM_HEADS = 64
SSM_HEAD_DIM = 64
SSM_GROUPS = 8
SSM_GROUP_W = SSM_INNER // SSM_GROUPS
SSM_STATE = 128
SSM_CONV_K = 4
SSM_CONV_DIM = SSM_INNER + 2 * SSM_GROUPS * SSM_STATE
SSM_IN_DIM = SSM_INNER + SSM_CONV_DIM + SSM_HEADS
SSM_SLAB = SSM_IN_DIM // N_DEV
SSM_SLAB_PAD = 1408
SSM_COMPACT = 1280 * (N_DEV - 1) + SSM_SLAB_PAD
FFN_HIDDEN = 5632
FFN_K = 3
HEAD_LANES = 128
RMS_EPS = 1e-6
LN_EPS = 1e-5
ADAM_LR, ADAM_B1, ADAM_B2, ADAM_EPS, ADAM_WD, ADAM_STEP = 0.001, 0.9, 0.999, 1e-08, 0.01, 10

_DIMS = {
    "nn": (((1,), (0,)), ((), ())),
    "nt": (((1,), (1,)), ((), ())),
    "tn": (((0,), (0,)), ((), ())),
}


def _params(sem=None, vmem_mb=48):
    return pltpu.CompilerParams(dimension_semantics=sem, vmem_limit_bytes=vmem_mb << 20)


_HBM = pl.BlockSpec(memory_space=pltpu.HBM)
_ANY = pl.BlockSpec(memory_space=pl.ANY)
_SEM = pl.BlockSpec(memory_space=pltpu.SEMAPHORE)


def _dot(a, b, mode="nn"):
    return lax.dot_general(a, b, _DIMS[mode], preferred_element_type=F32)


def _split3(x):
    hi = x.astype(BF16)
    r1 = x - hi.astype(F32)
    mid = r1.astype(BF16)
    lo = (r1 - mid.astype(F32)).astype(BF16)
    return hi, mid, lo


def _dot3_l(x, m, mode="nn"):
    hi, mid, lo = _split3(x)
    return _dot(hi, m, mode) + _dot(mid, m, mode) + _dot(lo, m, mode)


def _dot3_r(m, x, mode="nn"):
    hi, mid, lo = _split3(x)
    return _dot(m, hi, mode) + _dot(m, mid, mode) + _dot(m, lo, mode)


def _sigmoid(x):
    return jax.nn.sigmoid(x)


def _dsilu(x, sg):
    return sg * (1.0 + x * (1.0 - sg))


def _softplus(x):
    return jnp.maximum(x, 0.0) + jnp.log(1.0 + jnp.exp(-jnp.abs(x)))


def _tile(n, pref):
    return min(n, pref)


def _matmul(a, b, *, mode, grid, a_spec, b_spec, o_spec, o_block, out_shape, nk, name,
            extras=(), epilogue=None, after=None, vmem_mb=48):
    n_extra = len(extras)
    if after is not None:
        extras = tuple(extras) + ((after, _ANY),)
    n_in = len(extras)

    def body(a_ref, b_ref, *rest):
        extra_refs = rest[:n_extra]
        o_ref = rest[n_in]
        part = _dot(a_ref[...].astype(BF16), b_ref[...].astype(BF16), mode)

        def finish(acc):
            if epilogue is not None:
                acc = epilogue(acc, *[r[...] for r in extra_refs])
            o_ref[...] = acc.astype(o_ref.dtype)

        if nk == 1:
            finish(part)
        else:
            acc_ref = rest[n_in + 1]
            k = pl.program_id(len(grid) - 1)

            @pl.when(k == 0)
            def _():
                acc_ref[...] = part

            @pl.when(k > 0)
            def _():
                acc_ref[...] += part

            @pl.when(k == nk - 1)
            def _():
                finish(acc_ref[...])

    scratch = [] if nk == 1 else [pltpu.VMEM(o_block, F32)]
    sem = ("parallel",) * (len(grid) - 1) + ("arbitrary",)
    return pl.pallas_call(
        body, name=name, grid=grid,
        in_specs=[a_spec, b_spec] + [s for _, s in extras],
        out_specs=o_spec, out_shape=out_shape, scratch_shapes=scratch,
        compiler_params=_params(sem, vmem_mb),
    )(a, b, *[x for x, _ in extras])


HALF_SLABS = N_DEV // 2


def _mm_cols_nn(h, wg, *, name, bias=None, after=None, halves=False):
    L, K = h.shape
    n = wg.shape[-1]
    tm = _tile(L, 512)
    extras, epi = (), None
    if bias is not None:
        extras = ((bias, pl.BlockSpec((1, n), lambda s, i: (0, s))),)
        epi = lambda acc, b: acc + b
    if halves:
        o_spec = pl.BlockSpec((None, tm, n), lambda s, i: (s // HALF_SLABS, i, s % HALF_SLABS))
        out_shape = jax.ShapeDtypeStruct((2, L, HALF_SLABS * n), F32)
    else:
        o_spec = pl.BlockSpec((tm, n), lambda s, i: (i, s))
        out_shape = jax.ShapeDtypeStruct((L, N_DEV * n), F32)
    return _matmul(
        h, wg, mode="nn", grid=(N_DEV, L // tm),
        a_spec=pl.BlockSpec((tm, K), lambda s, i: (i, 0)),
        b_spec=pl.BlockSpec((None, K, n), lambda s, i: (s, 0, 0)),
        o_spec=o_spec, o_block=(tm, n), out_shape=out_shape, nk=1, name=name,
        extras=extras, epilogue=epi, after=after)


def _mm_cols_nt(du, wg, *, name):
    L = du.shape[-2]
    K, n = wg.shape[-2:]
    tm = _tile(L, 512)
    if du.ndim == 3:
        a_spec = pl.BlockSpec((None, tm, n), lambda i, s: (s // HALF_SLABS, i, s % HALF_SLABS))
    else:
        a_spec = pl.BlockSpec((tm, n), lambda i, s: (i, s))
    return _matmul(
        du, wg, mode="nt", grid=(L // tm, N_DEV), a_spec=a_spec,
        b_spec=pl.BlockSpec((None, K, n), lambda i, s: (s, 0, 0)),
        o_spec=pl.BlockSpec((tm, K), lambda i, s: (i, 0)), o_block=(tm, K),
        out_shape=jax.ShapeDtypeStruct((L, K), F32), nk=N_DEV, name=name)


def _mm_cols_tn(h, du, *, name):
    L, K = h.shape
    tm = _tile(L, 512)
    kh = K // 2
    if du.ndim == 3:
        n = du.shape[2] // HALF_SLABS
        b_spec = pl.BlockSpec((None, tm, n), lambda s, q, t: (s // HALF_SLABS, t, s % HALF_SLABS))
    else:
        n = du.shape[1] // N_DEV
        b_spec = pl.BlockSpec((tm, n), lambda s, q, t: (t, s))
    return _matmul(
        h, du, mode="tn", grid=(N_DEV, 2, L // tm),
        a_spec=pl.BlockSpec((tm, kh), lambda s, q, t: (t, q)), b_spec=b_spec,
        o_spec=pl.BlockSpec((None, kh, n), lambda s, q, t: (s, q, 0)), o_block=(kh, n),
        out_shape=jax.ShapeDtypeStruct((N_DEV, K, n), BF16), nk=L // tm, name=name)


def _mm_rows_nn(a, wg, *, res, name, bias=None):
    L, Kw = a.shape
    N = wg.shape[-1]
    tm, tk = _tile(L, 512), 512
    extras = [(res, pl.BlockSpec((tm, N), lambda i, k: (i, 0)))]
    if bias is not None:
        extras.append((bias, pl.BlockSpec((1, N), lambda i, k: (0, 0))))
        epi = lambda acc, r, b: acc + r + b
    else:
        epi = lambda acc, r: acc + r
    return _matmul(
        a, wg, mode="nn", grid=(L // tm, Kw // tk),
        a_spec=pl.BlockSpec((tm, tk), lambda i, k: (i, k)),
        b_spec=pl.BlockSpec((tk, N), lambda i, k: (k, 0)),
        o_spec=pl.BlockSpec((tm, N), lambda i, k: (i, 0)), o_block=(tm, N),
        out_shape=jax.ShapeDtypeStruct((L, N), F32), nk=Kw // tk, name=name,
        extras=tuple(extras), epilogue=epi)


def _mm_rows_nt(dy, wg, *, name, after=None):
    L, N = dy.shape
    Kw = wg.shape[-2]
    tm, tn = _tile(L, 512), 512
    return _matmul(
        dy, wg, mode="nt", grid=(L // tm, Kw // tn),
        a_spec=pl.BlockSpec((tm, N), lambda i, q: (i, 0)),
        b_spec=pl.BlockSpec((tn, N), lambda i, q: (q, 0)),
        o_spec=pl.BlockSpec((tm, tn), lambda i, q: (i, q)), o_block=(tm, tn),
        out_shape=jax.ShapeDtypeStruct((L, Kw), F32), nk=1, name=name, after=after)


def _mm_rows_tn(a, dy, *, name):
    L, Kw = a.shape
    N = dy.shape[1]
    tm, tq = _tile(L, 512), 512
    return _matmul(
        a, dy, mode="tn", grid=(Kw // tq, L // tm),
        a_spec=pl.BlockSpec((tm, tq), lambda q, t: (t, q)),
        b_spec=pl.BlockSpec((tm, N), lambda q, t: (t, 0)),
        o_spec=pl.BlockSpec((tq, N), lambda q, t: (q, 0)), o_block=(tq, N),
        out_shape=jax.ShapeDtypeStruct((Kw, N), BF16), nk=L // tm, name=name)


def _rms_fwd(x, g, *, name):
    L, Dm = x.shape
    tm = _tile(L, 256)

    def body(x_ref, g_ref, h_ref):
        xv = x_ref[...]
        r = lax.rsqrt(jnp.mean(xv * xv, axis=-1, keepdims=True) + RMS_EPS)
        h_ref[...] = (xv * r * g_ref[...]).astype(BF16)

    return pl.pallas_call(
        body, name=name, grid=(L // tm,),
        in_specs=[pl.BlockSpec((tm, Dm), lambda i: (i, 0)), pl.BlockSpec((1, Dm), lambda i: (0, 0))],
        out_specs=pl.BlockSpec((tm, Dm), lambda i: (i, 0)),
        out_shape=jax.ShapeDtypeStruct((L, Dm), BF16),
        compiler_params=_params(("parallel",)),
    )(x, g)


def _rms_bwd(x, g, dh, dres, *, name):
    L, Dm = x.shape
    tm = _tile(L, 256)

    def body(x_ref, g_ref, dh_ref, dres_ref, dx_ref, dg_ref, cs_ref):
        i = pl.program_id(0)
        xv = x_ref[...]
        r = lax.rsqrt(jnp.mean(xv * xv, axis=-1, keepdims=True) + RMS_EPS)
        xh = xv * r
        dh = dh_ref[...]
        dxh = dh * g_ref[...]
        dx = dres_ref[...] + r * (dxh - xh * jnp.mean(dxh * xh, axis=-1, keepdims=True))
        dx_ref[...] = dx

        @pl.when(i == 0)
        def _():
            dg_ref[...] = jnp.zeros_like(dg_ref)
            cs_ref[...] = jnp.zeros_like(cs_ref)

        dg_ref[...] += jnp.sum(dh * xh, axis=0, keepdims=True)
        cs_ref[...] += jnp.sum(dx, axis=0, keepdims=True)

    row = pl.BlockSpec((tm, Dm), lambda i: (i, 0))
    vec = pl.BlockSpec((1, Dm), lambda i: (0, 0))
    return pl.pallas_call(
        body, name=name, grid=(L // tm,),
        in_specs=[row, vec, row, row], out_specs=[row, vec, vec],
        out_shape=[jax.ShapeDtypeStruct((L, Dm), F32), jax.ShapeDtypeStruct((1, Dm), F32),
                   jax.ShapeDtypeStruct((1, Dm), F32)],
        compiler_params=_params(("arbitrary",)),
    )(x, g, dh, dres)


def _final_loss(x, g, target, *, name):
    L, Dm = x.shape
    tm = _tile(L, 256)

    def body(x_ref, g_ref, t_ref, dx_ref, dg_ref, loss_ref):
        i = pl.program_id(0)
        xv = x_ref[...]
        gv = g_ref[...]
        r = lax.rsqrt(jnp.mean(xv * xv, axis=-1, keepdims=True) + RMS_EPS)
        xh = xv * r
        err = xh * gv - t_ref[...]
        dy = err * (1.0 / Dm)
        dxh = dy * gv
        dx_ref[...] = r * (dxh - xh * jnp.mean(dxh * xh, axis=-1, keepdims=True))

        @pl.when(i == 0)
        def _():
            dg_ref[...] = jnp.zeros_like(dg_ref)
            loss_ref[...] = jnp.zeros_like(loss_ref)

        dg_ref[...] += jnp.sum(dy * xh, axis=0, keepdims=True)
        loss_ref[...] += 0.5 * jnp.sum(jnp.mean(err * err, axis=-1, keepdims=True), axis=0, keepdims=True)

    row = pl.BlockSpec((tm, Dm), lambda i: (i, 0))
    vec = pl.BlockSpec((1, Dm), lambda i: (0, 0))
    return pl.pallas_call(
        body, name=name, grid=(L // tm,),
        in_specs=[row, vec, row],
        out_specs=[row, vec, pl.BlockSpec((1, 128), lambda i: (0, 0))],
        out_shape=[jax.ShapeDtypeStruct((L, Dm), F32), jax.ShapeDtypeStruct((1, Dm), F32),
                   jax.ShapeDtypeStruct((1, 128), F32)],
        compiler_params=_params(("arbitrary",)),
    )(x, g, target)


def _conv_from_ext(ext_ref, w_ref, bias, taps, halo, rows):
    acc = jnp.broadcast_to(bias, (rows, ext_ref.shape[1]))
    for j in range(taps):
        acc = acc + w_ref[taps - 1 - j:taps - j, :] * ext_ref[halo - j:halo - j + rows, :]
    return acc


def _conv_back(dext_ref, x_cur, w_ref, dw_ref, taps, rows):
    dx = jnp.zeros((rows, dext_ref.shape[1]), F32)
    for j in range(taps):
        sh = dext_ref[j:j + rows, :]
        dx = dx + w_ref[taps - 1 - j:taps - j, :] * sh
        dw_ref[taps - 1 - j:taps - j, :] += jnp.sum(x_cur * sh, axis=0, keepdims=True)
    return dx


def _prev_blk(i, r):
    return jnp.maximum(i * r - 1, 0)


def _next_blk(i, r, nblk):
    return jnp.minimum((i + 1) * r, nblk - 1)


CV_HALO = 32


def _cv_mid_fwd(u, w_dw, b_dw, ln_g, ln_b, *, name):
    L = u.shape[0]
    Dm = D_MODEL
    tm = _tile(L, 256)
    r = tm // CV_HALO

    def body(a_ref, g_ref, ah_ref, gh_ref, w_ref, bdw_ref, lg_ref, lb_ref, s_ref, c_ref, ext_ref):
        i = pl.program_id(0)
        keep = (i > 0).astype(F32)
        ext_ref[0:CV_HALO, :] = ah_ref[...] * _sigmoid(gh_ref[...]) * keep
        ext_ref[CV_HALO:CV_HALO + tm, :] = a_ref[...] * _sigmoid(g_ref[...])
        c = _conv_from_ext(ext_ref, w_ref, bdw_ref[...], CONV_K, CV_HALO, tm)
        c_ref[...] = c
        mu = jnp.mean(c, axis=-1, keepdims=True)
        xc = c - mu
        rstd = lax.rsqrt(jnp.mean(xc * xc, axis=-1, keepdims=True) + LN_EPS)
        l = xc * rstd * lg_ref[...] + lb_ref[...]
        s_ref[...] = (l * _sigmoid(l)).astype(BF16)

    vec = pl.BlockSpec((1, Dm), lambda i: (0, 0))
    return pl.pallas_call(
        body, name=name, grid=(L // tm,),
        in_specs=[pl.BlockSpec((tm, Dm), lambda i: (i, 0)), pl.BlockSpec((tm, Dm), lambda i: (i, 1)),
                  pl.BlockSpec((CV_HALO, Dm), lambda i: (_prev_blk(i, r), 0)),
                  pl.BlockSpec((CV_HALO, Dm), lambda i: (_prev_blk(i, r), 1)),
                  pl.BlockSpec((CONV_K, Dm), lambda i: (0, 0)), vec, vec, vec],
        out_specs=[pl.BlockSpec((tm, Dm), lambda i: (i, 0)), pl.BlockSpec((tm, Dm), lambda i: (i, 0))],
        out_shape=[jax.ShapeDtypeStruct((L, Dm), BF16), jax.ShapeDtypeStruct((L, Dm), F32)],
        scratch_shapes=[pltpu.VMEM((CV_HALO + tm, Dm), F32)],
        compiler_params=_params(("parallel",)),
    )(u, u, u, u, w_dw, b_dw, ln_g, ln_b)


def _cv_bwd(ds, c, u, w_dw, ln_g, ln_b, *, name):
    L = u.shape[0]
    Dm = D_MODEL
    tm = _tile(L, 128)
    r = tm // CV_HALO
    nt = L // tm
    nblk = L // CV_HALO
    ext_rows = tm + CV_HALO

    def body(ds_ref, dsn_ref, c_ref, cn_ref, a_ref, g_ref, w_ref, lg_ref, lb_ref,
             du_ref, dw_ref, dbdw_ref, dlg_ref, dlb_ref, dbin_ref, dsx_ref, cx_ref, dext_ref):
        i = pl.program_id(0)

        @pl.when(i == 0)
        def _():
            for ref in (dw_ref, dbdw_ref, dlg_ref, dlb_ref, dbin_ref):
                ref[...] = jnp.zeros_like(ref)

        keep = (i < nt - 1).astype(F32)
        dsx_ref[0:tm, :] = ds_ref[...]
        dsx_ref[tm:ext_rows, :] = dsn_ref[...] * keep
        cx_ref[0:tm, :] = c_ref[...]
        cx_ref[tm:ext_rows, :] = cn_ref[...]
        cv = cx_ref[...]
        mu = jnp.mean(cv, axis=-1, keepdims=True)
        xc = cv - mu
        rstd = lax.rsqrt(jnp.mean(xc * xc, axis=-1, keepdims=True) + LN_EPS)
        nrm = xc * rstd
        lg = lg_ref[...]
        l = nrm * lg + lb_ref[...]
        dl = dsx_ref[...] * _dsilu(l, _sigmoid(l))
        dn = dl * lg
        dc = rstd * (dn - jnp.mean(dn, axis=-1, keepdims=True)
                     - nrm * jnp.mean(dn * nrm, axis=-1, keepdims=True))
        dext_ref[...] = dc
        dlg_ref[...] += jnp.sum((dl * nrm)[0:tm], axis=0, keepdims=True)
        dlb_ref[...] += jnp.sum(dl[0:tm], axis=0, keepdims=True)
        dbdw_ref[...] += jnp.sum(dc[0:tm], axis=0, keepdims=True)

        av, gv = a_ref[...], g_ref[...]
        sg = _sigmoid(gv)
        dv = _conv_back(dext_ref, av * sg, w_ref, dw_ref, CONV_K, tm)
        da = dv * sg
        dgate = dv * av * sg * (1.0 - sg)
        du_ref[:, 0:Dm] = da.astype(BF16)
        du_ref[:, Dm:2 * Dm] = dgate.astype(BF16)
        dbin_ref[:, 0:Dm] += jnp.sum(da, axis=0, keepdims=True)
        dbin_ref[:, Dm:2 * Dm] += jnp.sum(dgate, axis=0, keepdims=True)

    row = pl.BlockSpec((tm, Dm), lambda i: (i, 0))
    nxt = pl.BlockSpec((CV_HALO, Dm), lambda i: (_next_blk(i, r, nblk), 0))
    vec = pl.BlockSpec((1, Dm), lambda i: (0, 0))
    return pl.pallas_call(
        body, name=name, grid=(nt,),
        in_specs=[row, nxt, row, nxt, row, pl.BlockSpec((tm, Dm), lambda i: (i, 1)),
                  pl.BlockSpec((CONV_K, Dm), lambda i: (0, 0)), vec, vec],
        out_specs=[pl.BlockSpec((tm, 2 * Dm), lambda i: (i, 0)), pl.BlockSpec((CONV_K, Dm), lambda i: (0, 0)),
                   vec, vec, vec, pl.BlockSpec((1, 2 * Dm), lambda i: (0, 0))],
        out_shape=[jax.ShapeDtypeStruct((L, 2 * Dm), BF16), jax.ShapeDtypeStruct((CONV_K, Dm), F32),
                   jax.ShapeDtypeStruct((1, Dm), F32), jax.ShapeDtypeStruct((1, Dm), F32),
                   jax.ShapeDtypeStruct((1, Dm), F32), jax.ShapeDtypeStruct((1, 2 * Dm), F32)],
        scratch_shapes=[pltpu.VMEM((ext_rows, Dm), F32)] * 3,
        compiler_params=_params(("arbitrary",)),
    )(ds, ds, c, c, u, u, w_dw, ln_g, ln_b)


FFN_TC = 512
FFN_NJ = FFN_HIDDEN // FFN_TC
HALO8 = 8


def _ffn_mid_fwd(u0, w_dw, b_dw, *, name):
    L = u0.shape[1]
    tm = _tile(L, 256)
    r = tm // HALO8
    tc, nj = FFN_TC, FFN_NJ

    def body(u_ref, uh_ref, w_ref, b_ref, o_ref, ext_ref):
        keep = (pl.program_id(0) > 0).astype(F32)
        conv = []
        for h in range(2):
            ext = ext_ref.at[h]
            ext[0:HALO8, :] = uh_ref[h] * keep
            ext[HALO8:HALO8 + tm, :] = u_ref[h]
            conv.append(_conv_from_ext(ext, w_ref.at[h], b_ref[h], FFN_K, HALO8, tm))
        ug, uv = conv
        o_ref[...] = (ug * _sigmoid(ug) * uv).astype(BF16)

    return pl.pallas_call(
        body, name=name, grid=(L // tm, nj),
        in_specs=[pl.BlockSpec((2, tm, tc), lambda i, j: (0, i, j)),
                  pl.BlockSpec((2, HALO8, tc), lambda i, j: (0, _prev_blk(i, r), j)),
                  pl.BlockSpec((2, FFN_K, tc), lambda i, j: (0, 0, j)), pl.BlockSpec((2, 1, tc), lambda i, j: (0, 0, j))],
        out_specs=pl.BlockSpec((tm, tc), lambda i, j: (i, j)),
        out_shape=jax.ShapeDtypeStruct((L, FFN_HIDDEN), BF16),
        scratch_shapes=[pltpu.VMEM((2, HALO8 + tm, tc), F32)],
        compiler_params=_params(("parallel", "parallel")),
    )(u0, u0, w_dw, b_dw)


def _ffn_bwd(dg, u0, w_dw, b_dw, *, name):
    L = u0.shape[1]
    tm = _tile(L, 256)
    r = tm // HALO8
    nt = L // tm
    nblk = L // HALO8
    tc, nj = FFN_TC, FFN_NJ
    rows = tm + HALO8

    def body(up_ref, uc_ref, un_ref, dg_ref, dgn_ref, w_ref, b_ref, du0_ref, dw_ref, db_ref, ext_ref, dext_ref):
        i = pl.program_id(1)

        @pl.when(i == 0)
        def _():
            dw_ref[...] = jnp.zeros_like(dw_ref)
            db_ref[...] = jnp.zeros_like(db_ref)

        keep_prev = (i > 0).astype(F32)
        keep_next = (i < nt - 1).astype(F32)
        conv = []
        for h in range(2):
            ext = ext_ref.at[h]
            ext[0:HALO8, :] = up_ref[h] * keep_prev
            ext[HALO8:HALO8 + tm, :] = uc_ref[h]
            ext[HALO8 + tm:HALO8 + rows, :] = un_ref[h]
            conv.append(_conv_from_ext(ext, w_ref.at[h], b_ref[h], FFN_K, HALO8, rows))
        ug, uv = conv
        dgx = jnp.concatenate([dg_ref[...], dgn_ref[...] * keep_next], axis=0)
        sg = _sigmoid(ug)
        for h, du in ((0, dgx * uv * _dsilu(ug, sg)), (1, dgx * (ug * sg))):
            dext = dext_ref.at[h]
            dext[...] = du
            dbh = db_ref.at[h]
            dbh[...] += jnp.sum(du[0:tm], axis=0, keepdims=True)
            du0_ref[h] = _conv_back(dext, uc_ref[h], w_ref.at[h], dw_ref.at[h], FFN_K, tm).astype(BF16)

    cur = pl.BlockSpec((2, tm, tc), lambda j, i: (0, i, j))
    prv = pl.BlockSpec((2, HALO8, tc), lambda j, i: (0, _prev_blk(i, r), j))
    nxt = pl.BlockSpec((2, HALO8, tc), lambda j, i: (0, _next_blk(i, r, nblk), j))
    wsp = pl.BlockSpec((2, FFN_K, tc), lambda j, i: (0, 0, j))
    bsp = pl.BlockSpec((2, 1, tc), lambda j, i: (0, 0, j))
    return pl.pallas_call(
        body, name=name, grid=(nj, nt),
        in_specs=[prv, cur, nxt, pl.BlockSpec((tm, tc), lambda j, i: (i, j)),
                  pl.BlockSpec((HALO8, tc), lambda j, i: (_next_blk(i, r, nblk), j)), wsp, bsp],
        out_specs=[cur, wsp, bsp],
        out_shape=[jax.ShapeDtypeStruct((2, L, FFN_HIDDEN), BF16), jax.ShapeDtypeStruct((2, FFN_K, FFN_HIDDEN), F32),
                   jax.ShapeDtypeStruct((2, 1, FFN_HIDDEN), F32)],
        scratch_shapes=[pltpu.VMEM((2, HALO8 + rows, tc), F32), pltpu.VMEM((2, rows, tc), F32)],
        compiler_params=_params(("parallel", "arbitrary")),
    )(u0, u0, u0, dg, dg, w_dw, b_dw)


def _ssm_unpad(zx_pad, *, name):
    L = zx_pad.shape[0]
    tm = _tile(L, 128)

    def body(p_ref, z_ref, xbc_ref, dt_ref, comp_ref):
        comp_ref[...] = jnp.zeros_like(comp_ref)
        for d in range(N_DEV):
            slab = p_ref[:, d * SSM_SLAB_PAD:(d + 1) * SSM_SLAB_PAD]
            if d:
                slab = pltpu.roll(slab, 8 * d, axis=1)
            comp_ref[:, 1280 * d:1280 * d + SSM_SLAB_PAD] += slab
        z_ref[...] = comp_ref[:, 0:SSM_INNER]
        xbc_ref[...] = comp_ref[:, SSM_INNER:SSM_INNER + SSM_CONV_DIM]
        dt_ref[...] = comp_ref[:, SSM_INNER + SSM_CONV_DIM:SSM_COMPACT]

    return pl.pallas_call(
        body, name=name, grid=(L // tm,),
        in_specs=[pl.BlockSpec((tm, N_DEV * SSM_SLAB_PAD), lambda i: (i, 0))],
        out_specs=[pl.BlockSpec((tm, SSM_INNER), lambda i: (i, 0)), pl.BlockSpec((tm, SSM_CONV_DIM), lambda i: (i, 0)),
                   pl.BlockSpec((tm, HEAD_LANES), lambda i: (i, 0))],
        out_shape=[jax.ShapeDtypeStruct((L, SSM_INNER), F32), jax.ShapeDtypeStruct((L, SSM_CONV_DIM), F32),
                   jax.ShapeDtypeStruct((L, HEAD_LANES), F32)],
        scratch_shapes=[pltpu.VMEM((tm, SSM_COMPACT), F32)],
        compiler_params=_params(("parallel",)),
    )(zx_pad)


def _ssm_pad(dz, dxbc, ddt, *, name):
    L = dz.shape[0]
    tm = _tile(L, 128)

    def body(dz_ref, dx_ref, dt_ref, p_ref, comp_ref):
        comp_ref[:, 0:SSM_INNER] = dz_ref[...]
        comp_ref[:, SSM_INNER:SSM_INNER + SSM_CONV_DIM] = dx_ref[...]
        lane = lax.broadcasted_iota(jnp.int32, (tm, HEAD_LANES), 1)
        comp_ref[:, SSM_INNER + SSM_CONV_DIM:SSM_COMPACT] = jnp.where(lane < SSM_HEADS, dt_ref[...], 0.0)
        col = lax.broadcasted_iota(jnp.int32, (tm, SSM_SLAB_PAD), 1)
        for d in range(N_DEV):
            win = comp_ref[:, 1280 * d:1280 * d + SSM_SLAB_PAD]
            if d:
                win = pltpu.roll(win, SSM_SLAB_PAD - 8 * d, axis=1)
            p_ref[:, d * SSM_SLAB_PAD:(d + 1) * SSM_SLAB_PAD] = jnp.where(col < SSM_SLAB, win, 0.0).astype(BF16)

    return pl.pallas_call(
        body, name=name, grid=(L // tm,),
        in_specs=[pl.BlockSpec((tm, SSM_INNER), lambda i: (i, 0)), pl.BlockSpec((tm, SSM_CONV_DIM), lambda i: (i, 0)),
                  pl.BlockSpec((tm, HEAD_LANES), lambda i: (i, 0))],
        out_specs=pl.BlockSpec((tm, N_DEV * SSM_SLAB_PAD), lambda i: (i, 0)),
        out_shape=jax.ShapeDtypeStruct((L, N_DEV * SSM_SLAB_PAD), BF16),
        scratch_shapes=[pltpu.VMEM((tm, SSM_COMPACT), F32)],
        compiler_params=_params(("parallel",)),
    )(dz, dxbc, ddt)


SSM_TC = 1024


def _ssm_conv_fwd(xpre, w, b, *, name):
    L, C = xpre.shape
    tm = _tile(L, 256)
    r = tm // HALO8
    tc = SSM_TC

    def body(x_ref, xh_ref, w_ref, b_ref, o_ref, ext_ref):
        keep = (pl.program_id(0) > 0).astype(F32)
        ext_ref[0:HALO8, :] = xh_ref[...] * keep
        ext_ref[HALO8:HALO8 + tm, :] = x_ref[...]
        pre = _conv_from_ext(ext_ref, w_ref, b_ref[...], SSM_CONV_K, HALO8, tm)
        o_ref[...] = pre * _sigmoid(pre)

    return pl.pallas_call(
        body, name=name, grid=(L // tm, C // tc),
        in_specs=[pl.BlockSpec((tm, tc), lambda i, j: (i, j)), pl.BlockSpec((HALO8, tc), lambda i, j: (_prev_blk(i, r), j)),
                  pl.BlockSpec((SSM_CONV_K, tc), lambda i, j: (0, j)), pl.BlockSpec((1, tc), lambda i, j: (0, j))],
        out_specs=pl.BlockSpec((tm, tc), lambda i, j: (i, j)),
        out_shape=jax.ShapeDtypeStruct((L, C), F32),
        scratch_shapes=[pltpu.VMEM((HALO8 + tm, tc), F32)],
        compiler_params=_params(("parallel", "parallel")),
    )(xpre, xpre, w, b)


def _ssm_conv_bwd(dy, xpre, w, b, *, name):
    L, C = xpre.shape
    tm = _tile(L, 256)
    r = tm // HALO8
    nt = L // tm
    nblk = L // HALO8
    tc = SSM_TC
    rows = tm + HALO8

    def body(xp_ref, xc_ref, xn_ref, dy_ref, dyn_ref, w_ref, b_ref, dx_ref, dw_ref, db_ref, ext_ref, dext_ref):
        i = pl.program_id(1)

        @pl.when(i == 0)
        def _():
            dw_ref[...] = jnp.zeros_like(dw_ref)
            db_ref[...] = jnp.zeros_like(db_ref)

        ext_ref[0:HALO8, :] = xp_ref[...] * (i > 0).astype(F32)
        ext_ref[HALO8:HALO8 + tm, :] = xc_ref[...]
        ext_ref[HALO8 + tm:HALO8 + rows, :] = xn_ref[...]
        pre = _conv_from_ext(ext_ref, w_ref, b_ref[...], SSM_CONV_K, HALO8, rows)
        dyx = jnp.concatenate([dy_ref[...], dyn_ref[...] * (i < nt - 1).astype(F32)], axis=0)
        dpre = dyx * _dsilu(pre, _sigmoid(pre))
        dext_ref[...] = dpre
        db_ref[...] += jnp.sum(dpre[0:tm], axis=0, keepdims=True)
        dx_ref[...] = _conv_back(dext_ref, xc_ref[...], w_ref, dw_ref, SSM_CONV_K, tm)

    cur = pl.BlockSpec((tm, tc), lambda j, i: (i, j))
    prv = pl.BlockSpec((HALO8, tc), lambda j, i: (_prev_blk(i, r), j))
    nxt = pl.BlockSpec((HALO8, tc), lambda j, i: (_next_blk(i, r, nblk), j))
    return pl.pallas_call(
        body, name=name, grid=(C // tc, nt),
        in_specs=[prv, cur, nxt, cur, nxt, pl.BlockSpec((SSM_CONV_K, tc), lambda j, i: (0, j)),
                  pl.BlockSpec((1, tc), lambda j, i: (0, j))],
        out_specs=[cur, pl.BlockSpec((SSM_CONV_K, tc), lambda j, i: (0, j)), pl.BlockSpec((1, tc), lambda j, i: (0, j))],
        out_shape=[jax.ShapeDtypeStruct((L, C), F32), jax.ShapeDtypeStruct((SSM_CONV_K, C), F32),
                   jax.ShapeDtypeStruct((1, C), F32)],
        scratch_shapes=[pltpu.VMEM((HALO8 + rows, tc), F32), pltpu.VMEM((rows, tc), F32)],
        compiler_params=_params(("parallel", "arbitrary")),
    )(xpre, xpre, xpre, dy, dy, w, b)


def _gnorm_fwd(y, z, g, *, name):
    L, C = y.shape
    tm = _tile(L, 256)
    gw = SSM_GROUP_W

    def body(y_ref, z_ref, g_ref, o_ref):
        for k in range(SSM_GROUPS):
            sl = slice(k * gw, (k + 1) * gw)
            zv = z_ref[:, sl]
            yz = y_ref[:, sl] * zv * _sigmoid(zv)
            r = lax.rsqrt(jnp.mean(yz * yz, axis=-1, keepdims=True) + RMS_EPS)
            o_ref[:, sl] = (yz * r * g_ref[:, sl]).astype(BF16)

    row = pl.BlockSpec((tm, C), lambda i: (i, 0))
    return pl.pallas_call(
        body, name=name, grid=(L // tm,),
        in_specs=[row, row, pl.BlockSpec((1, C), lambda i: (0, 0))], out_specs=row,
        out_shape=jax.ShapeDtypeStruct((L, C), BF16),
        compiler_params=_params(("parallel",)),
    )(y, z, g)


def _gnorm_bwd(dn, y, z, g, *, name):
    L, C = y.shape
    tm = _tile(L, 256)
    gw = SSM_GROUP_W

    def body(dn_ref, y_ref, z_ref, g_ref, dy_ref, dz_ref, dg_ref):
        @pl.when(pl.program_id(0) == 0)
        def _():
            dg_ref[...] = jnp.zeros_like(dg_ref)

        for k in range(SSM_GROUPS):
            sl = slice(k * gw, (k + 1) * gw)
            zv, yv = z_ref[:, sl], y_ref[:, sl]
            sz = _sigmoid(zv)
            silu = zv * sz
            yz = yv * silu
            r = lax.rsqrt(jnp.mean(yz * yz, axis=-1, keepdims=True) + RMS_EPS)
            nrm = yz * r
            dnv = dn_ref[:, sl]
            dg_ref[:, sl] += jnp.sum(dnv * nrm, axis=0, keepdims=True)
            dh = dnv * g_ref[:, sl]
            dyz = r * (dh - nrm * jnp.mean(dh * nrm, axis=-1, keepdims=True))
            dy_ref[:, sl] = dyz * silu
            dz_ref[:, sl] = dyz * yv * _dsilu(zv, sz)

    row = pl.BlockSpec((tm, C), lambda i: (i, 0))
    vec = pl.BlockSpec((1, C), lambda i: (0, 0))
    return pl.pallas_call(
        body, name=name, grid=(L // tm,),
        in_specs=[row, row, row, vec], out_specs=[row, row, vec],
        out_shape=[jax.ShapeDtypeStruct((L, C), F32), jax.ShapeDtypeStruct((L, C), F32),
                   jax.ShapeDtypeStruct((1, C), F32)],
        compiler_params=_params(("arbitrary",)),
    )(dn, y, z, g)


def _ssd_consts():
    q, gw = CHUNK, SSM_GROUP_W
    sub = lax.broadcasted_iota(jnp.int32, (q, gw), 0)
    lane_in = lax.broadcasted_iota(jnp.int32, (q, gw), 1) % q
    sub2 = lax.broadcasted_iota(jnp.int32, (gw, gw), 0) // q
    lane2 = lax.broadcasted_iota(jnp.int32, (gw, gw), 1) // q
    t = lax.broadcasted_iota(jnp.int32, (q, q), 0)
    u = lax.broadcasted_iota(jnp.int32, (q, q), 1)
    return dict(
        diag=(lane_in == sub), low=(lane_in <= sub), up=(sub <= lane_in), block=(sub2 == lane2),
        tri=(u <= t).astype(BF16), trit=(u >= t).astype(BF16), ones=jnp.ones((q, q), BF16),
        last=(lax.broadcasted_iota(jnp.int32, (q, HEAD_LANES), 0) == q - 1))


def _ssd_chunk_terms(dt_raw, bias, a_log, e, et, k):
    dt = _softplus(dt_raw + bias)
    a_neg = -jnp.exp(a_log)
    cs = _dot3_r(k["tri"], dt * a_neg)
    cs_last = cs[CHUNK - 1:CHUNK, :]
    ecs = jnp.exp(cs)
    dte = jnp.exp(cs_last - cs)
    ecl_hn = _dot3_l(ecs, k["last"].astype(BF16), "tn")
    rows = _dot3_r(et, ecl_hn)
    return dict(dt=dt, a_neg=a_neg, cs=cs, ecs=ecs, dte=dte, rows=rows,
                dtx=_dot3_l(dt, e), csx=_dot3_l(cs, e), ecsx=_dot3_l(ecs, e), dtex=_dot3_l(dte, e))


def _tile8(x):
    return jnp.concatenate([x] * 8, axis=0)


def _ssd_fwd(xbc, dt_raw, dt_bias, a_log, d_x, e, et, *, name):
    L = xbc.shape[0]
    nc = L // CHUNK
    q, gw, ns = CHUNK, SSM_GROUP_W, SSM_STATE

    def body(xbc_ref, dt_ref, bias_ref, alog_ref, dx_ref, e_ref, et_ref, y_ref, st_ref, s_ref):
        @pl.when(pl.program_id(0) == 0)
        def _():
            s_ref[...] = jnp.zeros_like(s_ref)

        k = _ssd_consts()
        t = _ssd_chunk_terms(dt_ref[...], bias_ref[...], alog_ref[...], e_ref[...], et_ref[...], k)
        st_ref[...] = s_ref[...]
        for g in range(SSM_GROUPS):
            ch = slice(g * gw, (g + 1) * gw)
            xs = xbc_ref[:, ch]
            bm = xbc_ref[:, SSM_INNER + g * ns:SSM_INNER + (g + 1) * ns].astype(BF16)
            cm = xbc_ref[:, SSM_INNER + (SSM_GROUPS + g) * ns:SSM_INNER + (SSM_GROUPS + g + 1) * ns].astype(BF16)
            xd = xs * t["dtx"][:, ch]
            csx = t["csx"][:, ch]
            csrow = _dot3_r(k["ones"], jnp.where(k["diag"], csx, 0.0))
            lcat = jnp.where(k["low"], jnp.exp(jnp.minimum(csx - csrow, 0.0)), 0.0)
            mcat = _dot(cm, _tile8(bm), "nt") * lcat
            xdbd = jnp.where(k["block"], _tile8(xd), 0.0).astype(BF16)
            sg = s_ref[ch, :]
            y = _dot(mcat.astype(BF16), xdbd)
            y = y + _dot(cm, sg.astype(BF16), "nt") * t["ecsx"][:, ch]
            y_ref[:, ch] = y + dx_ref[:, ch] * xs
            s_ref[ch, :] = sg * t["rows"][ch, :] + _dot((xd * t["dtex"][:, ch]).astype(BF16), bm, "tn")

    hv = pl.BlockSpec((1, HEAD_LANES), lambda c: (0, 0))
    return pl.pallas_call(
        body, name=name, grid=(nc,),
        in_specs=[pl.BlockSpec((q, SSM_CONV_DIM), lambda c: (c, 0)), pl.BlockSpec((q, HEAD_LANES), lambda c: (c, 0)),
                  hv, hv, pl.BlockSpec((1, SSM_INNER), lambda c: (0, 0)),
                  pl.BlockSpec((HEAD_LANES, SSM_INNER), lambda c: (0, 0)),
                  pl.BlockSpec((SSM_INNER, HEAD_LANES), lambda c: (0, 0))],
        out_specs=[pl.BlockSpec((q, SSM_INNER), lambda c: (c, 0)),
                   pl.BlockSpec((None, SSM_INNER, ns), lambda c: (c, 0, 0))],
        out_shape=[jax.ShapeDtypeStruct((L, SSM_INNER), F32), jax.ShapeDtypeStruct((nc, SSM_INNER, ns), F32)],
        scratch_shapes=[pltpu.VMEM((SSM_INNER, ns), F32)],
        compiler_params=_params(("arbitrary",)),
    )(xbc, dt_raw, dt_bias, a_log, d_x, e, et)


def _ssd_bwd(dy, xbc, dt_raw, states, dt_bias, a_log, d_x, e, et, *, name):
    L = xbc.shape[0]
    nc = L // CHUNK
    q, gw, ns = CHUNK, SSM_GROUP_W, SSM_STATE

    def body(dy_ref, xbc_ref, dt_ref, st_ref, bias_ref, alog_ref, dx_ref, e_ref, et_ref,
             dxbc_ref, ddt_ref, dbias_ref, dalog_ref, dd_ref, ds_ref, f1_ref, f2_ref, f3_ref, f4_ref, fs_ref):
        @pl.when(pl.program_id(0) == 0)
        def _():
            ds_ref[...] = jnp.zeros_like(ds_ref)
            for ref in (dbias_ref, dalog_ref, dd_ref):
                ref[...] = jnp.zeros_like(ref)

        k = _ssd_consts()
        ev = e_ref[...]
        t = _ssd_chunk_terms(dt_ref[...], bias_ref[...], alog_ref[...], ev, et_ref[...], k)
        ones8 = jnp.ones((8, ns), BF16)
        for g in range(SSM_GROUPS):
            ch = slice(g * gw, (g + 1) * gw)
            bsl = slice(SSM_INNER + g * ns, SSM_INNER + (g + 1) * ns)
            csl = slice(SSM_INNER + (SSM_GROUPS + g) * ns, SSM_INNER + (SSM_GROUPS + g + 1) * ns)
            xs = xbc_ref[:, ch]
            bm = xbc_ref[:, bsl].astype(BF16)
            cm = xbc_ref[:, csl].astype(BF16)
            dyv = dy_ref[:, ch]
            dtx, ecsx, dtex, csx = t["dtx"][:, ch], t["ecsx"][:, ch], t["dtex"][:, ch], t["csx"][:, ch]
            xd = xs * dtx
            csrow = _dot3_r(k["ones"], jnp.where(k["diag"], csx, 0.0))
            lcat = jnp.where(k["low"], jnp.exp(jnp.minimum(csx - csrow, 0.0)), 0.0)
            ltcat = jnp.where(k["up"], jnp.exp(jnp.minimum(csrow - csx, 0.0)), 0.0)
            mcat = _dot(cm, _tile8(bm), "nt") * lcat
            mtcat = _dot(bm, _tile8(cm), "nt") * ltcat
            xdbd = jnp.where(k["block"], _tile8(xd), 0.0).astype(BF16)
            dybd = jnp.where(k["block"], _tile8(dyv), 0.0).astype(BF16)
            gq = _dot(dyv.astype(BF16), xdbd, "nt") * mcat
            gt = _dot(xd.astype(BF16), dybd, "nt")
            gqt = gt * mtcat
            dcbt = _dot((gt * ltcat).astype(BF16), jnp.where(k["diag"], 1.0, 0.0).astype(BF16), "nt")
            dcbt = dcbt.astype(BF16)
            sg = st_ref[ch, :]
            dsg = ds_ref[ch, :]
            sgb, dsgb = sg.astype(BF16), dsg.astype(BF16)
            yoff = _dot(cm, sgb, "nt") * ecsx
            dye = (dyv * ecsx).astype(BF16)
            xdd = xd * dtex
            dxbc_ref[:, csl] = _dot(dcbt, bm, "tn") + _dot(dye, sgb)
            dxbc_ref[:, bsl] = _dot(dcbt, cm) + _dot(xdd.astype(BF16), dsgb)
            bds = _dot(bm, dsgb, "nt")
            dxd = _dot(mtcat.astype(BF16), dybd) + dtex * bds
            wx = xdd * bds
            ds_ref[ch, :] = t["rows"][ch, :] * dsg + _dot(dye, cm, "tn")
            f1_ref[:, ch] = gq - gqt + dyv * yoff - wx
            f2_ref[:, ch] = wx
            f3_ref[:, ch] = dxd * xs
            f4_ref[:, ch] = dyv * xs
            dxbc_ref[:, ch] = dxd * dtx + dx_ref[:, ch] * dyv
            fs_ref[:, ch] = _dot3_r(ones8, dsg * sg, "nt")
        fold = lambda v: _dot3_l(v, ev, "nt")
        f2 = fold(f2_ref[...])
        last_row = jnp.sum(f2, axis=0, keepdims=True) + t["ecs"][q - 1:q, :] * fold(fs_ref[...])[0:1, :]
        dcs = fold(f1_ref[...]) + jnp.where(k["last"], last_row, 0.0)
        da = _dot3_r(k["trit"], dcs)
        ddt = da * t["a_neg"] + fold(f3_ref[...])
        ddt_raw = ddt * _sigmoid(dt_ref[...] + bias_ref[...])
        ddt_ref[...] = ddt_raw
        dbias_ref[...] += jnp.sum(ddt_raw, axis=0, keepdims=True)
        dalog_ref[...] += jnp.sum(da * t["dt"], axis=0, keepdims=True) * t["a_neg"]
        dd_ref[...] += jnp.sum(fold(f4_ref[...]), axis=0, keepdims=True)

    rev = lambda c: (nc - 1 - c, 0)
    hv = pl.BlockSpec((1, HEAD_LANES), lambda c: (0, 0))
    return pl.pallas_call(
        body, name=name, grid=(nc,),
        in_specs=[pl.BlockSpec((q, SSM_INNER), rev), pl.BlockSpec((q, SSM_CONV_DIM), rev),
                  pl.BlockSpec((q, HEAD_LANES), rev),
                  pl.BlockSpec((None, SSM_INNER, ns), lambda c: (nc - 1 - c, 0, 0)),
                  hv, hv, pl.BlockSpec((1, SSM_INNER), lambda c: (0, 0)),
                  pl.BlockSpec((HEAD_LANES, SSM_INNER), lambda c: (0, 0)),
                  pl.BlockSpec((SSM_INNER, HEAD_LANES), lambda c: (0, 0))],
        out_specs=[pl.BlockSpec((q, SSM_CONV_DIM), rev), pl.BlockSpec((q, HEAD_LANES), rev), hv, hv, hv],
        out_shape=[jax.ShapeDtypeStruct((L, SSM_CONV_DIM), F32), jax.ShapeDtypeStruct((L, HEAD_LANES), F32)]
        + [jax.ShapeDtypeStruct((1, HEAD_LANES), F32)] * 3,
        scratch_shapes=[pltpu.VMEM((SSM_INNER, ns), F32)] + [pltpu.VMEM((q, SSM_INNER), F32)] * 4
        + [pltpu.VMEM((8, SSM_INNER), F32)],
        compiler_params=_params(("arbitrary",)),
    )(dy, xbc, dt_raw, states, dt_bias, a_log, d_x, e, et)


def _my_index():
    return 4 * lax.axis_index("x") + 2 * lax.axis_index("y") + lax.axis_index("c")


def _peer(k):
    return (lax.axis_index("x") ^ ((k >> 2) & 1), lax.axis_index("y") ^ ((k >> 1) & 1), lax.axis_index("c") ^ (k & 1))


def _all_gather(shard, *, by_rows, name):
    nl, a, b = shard.shape
    out_shape = (nl, N_DEV * a, b) if by_rows else (nl, N_DEV, a, b)

    def body(src_ref, out_ref, send_sems, recv_sems, local_sems):
        me = _my_index()

        def mine(j):
            if by_rows:
                return out_ref.at[j, pl.ds(pl.multiple_of(me * a, 16), a), :]
            return out_ref.at[j, me]

        local = [pltpu.make_async_copy(src_ref.at[j], mine(j), local_sems.at[j]) for j in range(nl)]
        for cp in local:
            cp.start()
        copies = []
        for j in range(nl):
            for k in range(1, N_DEV):
                cp = pltpu.make_async_remote_copy(
                    src_ref=src_ref.at[j], dst_ref=mine(j), send_sem=send_sems.at[j, k - 1],
                    recv_sem=recv_sems.at[j, k - 1], device_id=_peer(k), device_id_type=MESH_ID)
                cp.start()
                copies.append(cp)
        for cp in copies:
            cp.wait()
        for cp in local:
            cp.wait()

    return pl.pallas_call(
        body, name=name, in_specs=[_HBM], out_specs=_HBM,
        out_shape=jax.ShapeDtypeStruct(out_shape, shard.dtype),
        scratch_shapes=[pltpu.SemaphoreType.DMA((nl, N_DEV - 1)), pltpu.SemaphoreType.DMA((nl, N_DEV - 1)),
                        pltpu.SemaphoreType.DMA((nl,))],
    )(shard)


_EFFECT = pltpu.SideEffectType.DATAFLOW_SIDE_EFFECTING


def _blk(ref, i, rows):
    if len(ref.shape) == 3:
        return ref.at[i]
    return ref.at[pl.ds(pl.multiple_of(i * rows, 16), rows), :]


def _remote_copies(scatter, src_ref, land_ref, send_sems, recv_sems, idx):
    me = _my_index()
    out = []
    for k in range(1, N_DEV):
        if scatter:
            src, dst = _blk(src_ref, me ^ k, land_ref.shape[1]), land_ref.at[me]
        else:
            src, dst = src_ref, _blk(land_ref, me, src_ref.shape[0])
        sem = idx * (N_DEV - 1) + k - 1
        out.append(pltpu.make_async_remote_copy(
            src_ref=src, dst_ref=dst, send_sem=send_sems.at[sem], recv_sem=recv_sems.at[sem],
            device_id=_peer(k), device_id_type=MESH_ID))
    return out


def _own_copy(scatter, src_ref, land_ref, own_sems, idx):
    me = _my_index()
    if scatter:
        src, dst = _blk(src_ref, me, land_ref.shape[1]), land_ref.at[me]
    else:
        src, dst = src_ref, _blk(land_ref, me, src_ref.shape[0])
    return pltpu.make_async_copy(src, dst, own_sems.at[idx])


def _copies_start(srcs, land_shapes, *, scatter, after, name):
    n = len(srcs)
    n_after = len(after)

    def body(*refs):
        src_refs, land_refs = refs[:n], refs[n:2 * n]
        send_sems, recv_sems, own_sems = refs[2 * n + n_after:2 * n + n_after + 3]
        for i in range(n):
            for cp in _remote_copies(scatter, src_refs[i], land_refs[i], send_sems, recv_sems, i):
                cp.start()
            _own_copy(scatter, src_refs[i], land_refs[i], own_sems, i).start()
        refs[-1][...] = jnp.zeros_like(refs[-1])

    lands = [lax.empty(shp, t.dtype) for shp, t in zip(land_shapes, srcs)]
    arrays = [pltpu.with_memory_space_constraint(t, pltpu.HBM) for t in list(srcs) + lands]
    sems = pltpu.SemaphoreType.DMA((n * (N_DEV - 1),))
    res = pl.pallas_call(
        body, name=name,
        out_shape=(sems, sems, pltpu.SemaphoreType.DMA((n,)), *[pltpu.HBM(t.shape, t.dtype) for t in arrays],
                   jax.ShapeDtypeStruct((8, 128), F32)),
        in_specs=[_HBM] * (2 * n) + [_ANY] * n_after,
        out_specs=(_SEM, _SEM, _SEM, *[_HBM] * (2 * n), pl.BlockSpec(memory_space=pltpu.VMEM)),
        input_output_aliases={i: 3 + i for i in range(2 * n)},
        compiler_params=pltpu.CompilerParams(has_side_effects=_EFFECT),
    )(*arrays, *after)
    return (res[:3], res[3:3 + n], res[3 + n:3 + 2 * n]), res[-1]


def _copies_wait(started, *, scatter, after, name):
    sems, srcs, lands = started
    n = len(srcs)

    def body(*refs):
        src_refs, land_refs = refs[:n], refs[n:2 * n]
        s_sems, r_sems, o_sems = refs[2 * n:2 * n + 3]
        for i in range(n):
            for cp in _remote_copies(scatter, src_refs[i], land_refs[i], s_sems, r_sems, i):
                cp.wait_send()
                cp.wait_recv()
            _own_copy(scatter, src_refs[i], land_refs[i], o_sems, i).wait()

    arrays = list(srcs) + list(lands)
    res = pl.pallas_call(
        body, name=name,
        out_shape=tuple(pltpu.HBM(t.shape, t.dtype) for t in arrays),
        in_specs=[_HBM] * (2 * n) + [_SEM] * 3 + [_ANY] * len(after), out_specs=tuple([_HBM] * (2 * n)),
        input_output_aliases={i: i for i in range(2 * n)},
        compiler_params=pltpu.CompilerParams(has_side_effects=_EFFECT),
    )(*arrays, *sems, *after)
    return res[n:]


def _adamw_math(w, g, m, v):
    m = ADAM_B1 * m + (1.0 - ADAM_B1) * g
    v = ADAM_B2 * v + (1.0 - ADAM_B2) * (g * g)
    m_hat = m / (1.0 - ADAM_B1 ** ADAM_STEP)
    v_hat = v / (1.0 - ADAM_B2 ** ADAM_STEP)
    delta = -ADAM_LR * (m_hat / (jnp.sqrt(v_hat) + ADAM_EPS) + ADAM_WD * w)
    return delta, m, v


def _adamw(recvs, w, m, v, *, name):
    nl, R, C = w.shape
    tr = max(t for t in range(8, (128 if C > 1024 else 512) + 1, 8) if R % t == 0)

    def body(*refs):
        r_refs = refs[:nl]
        w_ref, m_ref, v_ref, g_ref, d_ref, nm_ref, nv_ref = refs[nl:]
        for layer in range(nl):
            @pl.when(pl.program_id(0) == layer)
            def _(r_ref=r_refs[layer]):
                g = r_ref[0].astype(F32)
                for s in range(1, N_DEV):
                    g = g + r_ref[s].astype(F32)
                delta, nm, nv = _adamw_math(w_ref[...], g, m_ref[...], v_ref[...])
                g_ref[...] = g
                d_ref[...] = delta
                nm_ref[...] = nm
                nv_ref[...] = nv

    def recv_spec(layer):
        return pl.BlockSpec((N_DEV, tr, C), lambda j, i: (0, jnp.where(j == layer, i, 0), 0))

    row = pl.BlockSpec((None, tr, C), lambda j, i: (j, i, 0))
    return pl.pallas_call(
        body, name=name, grid=(nl, R // tr),
        in_specs=[recv_spec(layer) for layer in range(nl)] + [row, row, row],
        out_specs=[row] * 4, out_shape=[jax.ShapeDtypeStruct((nl, R, C), F32)] * 4,
        compiler_params=_params(("parallel", "parallel")),
    )(*recvs, w, m, v)


def _sum_slots(g8, *, name):
    _, P, C = g8.shape

    def body(r_ref, o_ref):
        g = r_ref[0]
        for s in range(1, N_DEV):
            g = g + r_ref[s]
        o_ref[...] = g

    return pl.pallas_call(
        body, name=name, grid=(1,),
        in_specs=[pl.BlockSpec((N_DEV, P, C), lambda i: (0, 0, 0))],
        out_specs=pl.BlockSpec((P, C), lambda i: (0, 0)),
        out_shape=jax.ShapeDtypeStruct((P, C), F32),
        compiler_params=_params(("arbitrary",)),
    )(g8)


def _adamw_small(g, w, m, v, *, name):
    P, C = w.shape

    def body(g_ref, w_ref, m_ref, v_ref, d_ref, nm_ref, nv_ref):
        delta, nm, nv = _adamw_math(w_ref[...], g_ref[...], m_ref[...], v_ref[...])
        d_ref[...] = delta
        nm_ref[...] = nm
        nv_ref[...] = nv

    full = pl.BlockSpec((P, C), lambda i: (0, 0))
    return pl.pallas_call(
        body, name=name, grid=(1,), in_specs=[full] * 4, out_specs=[full] * 3,
        out_shape=[jax.ShapeDtypeStruct((P, C), F32)] * 3,
        compiler_params=_params(("arbitrary",)),
    )(g, w, m, v)


def _pack(arrays):
    flat = jnp.concatenate([a.reshape(-1) for a in arrays])
    pad = (-flat.shape[0]) % (8 * 128)
    return jnp.pad(flat, (0, pad)).reshape(-1, 128)


def _unpack(buf, shapes):
    flat = buf.reshape(-1)
    out, off = [], 0
    for shp in shapes:
        n = 1
        for s in shp:
            n *= s
        out.append(flat[off:off + n].reshape(shp))
        off += n
    return out


def _expand_matrices():
    h = lax.broadcasted_iota(jnp.int32, (HEAD_LANES, SSM_INNER), 0)
    col = lax.broadcasted_iota(jnp.int32, (HEAD_LANES, SSM_INNER), 1) // SSM_HEAD_DIM
    e = (h == col).astype(BF16)
    return e, e.T


def _pad_heads(v):
    return jnp.pad(v.reshape(1, -1), ((0, 0), (0, HEAD_LANES - v.shape[-1])))


SHARDED_SMALL = ("cv_w_dw", "ssm_w_conv", "ssm_b_conv", "ssm_norm_g", "ffn_w_dw")
REPLICATED_SMALL = ("norm_mix_g", "norm_ffn_g", "norm_final_g", "cv_b_in", "cv_b_dw", "cv_ln_g", "cv_ln_b",
                    "cv_b_out", "ssm_dt_bias", "ssm_a_log", "ssm_d", "ffn_b_dw")
SMALL = REPLICATED_SMALL + SHARDED_SMALL
BIG_COLS = ("cv_w_in", "ssm_w_in", "ffn_w_up")
BIG_ROWS = ("cv_w_out", "ssm_w_out", "ffn_w_down")
WEIGHTS = ("norm_mix_g", "norm_ffn_g", "norm_final_g", "cv_w_in", "cv_b_in", "cv_w_dw", "cv_b_dw", "cv_ln_g",
           "cv_ln_b", "cv_w_out", "cv_b_out", "ssm_w_in", "ssm_w_conv", "ssm_b_conv", "ssm_dt_bias", "ssm_a_log",
           "ssm_d", "ssm_norm_g", "ssm_w_out", "ffn_w_up", "ffn_w_dw", "ffn_b_dw", "ffn_w_down")


N_STAGES = 2 * DEPTH


def _stage_weights(si):
    i = si // 2
    if si % 2:
        return "ffn_w_up", "ffn_w_down", i
    return ("cv_w_in", "cv_w_out", i // 2) if i % 2 == 0 else ("ssm_w_in", "ssm_w_out", i // 2)


def _gather_small_weights(w):
    small_local = _pack([w[n] for n in SHARDED_SMALL])
    small8 = _all_gather(small_local[None], by_rows=False, name="gather_small")[0]
    full = {}
    per_dev = [_unpack(small8[d], [w[n].shape for n in SHARDED_SMALL]) for d in range(N_DEV)]
    for idx, n in enumerate(SHARDED_SMALL):
        full[n] = jnp.concatenate([per_dev[d][idx] for d in range(N_DEV)], axis=-1)
    for n in REPLICATED_SMALL:
        full[n] = w[n]
    return full


def _forward_backward(x, target, weights_of, full, emit):
    e, et = _expand_matrices()
    grads = {}

    saved = []
    for i in range(DEPTH):
        j = i // 2
        s = {"x_in": x}
        wc, wr, behind = weights_of(2 * i, x)
        h1 = _rms_fwd(x, full["norm_mix_g"][i][None], name=f"rms_mix_{i}")
        s["h1"] = h1
        if i % 2 == 0:
            u = _mm_cols_nn(h1, wc, bias=full["cv_b_in"][j][None], after=behind, name=f"cv_in_{i}")
            sact, c = _cv_mid_fwd(u, full["cv_w_dw"][j], full["cv_b_dw"][j][None], full["cv_ln_g"][j][None],
                                  full["cv_ln_b"][j][None], name=f"cv_mid_{i}")
            x = _mm_rows_nn(sact, wr, res=x, bias=full["cv_b_out"][j][None], name=f"cv_out_{i}")
            s.update(u=u, c=c, sact=sact)
        else:
            zx = _mm_cols_nn(h1, wc, after=behind, name=f"ssm_in_{i}")
            z, xpre, dt_raw = _ssm_unpad(zx, name=f"ssm_unpad_{i}")
            xbc = _ssm_conv_fwd(xpre, full["ssm_w_conv"][j], full["ssm_b_conv"][j][None], name=f"ssm_conv_{i}")
            hp = dict(dt_bias=_pad_heads(full["ssm_dt_bias"][j]), a_log=_pad_heads(full["ssm_a_log"][j]),
                      d_x=jnp.repeat(full["ssm_d"][j], SSM_HEAD_DIM)[None])
            y, states = _ssd_fwd(xbc, dt_raw, hp["dt_bias"], hp["a_log"], hp["d_x"], e, et, name=f"ssd_fwd_{i}")
            yn = _gnorm_fwd(y, z, full["ssm_norm_g"][j][None], name=f"gnorm_{i}")
            x = _mm_rows_nn(yn, wr, res=x, name=f"ssm_out_{i}")
            s.update(z=z, xpre=xpre, dt_raw=dt_raw, xbc=xbc, y=y, states=states, yn=yn, hp=hp)
        s["x_mid"] = x
        wc2, wr2, behind = weights_of(2 * i + 1, x)
        h2 = _rms_fwd(x, full["norm_ffn_g"][i][None], name=f"rms_ffn_{i}")
        u0 = _mm_cols_nn(h2, wc2, after=behind, halves=True, name=f"ffn_up_{i}")
        ffn_w = full["ffn_w_dw"][i].reshape(FFN_K, 2, FFN_HIDDEN).transpose(1, 0, 2)
        ffn_b = full["ffn_b_dw"][i].reshape(2, 1, FFN_HIDDEN)
        gact = _ffn_mid_fwd(u0, ffn_w, ffn_b, name=f"ffn_mid_{i}")
        x = _mm_rows_nn(gact, wr2, res=x, name=f"ffn_down_{i}")
        s.update(h2=h2, u0=u0, gact=gact, ffn_w=ffn_w, ffn_b=ffn_b, weights=(wc, wr, wc2, wr2))
        saved.append(s)

    dx, dg_final, loss = _final_loss(x, full["norm_final_g"][None], target, name="final_loss")
    grads["norm_final_g"] = dg_final[0]

    small_layers = {n: [None] * full[n].shape[0] for n in SMALL if n != "norm_final_g"}
    behind = None
    for i in reversed(range(DEPTH)):
        j = i // 2
        s = saved[i]
        wc, wr, wc2, wr2 = s["weights"]
        dgact = _mm_rows_nt(dx, wr2, after=behind, name=f"ffn_down_dx_{i}")
        dwr2 = _mm_rows_tn(s["gact"], dx, name=f"ffn_down_dw_{i}")
        du0, dw_dw, db_dw = _ffn_bwd(dgact, s["u0"], s["ffn_w"], s["ffn_b"], name=f"ffn_bwd_{i}")
        small_layers["ffn_w_dw"][i] = dw_dw.transpose(1, 0, 2).reshape(FFN_K, 2 * FFN_HIDDEN)
        small_layers["ffn_b_dw"][i] = db_dw.reshape(2 * FFN_HIDDEN)
        dh2 = _mm_cols_nt(du0, wc2, name=f"ffn_up_dx_{i}")
        dwc2 = _mm_cols_tn(s["h2"], du0, name=f"ffn_up_dw_{i}")
        dx, dg, colsum = _rms_bwd(s["x_mid"], full["norm_ffn_g"][i][None], dh2, dx, name=f"rms_ffn_bwd_{i}")
        small_layers["norm_ffn_g"][i] = dg[0]
        behind = emit(2 * i + 1, dwc2, dwr2)
        if i % 2 == 0:
            small_layers["cv_b_out"][j] = colsum[0]
            dsact = _mm_rows_nt(dx, wr, after=behind, name=f"cv_out_dx_{i}")
            dwr = _mm_rows_tn(s["sact"], dx, name=f"cv_out_dw_{i}")
            du, dw_dw, db_dw, dlg, dlb, db_in = _cv_bwd(dsact, s["c"], s["u"], full["cv_w_dw"][j], full["cv_ln_g"][j][None],
                                                        full["cv_ln_b"][j][None], name=f"cv_bwd_{i}")
            small_layers["cv_w_dw"][j] = dw_dw
            small_layers["cv_b_dw"][j] = db_dw[0]
            small_layers["cv_ln_g"][j] = dlg[0]
            small_layers["cv_ln_b"][j] = dlb[0]
            small_layers["cv_b_in"][j] = db_in[0]
            dh1 = _mm_cols_nt(du, wc, name=f"cv_in_dx_{i}")
            dwc = _mm_cols_tn(s["h1"], du, name=f"cv_in_dw_{i}")
        else:
            hp = s["hp"]
            dyn = _mm_rows_nt(dx, wr, after=behind, name=f"ssm_out_dx_{i}")
            dwr = _mm_rows_tn(s["yn"], dx, name=f"ssm_out_dw_{i}")
            dy, dz, dng = _gnorm_bwd(dyn, s["y"], s["z"], full["ssm_norm_g"][j][None], name=f"gnorm_bwd_{i}")
            small_layers["ssm_norm_g"][j] = dng[0]
            dxbc, ddt_raw, dbias, dalog, dd = _ssd_bwd(dy, s["xbc"], s["dt_raw"], s["states"], hp["dt_bias"], hp["a_log"],
                                                      hp["d_x"], e, et, name=f"ssd_bwd_{i}")
            small_layers["ssm_dt_bias"][j] = dbias[0, :SSM_HEADS]
            small_layers["ssm_a_log"][j] = dalog[0, :SSM_HEADS]
            small_layers["ssm_d"][j] = dd[0, :SSM_HEADS]
            dxpre, dw_conv, db_conv = _ssm_conv_bwd(dxbc, s["xpre"], full["ssm_w_conv"][j], full["ssm_b_conv"][j][None],
                                                    name=f"ssm_conv_bwd_{i}")
            small_layers["ssm_w_conv"][j] = dw_conv
            small_layers["ssm_b_conv"][j] = db_conv[0]
            dzx = _ssm_pad(dz, dxpre, ddt_raw, name=f"ssm_pad_{i}")
            dh1 = _mm_cols_nt(dzx, wc, name=f"ssm_in_dx_{i}")
            dwc = _mm_cols_tn(s["h1"], dzx, name=f"ssm_in_dw_{i}")
        dx, dg, _ = _rms_bwd(s["x_in"], full["norm_mix_g"][i][None], dh1, dx, name=f"rms_mix_bwd_{i}")
        small_layers["norm_mix_g"][i] = dg[0]
        behind = emit(2 * i, dwc, dwr)
    for n, layers in small_layers.items():
        grads[n] = jnp.stack(layers)
    return loss, dx, grads


def _update_big(names, recvs, w, m, v):
    out = {}
    for n in names:
        b = recvs[n][0].shape[-1]
        pad = ((0, 0), (0, 0), (0, b - w[n].shape[-1]))
        res = _adamw(recvs[n], *[jnp.pad(t, pad) for t in (w[n], m[n], v[n])], name="adamw_" + n)
        out[n] = [r[..., :w[n].shape[-1]] for r in res]
    return out


def _update_small(grads, w, m, v):
    me = _my_index()
    out = {}
    small_part = _pack([grads[n] for n in SMALL])
    small_all = _all_gather(small_part[None], by_rows=False, name="gather_small_grads")[0]
    small_sum = _sum_slots(small_all, name="sum_small_grads")
    gfull = dict(zip(SMALL, _unpack(small_sum, [grads[n].shape for n in SMALL])))
    glocal = []
    for n in SMALL:
        g = gfull[n]
        if n in SHARDED_SMALL:
            cols = w[n].shape[-1]
            g = lax.dynamic_slice_in_dim(g, me * cols, cols, axis=g.ndim - 1)
        glocal.append(g)
    packed = [_pack(glocal)] + [_pack([t[n] for n in SMALL]) for t in (w, m, v)]
    res = _adamw_small(*packed, name="adamw_small")
    shapes = [w[n].shape for n in SMALL]
    unpacked = [_unpack(r, shapes) for r in res]
    for idx, n in enumerate(SMALL):
        out[n] = [glocal[idx]] + [u[idx] for u in unpacked]
    return out


GATHER_AHEAD = 3


def _train_step(x, target, w, m, v):
    full = _gather_small_weights(w)
    gathers, exchanges = {}, {}
    last_start = [jnp.zeros((8, 128), F32)]

    def start_gather(si, after):
        cn, rn, j = _stage_weights(si)
        wc, wr = w[cn][j].astype(BF16), w[rn][j].astype(BF16)
        if cn == "ssm_w_in":
            wc = jnp.pad(wc, ((0, 0), (0, SSM_SLAB_PAD - SSM_SLAB)))
        shapes = [(N_DEV,) + wc.shape, (N_DEV * wr.shape[0], wr.shape[1])]
        gathers[si], last_start[0] = _copies_start([wc, wr], shapes, scatter=False, after=after,
                                                   name=f"gather_start_{si}")

    for si in range(GATHER_AHEAD):
        start_gather(si, [last_start[0]])

    def weights_of(si, x_act):
        lc, lr = _copies_wait(gathers.pop(si), scatter=False, after=[x_act, last_start[0]], name=f"gather_wait_{si}")
        behind = None
        if si + GATHER_AHEAD < N_STAGES:
            start_gather(si + GATHER_AHEAD, [last_start[0], lc])
            behind = last_start[0]
        return lc, lr, behind

    def emit(si, dwc, dwr):
        shapes = [dwc.shape, (N_DEV, dwr.shape[0] // N_DEV, dwr.shape[1])]
        exchanges[si], token = _copies_start([dwc, dwr], shapes, scatter=True, after=[], name=f"exchange_start_{si}")
        return token

    loss, dx, grads = _forward_backward(x, target, weights_of, full, emit)

    out = _update_small(grads, w, m, v)
    recvs = {n: [None] * w[n].shape[0] for n in BIG_COLS + BIG_ROWS}
    behind = [dx, out[SMALL[0]][1]]
    for si in reversed(range(N_STAGES)):
        cn, rn, j = _stage_weights(si)
        if si == 0:
            out.update(_update_big([n for n in BIG_COLS + BIG_ROWS if not n.startswith("cv_")], recvs, w, m, v))
            behind = [out["ffn_w_up"][1], out["ffn_w_down"][1], out["ssm_w_in"][1], out["ssm_w_out"][1]]
        recvs[cn][j], recvs[rn][j] = _copies_wait(exchanges.pop(si), scatter=True, after=behind,
                                                  name=f"exchange_wait_{si}")
        behind = [recvs[cn][j]]
    out.update(_update_big(["cv_w_in", "cv_w_out"], recvs, w, m, v))
    return lax.psum(loss[0, 0], AXES), dx, out


def kernel(x, norm_mix_g, norm_ffn_g, norm_final_g, cv_w_in, cv_b_in, cv_w_dw, cv_b_dw, cv_ln_g, cv_ln_b, cv_w_out, cv_b_out, ssm_w_in, ssm_w_conv, ssm_b_conv, ssm_dt_bias, ssm_a_log, ssm_d, ssm_norm_g, ssm_w_out, ffn_w_up, ffn_w_dw, ffn_b_dw, ffn_w_down, loss_target, m_norm_mix_g, m_norm_ffn_g, m_norm_final_g, m_cv_w_in, m_cv_b_in, m_cv_w_dw, m_cv_b_dw, m_cv_ln_g, m_cv_ln_b, m_cv_w_out, m_cv_b_out, m_ssm_w_in, m_ssm_w_conv, m_ssm_b_conv, m_ssm_dt_bias, m_ssm_a_log, m_ssm_d, m_ssm_norm_g, m_ssm_w_out, m_ffn_w_up, m_ffn_w_dw, m_ffn_b_dw, m_ffn_w_down, v_norm_mix_g, v_norm_ffn_g, v_norm_final_g, v_cv_w_in, v_cv_b_in, v_cv_w_dw, v_cv_b_dw, v_cv_ln_g, v_cv_ln_b, v_cv_w_out, v_cv_b_out, v_ssm_w_in, v_ssm_w_conv, v_ssm_b_conv, v_ssm_dt_bias, v_ssm_a_log, v_ssm_d, v_ssm_norm_g, v_ssm_w_out, v_ffn_w_up, v_ffn_w_dw, v_ffn_b_dw, v_ffn_w_down):
    args = locals()
    w = {n: args[n] for n in WEIGHTS}
    m = {n: args["m_" + n] for n in WEIGHTS}
    v = {n: args["v_" + n] for n in WEIGHTS}
    loss, dx, out = _train_step(x[0], loss_target[0], w, m, v)
    return (loss, dx[None], *[out[n][0] for n in WEIGHTS], *[out[n][1] for n in WEIGHTS],
            *[out[n][2] for n in WEIGHTS], *[out[n][3] for n in WEIGHTS])
```

```python
import functools

import jax
import jax.numpy as jnp
from jax import lax
from jax.experimental import pallas as pl
from jax.experimental.pallas import tpu as pltpu

F32, BF16 = jnp.float32, jnp.bfloat16
AXES = ("x", "y", "c")
N_DEV = 8
MESH_ID = pl.DeviceIdType.MESH

D_MODEL = 2048
DEPTH = 4
CHUNK = 64
CONV_K = 31
SSM_INNER = 4096
SSM_HEADS = 64
SSM_HEAD_DIM = 64
SSM_GROUPS = 8
SSM_GROUP_W = SSM_INNER // SSM_GROUPS
SSM_STATE = 128
SSM_CONV_K = 4
SSM_CONV_DIM = SSM_INNER + 2 * SSM_GROUPS * SSM_STATE
SSM_IN_DIM = SSM_INNER + SSM_CONV_DIM + SSM_HEADS
SSM_SLAB = SSM_IN_DIM // N_DEV
SSM_SLAB_PAD = 1408
SSM_COMPACT = 1280 * (N_DEV - 1) + SSM_SLAB_PAD
FFN_HIDDEN = 5632
FFN_K = 3
HEAD_LANES = 128
RMS_EPS = 1e-6
LN_EPS = 1e-5
ADAM_LR, ADAM_B1, ADAM_B2, ADAM_EPS, ADAM_WD, ADAM_STEP = 0.001, 0.9, 0.999, 1e-08, 0.01, 10

_DIMS = {
    "nn": (((1,), (0,)), ((), ())),
    "nt": (((1,), (1,)), ((), ())),
    "tn": (((0,), (0,)), ((), ())),
}


def _params(sem=None, vmem_mb=48):
    return pltpu.CompilerParams(dimension_semantics=sem, vmem_limit_bytes=vmem_mb << 20)


_HBM = pl.BlockSpec(memory_space=pltpu.HBM)
_ANY = pl.BlockSpec(memory_space=pl.ANY)
_SEM = pl.BlockSpec(memory_space=pltpu.SEMAPHORE)


def _dot(a, b, mode="nn"):
    return lax.dot_general(a, b, _DIMS[mode], preferred_element_type=F32)


def _split3(x):
    hi = x.astype(BF16)
    r1 = x - hi.astype(F32)
    mid = r1.astype(BF16)
    lo = (r1 - mid.astype(F32)).astype(BF16)
    return hi, mid, lo


def _dot3_l(x, m, mode="nn", parts=3):
    return sum(_dot(p, m, mode) for p in _split3(x)[:parts])


def _dot3_r(m, x, mode="nn", parts=3):
    return sum(_dot(m, p, mode) for p in _split3(x)[:parts])


def _sigmoid(x):
    return jax.nn.sigmoid(x)


def _dsilu(x, sg):
    return sg * (1.0 + x * (1.0 - sg))


def _softplus(x):
    return jnp.maximum(x, 0.0) + jnp.log(1.0 + jnp.exp(-jnp.abs(x)))


def _tile(n, pref):
    return min(n, pref)


def _matmul(a, b, *, mode, grid, a_spec, b_spec, o_spec, o_block, out_shape, nk, name,
            extras=(), epilogue=None, after=None, vmem_mb=48):
    n_extra = len(extras)
    if after is not None:
        extras = tuple(extras) + ((after, _ANY),)
    n_in = len(extras)

    def body(a_ref, b_ref, *rest):
        extra_refs = rest[:n_extra]
        o_ref = rest[n_in]
        part = _dot(a_ref[...].astype(BF16), b_ref[...].astype(BF16), mode)

        def finish(acc):
            if epilogue is not None:
                acc = epilogue(acc, *[r[...] for r in extra_refs])
            o_ref[...] = acc.astype(o_ref.dtype)

        if nk == 1:
            finish(part)
        else:
            acc_ref = rest[n_in + 1]
            k = pl.program_id(len(grid) - 1)

            @pl.when(k == 0)
            def _():
                acc_ref[...] = part

            @pl.when(k > 0)
            def _():
                acc_ref[...] += part

            @pl.when(k == nk - 1)
            def _():
                finish(acc_ref[...])

    scratch = [] if nk == 1 else [pltpu.VMEM(o_block, F32)]
    sem = ("parallel",) * (len(grid) - 1) + ("arbitrary",)
    return pl.pallas_call(
        body, name=name, grid=grid,
        in_specs=[a_spec, b_spec] + [s for _, s in extras],
        out_specs=o_spec, out_shape=out_shape, scratch_shapes=scratch,
        compiler_params=_params(sem, vmem_mb),
    )(a, b, *[x for x, _ in extras])


HALF_SLABS = N_DEV // 2


def _mm_cols_nn(h, wg, *, name, bias=None, after=None, halves=False):
    L, K = h.shape
    n = wg.shape[-1]
    tm = _tile(L, 512)
    extras, epi = (), None
    if bias is not None:
        extras = ((bias, pl.BlockSpec((1, n), lambda s, i: (0, s))),)
        epi = lambda acc, b: acc + b
    if halves:
        o_spec = pl.BlockSpec((None, tm, n), lambda s, i: (s // HALF_SLABS, i, s % HALF_SLABS))
        out_shape = jax.ShapeDtypeStruct((2, L, HALF_SLABS * n), F32)
    else:
        o_spec = pl.BlockSpec((tm, n), lambda s, i: (i, s))
        out_shape = jax.ShapeDtypeStruct((L, N_DEV * n), F32)
    return _matmul(
        h, wg, mode="nn", grid=(N_DEV, L // tm),
        a_spec=pl.BlockSpec((tm, K), lambda s, i: (i, 0)),
        b_spec=pl.BlockSpec((None, K, n), lambda s, i: (s, 0, 0)),
        o_spec=o_spec, o_block=(tm, n), out_shape=out_shape, nk=1, name=name,
        extras=extras, epilogue=epi, after=after)


def _mm_cols_nt(du, wg, *, name):
    L = du.shape[-2]
    K, n = wg.shape[-2:]
    tm = _tile(L, 512)
    if du.ndim == 3:
        a_spec = pl.BlockSpec((None, tm, n), lambda i, s: (s // HALF_SLABS, i, s % HALF_SLABS))
    else:
        a_spec = pl.BlockSpec((tm, n), lambda i, s: (i, s))
    return _matmul(
        du, wg, mode="nt", grid=(L // tm, N_DEV), a_spec=a_spec,
        b_spec=pl.BlockSpec((None, K, n), lambda i, s: (s, 0, 0)),
        o_spec=pl.BlockSpec((tm, K), lambda i, s: (i, 0)), o_block=(tm, K),
        out_shape=jax.ShapeDtypeStruct((L, K), F32), nk=N_DEV, name=name)


def _mm_cols_tn(h, du, *, name):
    L, K = h.shape
    tm = _tile(L, 512)
    if du.ndim == 3:
        n = du.shape[2] // HALF_SLABS
        b_spec = pl.BlockSpec((None, L, n), lambda s, q: (s // HALF_SLABS, 0, s % HALF_SLABS))
    else:
        n = du.shape[1] // N_DEV
        b_spec = pl.BlockSpec((L, n), lambda s, q: (0, s))
    tq = 256 if n > 1024 else 512
    return _matmul(
        h, du, mode="tn", grid=(N_DEV, K // tq),
        a_spec=pl.BlockSpec((L, tq), lambda s, q: (0, q)), b_spec=b_spec,
        o_spec=pl.BlockSpec((None, tq, n), lambda s, q: (s, q, 0)), o_block=(tq, n),
        out_shape=jax.ShapeDtypeStruct((N_DEV, K, n), BF16), nk=1, name=name)


def _mm_rows_nn(a, wg, *, res, name, bias=None):
    L, Kw = a.shape
    N = wg.shape[-1]
    tm, tn = _tile(L, 512), 512
    extras = [(res, pl.BlockSpec((tm, tn), lambda j, i: (i, j)))]
    if bias is not None:
        extras.append((bias, pl.BlockSpec((1, tn), lambda j, i: (0, j))))
        epi = lambda acc, r, b: acc + r + b
    else:
        epi = lambda acc, r: acc + r
    return _matmul(
        a, wg, mode="nn", grid=(N // tn, L // tm),
        a_spec=pl.BlockSpec((tm, Kw), lambda j, i: (i, 0)),
        b_spec=pl.BlockSpec((Kw, tn), lambda j, i: (0, j)),
        o_spec=pl.BlockSpec((tm, tn), lambda j, i: (i, j)), o_block=(tm, tn),
        out_shape=jax.ShapeDtypeStruct((L, N), F32), nk=1, name=name,
        extras=tuple(extras), epilogue=epi)


def _mm_rows_nt(dy, wg, *, name, after=None):
    L, N = dy.shape
    Kw = wg.shape[-2]
    tm, tn = _tile(L, 512), 512
    return _matmul(
        dy, wg, mode="nt", grid=(L // tm, Kw // tn),
        a_spec=pl.BlockSpec((tm, N), lambda i, q: (i, 0)),
        b_spec=pl.BlockSpec((tn, N), lambda i, q: (q, 0)),
        o_spec=pl.BlockSpec((tm, tn), lambda i, q: (i, q)), o_block=(tm, tn),
        out_shape=jax.ShapeDtypeStruct((L, Kw), F32), nk=1, name=name, after=after)


def _mm_rows_tn(a, dy, *, name):
    L, Kw = a.shape
    N = dy.shape[1]
    tq, tn = 512, 1024
    return _matmul(
        a, dy, mode="tn", grid=(N // tn, Kw // tq),
        a_spec=pl.BlockSpec((L, tq), lambda p, q: (0, q)),
        b_spec=pl.BlockSpec((L, tn), lambda p, q: (0, p)),
        o_spec=pl.BlockSpec((tq, tn), lambda p, q: (q, p)), o_block=(tq, tn),
        out_shape=jax.ShapeDtypeStruct((Kw, N), BF16), nk=1, name=name)


def _rms_fwd(x, g, *, name):
    L, Dm = x.shape
    tm = _tile(L, 256)

    def body(x_ref, g_ref, h_ref):
        xv = x_ref[...]
        r = lax.rsqrt(jnp.mean(xv * xv, axis=-1, keepdims=True) + RMS_EPS)
        h_ref[...] = (xv * r * g_ref[...]).astype(BF16)

    return pl.pallas_call(
        body, name=name, grid=(L // tm,),
        in_specs=[pl.BlockSpec((tm, Dm), lambda i: (i, 0)), pl.BlockSpec((1, Dm), lambda i: (0, 0))],
        out_specs=pl.BlockSpec((tm, Dm), lambda i: (i, 0)),
        out_shape=jax.ShapeDtypeStruct((L, Dm), BF16),
        compiler_params=_params(("parallel",)),
    )(x, g)


def _rms_bwd(x, g, dh, dres, *, name):
    L, Dm = x.shape
    tm = _tile(L, 256)

    def body(x_ref, g_ref, dh_ref, dres_ref, dx_ref, dxb_ref, dg_ref, cs_ref):
        i = pl.program_id(0)
        xv = x_ref[...]
        r = lax.rsqrt(jnp.mean(xv * xv, axis=-1, keepdims=True) + RMS_EPS)
        xh = xv * r
        dh = dh_ref[...]
        dxh = dh * g_ref[...]
        dx = dres_ref[...] + r * (dxh - xh * jnp.mean(dxh * xh, axis=-1, keepdims=True))
        dx_ref[...] = dx
        dxb_ref[...] = dx.astype(BF16)

        @pl.when(i == 0)
        def _():
            dg_ref[...] = jnp.zeros_like(dg_ref)
            cs_ref[...] = jnp.zeros_like(cs_ref)

        dg_ref[...] += jnp.sum(dh * xh, axis=0, keepdims=True)
        cs_ref[...] += jnp.sum(dx, axis=0, keepdims=True)

    row = pl.BlockSpec((tm, Dm), lambda i: (i, 0))
    vec = pl.BlockSpec((1, Dm), lambda i: (0, 0))
    return pl.pallas_call(
        body, name=name, grid=(L // tm,),
        in_specs=[row, vec, row, row], out_specs=[row, row, vec, vec],
        out_shape=[jax.ShapeDtypeStruct((L, Dm), F32), jax.ShapeDtypeStruct((L, Dm), BF16),
                   jax.ShapeDtypeStruct((1, Dm), F32), jax.ShapeDtypeStruct((1, Dm), F32)],
        compiler_params=_params(("arbitrary",)),
    )(x, g, dh, dres)


def _final_loss(x, g, target, *, name):
    L, Dm = x.shape
    tm = _tile(L, 256)

    def body(x_ref, g_ref, t_ref, dx_ref, dxb_ref, dg_ref, loss_ref):
        i = pl.program_id(0)
        xv = x_ref[...]
        gv = g_ref[...]
        r = lax.rsqrt(jnp.mean(xv * xv, axis=-1, keepdims=True) + RMS_EPS)
        xh = xv * r
        err = xh * gv - t_ref[...]
        dy = err * (1.0 / Dm)
        dxh = dy * gv
        dx = r * (dxh - xh * jnp.mean(dxh * xh, axis=-1, keepdims=True))
        dx_ref[...] = dx
        dxb_ref[...] = dx.astype(BF16)

        @pl.when(i == 0)
        def _():
            dg_ref[...] = jnp.zeros_like(dg_ref)
            loss_ref[...] = jnp.zeros_like(loss_ref)

        dg_ref[...] += jnp.sum(dy * xh, axis=0, keepdims=True)
        loss_ref[...] += 0.5 * jnp.sum(jnp.mean(err * err, axis=-1, keepdims=True), axis=0, keepdims=True)

    row = pl.BlockSpec((tm, Dm), lambda i: (i, 0))
    vec = pl.BlockSpec((1, Dm), lambda i: (0, 0))
    return pl.pallas_call(
        body, name=name, grid=(L // tm,),
        in_specs=[row, vec, row],
        out_specs=[row, row, vec, pl.BlockSpec((1, 128), lambda i: (0, 0))],
        out_shape=[jax.ShapeDtypeStruct((L, Dm), F32), jax.ShapeDtypeStruct((L, Dm), BF16),
                   jax.ShapeDtypeStruct((1, Dm), F32), jax.ShapeDtypeStruct((1, 128), F32)],
        compiler_params=_params(("arbitrary",)),
    )(x, g, target)


def _conv_from_ext(ext_ref, w_ref, bias, taps, halo, rows):
    acc = jnp.broadcast_to(bias, (rows, ext_ref.shape[1]))
    for j in range(taps):
        acc = acc + w_ref[taps - 1 - j:taps - j, :] * ext_ref[halo - j:halo - j + rows, :]
    return acc


def _conv_back(dext_ref, x_cur, w_ref, dw_ref, taps, rows):
    dx = jnp.zeros((rows, dext_ref.shape[1]), F32)
    for j in range(taps):
        sh = dext_ref[j:j + rows, :]
        dx = dx + w_ref[taps - 1 - j:taps - j, :] * sh
        dw_ref[taps - 1 - j:taps - j, :] += jnp.sum(x_cur * sh, axis=0, keepdims=True)
    return dx


def _prev_blk(i, r):
    return jnp.maximum(i * r - 1, 0)


def _next_blk(i, r, nblk):
    return jnp.minimum((i + 1) * r, nblk - 1)


CV_HALO = 32


def _cv_mid_fwd(u, w_dw, b_dw, ln_g, ln_b, *, name):
    L = u.shape[0]
    Dm = D_MODEL
    tm = _tile(L, 256)
    r = tm // CV_HALO

    def body(a_ref, g_ref, ah_ref, gh_ref, w_ref, bdw_ref, lg_ref, lb_ref, s_ref, c_ref, ext_ref):
        i = pl.program_id(0)
        keep = (i > 0).astype(F32)
        ext_ref[0:CV_HALO, :] = ah_ref[...] * _sigmoid(gh_ref[...]) * keep
        ext_ref[CV_HALO:CV_HALO + tm, :] = a_ref[...] * _sigmoid(g_ref[...])
        c = _conv_from_ext(ext_ref, w_ref, bdw_ref[...], CONV_K, CV_HALO, tm)
        c_ref[...] = c
        mu = jnp.mean(c, axis=-1, keepdims=True)
        xc = c - mu
        rstd = lax.rsqrt(jnp.mean(xc * xc, axis=-1, keepdims=True) + LN_EPS)
        l = xc * rstd * lg_ref[...] + lb_ref[...]
        s_ref[...] = (l * _sigmoid(l)).astype(BF16)

    vec = pl.BlockSpec((1, Dm), lambda i: (0, 0))
    return pl.pallas_call(
        body, name=name, grid=(L // tm,),
        in_specs=[pl.BlockSpec((tm, Dm), lambda i: (i, 0)), pl.BlockSpec((tm, Dm), lambda i: (i, 1)),
                  pl.BlockSpec((CV_HALO, Dm), lambda i: (_prev_blk(i, r), 0)),
                  pl.BlockSpec((CV_HALO, Dm), lambda i: (_prev_blk(i, r), 1)),
                  pl.BlockSpec((CONV_K, Dm), lambda i: (0, 0)), vec, vec, vec],
        out_specs=[pl.BlockSpec((tm, Dm), lambda i: (i, 0)), pl.BlockSpec((tm, Dm), lambda i: (i, 0))],
        out_shape=[jax.ShapeDtypeStruct((L, Dm), BF16), jax.ShapeDtypeStruct((L, Dm), F32)],
        scratch_shapes=[pltpu.VMEM((CV_HALO + tm, Dm), F32)],
        compiler_params=_params(("parallel",)),
    )(u, u, u, u, w_dw, b_dw, ln_g, ln_b)


def _cv_bwd(ds, c, u, w_dw, ln_g, ln_b, *, name):
    L = u.shape[0]
    Dm = D_MODEL
    tm = _tile(L, 128)
    r = tm // CV_HALO
    nt = L // tm
    nblk = L // CV_HALO
    ext_rows = tm + CV_HALO

    def body(ds_ref, dsn_ref, c_ref, cn_ref, a_ref, g_ref, w_ref, lg_ref, lb_ref,
             du_ref, dw_ref, dbdw_ref, dlg_ref, dlb_ref, dbin_ref, dsx_ref, cx_ref, dext_ref):
        i = pl.program_id(0)

        @pl.when(i == 0)
        def _():
            for ref in (dw_ref, dbdw_ref, dlg_ref, dlb_ref, dbin_ref):
                ref[...] = jnp.zeros_like(ref)

        keep = (i < nt - 1).astype(F32)
        dsx_ref[0:tm, :] = ds_ref[...]
        dsx_ref[tm:ext_rows, :] = dsn_ref[...] * keep
        cx_ref[0:tm, :] = c_ref[...]
        cx_ref[tm:ext_rows, :] = cn_ref[...]
        cv = cx_ref[...]
        mu = jnp.mean(cv, axis=-1, keepdims=True)
        xc = cv - mu
        rstd = lax.rsqrt(jnp.mean(xc * xc, axis=-1, keepdims=True) + LN_EPS)
        nrm = xc * rstd
        lg = lg_ref[...]
        l = nrm * lg + lb_ref[...]
        dl = dsx_ref[...] * _dsilu(l, _sigmoid(l))
        dn = dl * lg
        dc = rstd * (dn - jnp.mean(dn, axis=-1, keepdims=True)
                     - nrm * jnp.mean(dn * nrm, axis=-1, keepdims=True))
        dext_ref[...] = dc
        dlg_ref[...] += jnp.sum((dl * nrm)[0:tm], axis=0, keepdims=True)
        dlb_ref[...] += jnp.sum(dl[0:tm], axis=0, keepdims=True)
        dbdw_ref[...] += jnp.sum(dc[0:tm], axis=0, keepdims=True)

        av, gv = a_ref[...], g_ref[...]
        sg = _sigmoid(gv)
        dv = _conv_back(dext_ref, av * sg, w_ref, dw_ref, CONV_K, tm)
        da = dv * sg
        dgate = dv * av * sg * (1.0 - sg)
        du_ref[:, 0:Dm] = da.astype(BF16)
        du_ref[:, Dm:2 * Dm] = dgate.astype(BF16)
        dbin_ref[:, 0:Dm] += jnp.sum(da, axis=0, keepdims=True)
        dbin_ref[:, Dm:2 * Dm] += jnp.sum(dgate, axis=0, keepdims=True)

    row = pl.BlockSpec((tm, Dm), lambda i: (i, 0))
    nxt = pl.BlockSpec((CV_HALO, Dm), lambda i: (_next_blk(i, r, nblk), 0))
    vec = pl.BlockSpec((1, Dm), lambda i: (0, 0))
    return pl.pallas_call(
        body, name=name, grid=(nt,),
        in_specs=[row, nxt, row, nxt, row, pl.BlockSpec((tm, Dm), lambda i: (i, 1)),
                  pl.BlockSpec((CONV_K, Dm), lambda i: (0, 0)), vec, vec],
        out_specs=[pl.BlockSpec((tm, 2 * Dm), lambda i: (i, 0)), pl.BlockSpec((CONV_K, Dm), lambda i: (0, 0)),
                   vec, vec, vec, pl.BlockSpec((1, 2 * Dm), lambda i: (0, 0))],
        out_shape=[jax.ShapeDtypeStruct((L, 2 * Dm), BF16), jax.ShapeDtypeStruct((CONV_K, Dm), F32),
                   jax.ShapeDtypeStruct((1, Dm), F32), jax.ShapeDtypeStruct((1, Dm), F32),
                   jax.ShapeDtypeStruct((1, Dm), F32), jax.ShapeDtypeStruct((1, 2 * Dm), F32)],
        scratch_shapes=[pltpu.VMEM((ext_rows, Dm), F32)] * 3,
        compiler_params=_params(("arbitrary",)),
    )(ds, ds, c, c, u, u, w_dw, ln_g, ln_b)


FFN_TC = 512
FFN_NJ = FFN_HIDDEN // FFN_TC
HALO8 = 8


def _ffn_mid_fwd(u0, w_dw, b_dw, *, name):
    L = u0.shape[1]
    tm = _tile(L, 256)
    r = tm // HALO8
    tc, nj = FFN_TC, FFN_NJ

    def body(u_ref, uh_ref, w_ref, b_ref, o_ref, ext_ref):
        keep = (pl.program_id(0) > 0).astype(F32)
        conv = []
        for h in range(2):
            ext = ext_ref.at[h]
            ext[0:HALO8, :] = uh_ref[h] * keep
            ext[HALO8:HALO8 + tm, :] = u_ref[h]
            conv.append(_conv_from_ext(ext, w_ref.at[h], b_ref[h], FFN_K, HALO8, tm))
        ug, uv = conv
        o_ref[...] = (ug * _sigmoid(ug) * uv).astype(BF16)

    return pl.pallas_call(
        body, name=name, grid=(L // tm, nj),
        in_specs=[pl.BlockSpec((2, tm, tc), lambda i, j: (0, i, j)),
                  pl.BlockSpec((2, HALO8, tc), lambda i, j: (0, _prev_blk(i, r), j)),
                  pl.BlockSpec((2, FFN_K, tc), lambda i, j: (0, 0, j)), pl.BlockSpec((2, 1, tc), lambda i, j: (0, 0, j))],
        out_specs=pl.BlockSpec((tm, tc), lambda i, j: (i, j)),
        out_shape=jax.ShapeDtypeStruct((L, FFN_HIDDEN), BF16),
        scratch_shapes=[pltpu.VMEM((2, HALO8 + tm, tc), F32)],
        compiler_params=_params(("parallel", "parallel")),
    )(u0, u0, w_dw, b_dw)


def _ffn_bwd(dg, u0, w_dw, b_dw, *, name):
    L = u0.shape[1]
    tm = _tile(L, 256)
    r = tm // HALO8
    nt = L // tm
    nblk = L // HALO8
    tc, nj = FFN_TC, FFN_NJ
    rows = tm + HALO8

    def body(up_ref, uc_ref, un_ref, dg_ref, dgn_ref, w_ref, b_ref, du0_ref, dw_ref, db_ref, ext_ref, dext_ref):
        i = pl.program_id(1)

        @pl.when(i == 0)
        def _():
            dw_ref[...] = jnp.zeros_like(dw_ref)
            db_ref[...] = jnp.zeros_like(db_ref)

        keep_prev = (i > 0).astype(F32)
        keep_next = (i < nt - 1).astype(F32)
        conv = []
        for h in range(2):
            ext = ext_ref.at[h]
            ext[0:HALO8, :] = up_ref[h] * keep_prev
            ext[HALO8:HALO8 + tm, :] = uc_ref[h]
            ext[HALO8 + tm:HALO8 + rows, :] = un_ref[h]
            conv.append(_conv_from_ext(ext, w_ref.at[h], b_ref[h], FFN_K, HALO8, rows))
        ug, uv = conv
        dgx = jnp.concatenate([dg_ref[...], dgn_ref[...] * keep_next], axis=0)
        sg = _sigmoid(ug)
        for h, du in ((0, dgx * uv * _dsilu(ug, sg)), (1, dgx * (ug * sg))):
            dext = dext_ref.at[h]
            dext[...] = du
            dbh = db_ref.at[h]
            dbh[...] += jnp.sum(du[0:tm], axis=0, keepdims=True)
            du0_ref[h] = _conv_back(dext, uc_ref[h], w_ref.at[h], dw_ref.at[h], FFN_K, tm).astype(BF16)

    cur = pl.BlockSpec((2, tm, tc), lambda j, i: (0, i, j))
    prv = pl.BlockSpec((2, HALO8, tc), lambda j, i: (0, _prev_blk(i, r), j))
    nxt = pl.BlockSpec((2, HALO8, tc), lambda j, i: (0, _next_blk(i, r, nblk), j))
    wsp = pl.BlockSpec((2, FFN_K, tc), lambda j, i: (0, 0, j))
    bsp = pl.BlockSpec((2, 1, tc), lambda j, i: (0, 0, j))
    return pl.pallas_call(
        body, name=name, grid=(nj, nt),
        in_specs=[prv, cur, nxt, pl.BlockSpec((tm, tc), lambda j, i: (i, j)),
                  pl.BlockSpec((HALO8, tc), lambda j, i: (_next_blk(i, r, nblk), j)), wsp, bsp],
        out_specs=[cur, wsp, bsp],
        out_shape=[jax.ShapeDtypeStruct((2, L, FFN_HIDDEN), BF16), jax.ShapeDtypeStruct((2, FFN_K, FFN_HIDDEN), F32),
                   jax.ShapeDtypeStruct((2, 1, FFN_HIDDEN), F32)],
        scratch_shapes=[pltpu.VMEM((2, HALO8 + rows, tc), F32), pltpu.VMEM((2, rows, tc), F32)],
        compiler_params=_params(("parallel", "arbitrary")),
    )(u0, u0, u0, dg, dg, w_dw, b_dw)


def _ssm_unpad(zx_pad, *, name):
    L = zx_pad.shape[0]
    tm = _tile(L, 128)

    def body(p_ref, z_ref, xbc_ref, dt_ref, comp_ref):
        comp_ref[...] = jnp.zeros_like(comp_ref)
        for d in range(N_DEV):
            slab = p_ref[:, d * SSM_SLAB_PAD:(d + 1) * SSM_SLAB_PAD]
            if d:
                slab = pltpu.roll(slab, 8 * d, axis=1)
            comp_ref[:, 1280 * d:1280 * d + SSM_SLAB_PAD] += slab
        z_ref[...] = comp_ref[:, 0:SSM_INNER]
        xbc_ref[...] = comp_ref[:, SSM_INNER:SSM_INNER + SSM_CONV_DIM]
        dt_ref[...] = comp_ref[:, SSM_INNER + SSM_CONV_DIM:SSM_COMPACT]

    return pl.pallas_call(
        body, name=name, grid=(L // tm,),
        in_specs=[pl.BlockSpec((tm, N_DEV * SSM_SLAB_PAD), lambda i: (i, 0))],
        out_specs=[pl.BlockSpec((tm, SSM_INNER), lambda i: (i, 0)), pl.BlockSpec((tm, SSM_CONV_DIM), lambda i: (i, 0)),
                   pl.BlockSpec((tm, HEAD_LANES), lambda i: (i, 0))],
        out_shape=[jax.ShapeDtypeStruct((L, SSM_INNER), F32), jax.ShapeDtypeStruct((L, SSM_CONV_DIM), F32),
                   jax.ShapeDtypeStruct((L, HEAD_LANES), F32)],
        scratch_shapes=[pltpu.VMEM((tm, SSM_COMPACT), F32)],
        compiler_params=_params(("parallel",)),
    )(zx_pad)


def _ssm_pad(dz, dxbc, ddt, *, name):
    L = dz.shape[0]
    tm = _tile(L, 128)

    def body(dz_ref, dx_ref, dt_ref, p_ref, comp_ref):
        comp_ref[:, 0:SSM_INNER] = dz_ref[...]
        comp_ref[:, SSM_INNER:SSM_INNER + SSM_CONV_DIM] = dx_ref[...]
        lane = lax.broadcasted_iota(jnp.int32, (tm, HEAD_LANES), 1)
        comp_ref[:, SSM_INNER + SSM_CONV_DIM:SSM_COMPACT] = jnp.where(lane < SSM_HEADS, dt_ref[...], 0.0)
        col = lax.broadcasted_iota(jnp.int32, (tm, SSM_SLAB_PAD), 1)
        for d in range(N_DEV):
            win = comp_ref[:, 1280 * d:1280 * d + SSM_SLAB_PAD]
            if d:
                win = pltpu.roll(win, SSM_SLAB_PAD - 8 * d, axis=1)
            p_ref[:, d * SSM_SLAB_PAD:(d + 1) * SSM_SLAB_PAD] = jnp.where(col < SSM_SLAB, win, 0.0).astype(BF16)

    return pl.pallas_call(
        body, name=name, grid=(L // tm,),
        in_specs=[pl.BlockSpec((tm, SSM_INNER), lambda i: (i, 0)), pl.BlockSpec((tm, SSM_CONV_DIM), lambda i: (i, 0)),
                  pl.BlockSpec((tm, HEAD_LANES), lambda i: (i, 0))],
        out_specs=pl.BlockSpec((tm, N_DEV * SSM_SLAB_PAD), lambda i: (i, 0)),
        out_shape=jax.ShapeDtypeStruct((L, N_DEV * SSM_SLAB_PAD), BF16),
        scratch_shapes=[pltpu.VMEM((tm, SSM_COMPACT), F32)],
        compiler_params=_params(("parallel",)),
    )(dz, dxbc, ddt)


SSM_TC = 1024


def _ssm_conv_fwd(xpre, w, b, *, name):
    L, C = xpre.shape
    tm = _tile(L, 256)
    r = tm // HALO8
    tc = SSM_TC

    def body(x_ref, xh_ref, w_ref, b_ref, o_ref, ext_ref):
        keep = (pl.program_id(0) > 0).astype(F32)
        ext_ref[0:HALO8, :] = xh_ref[...] * keep
        ext_ref[HALO8:HALO8 + tm, :] = x_ref[...]
        pre = _conv_from_ext(ext_ref, w_ref, b_ref[...], SSM_CONV_K, HALO8, tm)
        o_ref[...] = pre * _sigmoid(pre)

    return pl.pallas_call(
        body, name=name, grid=(L // tm, C // tc),
        in_specs=[pl.BlockSpec((tm, tc), lambda i, j: (i, j)), pl.BlockSpec((HALO8, tc), lambda i, j: (_prev_blk(i, r), j)),
                  pl.BlockSpec((SSM_CONV_K, tc), lambda i, j: (0, j)), pl.BlockSpec((1, tc), lambda i, j: (0, j))],
        out_specs=pl.BlockSpec((tm, tc), lambda i, j: (i, j)),
        out_shape=jax.ShapeDtypeStruct((L, C), F32),
        scratch_shapes=[pltpu.VMEM((HALO8 + tm, tc), F32)],
        compiler_params=_params(("parallel", "parallel")),
    )(xpre, xpre, w, b)


def _ssm_conv_bwd(dy, xpre, w, b, *, name):
    L, C = xpre.shape
    tm = _tile(L, 256)
    r = tm // HALO8
    nt = L // tm
    nblk = L // HALO8
    tc = SSM_TC
    rows = tm + HALO8

    def body(xp_ref, xc_ref, xn_ref, dy_ref, dyn_ref, w_ref, b_ref, dx_ref, dw_ref, db_ref, ext_ref, dext_ref):
        i = pl.program_id(1)

        @pl.when(i == 0)
        def _():
            dw_ref[...] = jnp.zeros_like(dw_ref)
            db_ref[...] = jnp.zeros_like(db_ref)

        ext_ref[0:HALO8, :] = xp_ref[...] * (i > 0).astype(F32)
        ext_ref[HALO8:HALO8 + tm, :] = xc_ref[...]
        ext_ref[HALO8 + tm:HALO8 + rows, :] = xn_ref[...]
        pre = _conv_from_ext(ext_ref, w_ref, b_ref[...], SSM_CONV_K, HALO8, rows)
        dyx = jnp.concatenate([dy_ref[...], dyn_ref[...] * (i < nt - 1).astype(F32)], axis=0)
        dpre = dyx * _dsilu(pre, _sigmoid(pre))
        dext_ref[...] = dpre
        db_ref[...] += jnp.sum(dpre[0:tm], axis=0, keepdims=True)
        dx_ref[...] = _conv_back(dext_ref, xc_ref[...], w_ref, dw_ref, SSM_CONV_K, tm)

    cur = pl.BlockSpec((tm, tc), lambda j, i: (i, j))
    prv = pl.BlockSpec((HALO8, tc), lambda j, i: (_prev_blk(i, r), j))
    nxt = pl.BlockSpec((HALO8, tc), lambda j, i: (_next_blk(i, r, nblk), j))
    return pl.pallas_call(
        body, name=name, grid=(C // tc, nt),
        in_specs=[prv, cur, nxt, cur, nxt, pl.BlockSpec((SSM_CONV_K, tc), lambda j, i: (0, j)),
                  pl.BlockSpec((1, tc), lambda j, i: (0, j))],
        out_specs=[cur, pl.BlockSpec((SSM_CONV_K, tc), lambda j, i: (0, j)), pl.BlockSpec((1, tc), lambda j, i: (0, j))],
        out_shape=[jax.ShapeDtypeStruct((L, C), F32), jax.ShapeDtypeStruct((SSM_CONV_K, C), F32),
                   jax.ShapeDtypeStruct((1, C), F32)],
        scratch_shapes=[pltpu.VMEM((HALO8 + rows, tc), F32), pltpu.VMEM((rows, tc), F32)],
        compiler_params=_params(("parallel", "arbitrary")),
    )(xpre, xpre, xpre, dy, dy, w, b)


def _gnorm_fwd(y, z, g, *, name):
    L, C = y.shape
    tm = _tile(L, 256)
    gw = SSM_GROUP_W

    def body(y_ref, z_ref, g_ref, o_ref):
        for k in range(SSM_GROUPS):
            sl = slice(k * gw, (k + 1) * gw)
            zv = z_ref[:, sl]
            yz = y_ref[:, sl] * zv * _sigmoid(zv)
            r = lax.rsqrt(jnp.mean(yz * yz, axis=-1, keepdims=True) + RMS_EPS)
            o_ref[:, sl] = (yz * r * g_ref[:, sl]).astype(BF16)

    row = pl.BlockSpec((tm, C), lambda i: (i, 0))
    return pl.pallas_call(
        body, name=name, grid=(L // tm,),
        in_specs=[row, row, pl.BlockSpec((1, C), lambda i: (0, 0))], out_specs=row,
        out_shape=jax.ShapeDtypeStruct((L, C), BF16),
        compiler_params=_params(("parallel",)),
    )(y, z, g)


def _gnorm_bwd(dn, y, z, g, *, name):
    L, C = y.shape
    tm = _tile(L, 256)
    gw = SSM_GROUP_W

    def body(dn_ref, y_ref, z_ref, g_ref, dy_ref, dz_ref, dg_ref):
        @pl.when(pl.program_id(0) == 0)
        def _():
            dg_ref[...] = jnp.zeros_like(dg_ref)

        for k in range(SSM_GROUPS):
            sl = slice(k * gw, (k + 1) * gw)
            zv, yv = z_ref[:, sl], y_ref[:, sl]
            sz = _sigmoid(zv)
            silu = zv * sz
            yz = yv * silu
            r = lax.rsqrt(jnp.mean(yz * yz, axis=-1, keepdims=True) + RMS_EPS)
            nrm = yz * r
            dnv = dn_ref[:, sl]
            dg_ref[:, sl] += jnp.sum(dnv * nrm, axis=0, keepdims=True)
            dh = dnv * g_ref[:, sl]
            dyz = r * (dh - nrm * jnp.mean(dh * nrm, axis=-1, keepdims=True))
            dy_ref[:, sl] = dyz * silu
            dz_ref[:, sl] = dyz * yv * _dsilu(zv, sz)

    row = pl.BlockSpec((tm, C), lambda i: (i, 0))
    vec = pl.BlockSpec((1, C), lambda i: (0, 0))
    return pl.pallas_call(
        body, name=name, grid=(L // tm,),
        in_specs=[row, row, row, vec], out_specs=[row, row, vec],
        out_shape=[jax.ShapeDtypeStruct((L, C), F32), jax.ShapeDtypeStruct((L, C), F32),
                   jax.ShapeDtypeStruct((1, C), F32)],
        compiler_params=_params(("arbitrary",)),
    )(dn, y, z, g)


def _ssd_consts():
    q, gw = CHUNK, SSM_GROUP_W
    sub = lax.broadcasted_iota(jnp.int32, (q, gw), 0)
    lane_in = lax.broadcasted_iota(jnp.int32, (q, gw), 1) % q
    sub2 = lax.broadcasted_iota(jnp.int32, (gw, gw), 0) // q
    lane2 = lax.broadcasted_iota(jnp.int32, (gw, gw), 1) // q
    t = lax.broadcasted_iota(jnp.int32, (q, q), 0)
    u = lax.broadcasted_iota(jnp.int32, (q, q), 1)
    return dict(
        diag=(lane_in == sub), low=(lane_in <= sub), up=(sub <= lane_in), block=(sub2 == lane2),
        tri=(u <= t).astype(BF16), trit=(u >= t).astype(BF16), ones=jnp.ones((q, q), BF16),
        last=(lax.broadcasted_iota(jnp.int32, (q, HEAD_LANES), 0) == q - 1))


def _ssd_chunk_terms(dt_raw, bias, a_log, e, et, k):
    dt = _softplus(dt_raw + bias)
    a_neg = -jnp.exp(a_log)
    cs = _dot3_r(k["tri"], dt * a_neg)
    cs_last = cs[CHUNK - 1:CHUNK, :]
    ecs = jnp.exp(cs)
    dte = jnp.exp(cs_last - cs)
    ecl_hn = _dot3_l(ecs, k["last"].astype(BF16), "tn")
    rows = _dot3_r(et, ecl_hn, parts=2)
    return dict(dt=dt, a_neg=a_neg, cs=cs, ecs=ecs, dte=dte, rows=rows,
                dtx=_dot3_l(dt, e, parts=1), csx=_dot3_l(cs, e, parts=2), ecsx=_dot3_l(ecs, e, parts=1),
                dtex=_dot3_l(dte, e, parts=1))


def _tile8(x):
    return jnp.concatenate([x] * 8, axis=0)


def _ssd_fwd(xbc, dt_raw, dt_bias, a_log, d_x, e, et, *, name):
    L = xbc.shape[0]
    nc = L // CHUNK
    q, gw, ns = CHUNK, SSM_GROUP_W, SSM_STATE

    def body(xbc_ref, dt_ref, bias_ref, alog_ref, dx_ref, e_ref, et_ref, y_ref, st_ref, s_ref):
        @pl.when(pl.program_id(0) == 0)
        def _():
            s_ref[...] = jnp.zeros_like(s_ref)

        k = _ssd_consts()
        t = _ssd_chunk_terms(dt_ref[...], bias_ref[...], alog_ref[...], e_ref[...], et_ref[...], k)
        st_ref[...] = s_ref[...]
        for g in range(SSM_GROUPS):
            ch = slice(g * gw, (g + 1) * gw)
            xs = xbc_ref[:, ch]
            bm = xbc_ref[:, SSM_INNER + g * ns:SSM_INNER + (g + 1) * ns].astype(BF16)
            cm = xbc_ref[:, SSM_INNER + (SSM_GROUPS + g) * ns:SSM_INNER + (SSM_GROUPS + g + 1) * ns].astype(BF16)
            xd = xs * t["dtx"][:, ch]
            csx = t["csx"][:, ch]
            csrow = _dot3_r(k["ones"], jnp.where(k["diag"], csx, 0.0), parts=2)
            lcat = jnp.where(k["low"], jnp.exp(jnp.minimum(csx - csrow, 0.0)), 0.0)
            mcat = _dot(cm, _tile8(bm), "nt") * lcat
            xdbd = jnp.where(k["block"], _tile8(xd), 0.0).astype(BF16)
            sg = s_ref[ch, :]
            y = _dot(mcat.astype(BF16), xdbd)
            y = y + _dot(cm, sg.astype(BF16), "nt") * t["ecsx"][:, ch]
            y_ref[:, ch] = y + dx_ref[:, ch] * xs
            s_ref[ch, :] = sg * t["rows"][ch, :] + _dot((xd * t["dtex"][:, ch]).astype(BF16), bm, "tn")

    hv = pl.BlockSpec((1, HEAD_LANES), lambda c: (0, 0))
    return pl.pallas_call(
        body, name=name, grid=(nc,),
        in_specs=[pl.BlockSpec((q, SSM_CONV_DIM), lambda c: (c, 0)), pl.BlockSpec((q, HEAD_LANES), lambda c: (c, 0)),
                  hv, hv, pl.BlockSpec((1, SSM_INNER), lambda c: (0, 0)),
                  pl.BlockSpec((HEAD_LANES, SSM_INNER), lambda c: (0, 0)),
                  pl.BlockSpec((SSM_INNER, HEAD_LANES), lambda c: (0, 0))],
        out_specs=[pl.BlockSpec((q, SSM_INNER), lambda c: (c, 0)),
                   pl.BlockSpec((None, SSM_INNER, ns), lambda c: (c, 0, 0))],
        out_shape=[jax.ShapeDtypeStruct((L, SSM_INNER), F32), jax.ShapeDtypeStruct((nc, SSM_INNER, ns), F32)],
        scratch_shapes=[pltpu.VMEM((SSM_INNER, ns), F32)],
        compiler_params=_params(("arbitrary",)),
    )(xbc, dt_raw, dt_bias, a_log, d_x, e, et)


def _ssd_bwd(dy, xbc, dt_raw, states, dt_bias, a_log, d_x, e, et, *, name):
    L = xbc.shape[0]
    nc = L // CHUNK
    q, gw, ns = CHUNK, SSM_GROUP_W, SSM_STATE

    def body(dy_ref, xbc_ref, dt_ref, st_ref, bias_ref, alog_ref, dx_ref, e_ref, et_ref,
             dxbc_ref, ddt_ref, dbias_ref, dalog_ref, dd_ref, ds_ref, f1_ref, f2_ref, f3_ref, f4_ref, fs_ref):
        @pl.when(pl.program_id(0) == 0)
        def _():
            ds_ref[...] = jnp.zeros_like(ds_ref)
            for ref in (dbias_ref, dalog_ref, dd_ref):
                ref[...] = jnp.zeros_like(ref)

        k = _ssd_consts()
        ev = e_ref[...]
        t = _ssd_chunk_terms(dt_ref[...], bias_ref[...], alog_ref[...], ev, et_ref[...], k)
        ones8 = jnp.ones((8, ns), BF16)
        for g in range(SSM_GROUPS):
            ch = slice(g * gw, (g + 1) * gw)
            bsl = slice(SSM_INNER + g * ns, SSM_INNER + (g + 1) * ns)
            csl = slice(SSM_INNER + (SSM_GROUPS + g) * ns, SSM_INNER + (SSM_GROUPS + g + 1) * ns)
            xs = xbc_ref[:, ch]
            bm = xbc_ref[:, bsl].astype(BF16)
            cm = xbc_ref[:, csl].astype(BF16)
            dyv = dy_ref[:, ch]
            dtx, ecsx, dtex, csx = t["dtx"][:, ch], t["ecsx"][:, ch], t["dtex"][:, ch], t["csx"][:, ch]
            xd = xs * dtx
            csrow = _dot3_r(k["ones"], jnp.where(k["diag"], csx, 0.0), parts=2)
            lcat = jnp.where(k["low"], jnp.exp(jnp.minimum(csx - csrow, 0.0)), 0.0)
            ltcat = jnp.where(k["up"], jnp.exp(jnp.minimum(csrow - csx, 0.0)), 0.0)
            mcat = _dot(cm, _tile8(bm), "nt") * lcat
            mtcat = _dot(bm, _tile8(cm), "nt") * ltcat
            xdbd = jnp.where(k["block"], _tile8(xd), 0.0).astype(BF16)
            dybd = jnp.where(k["block"], _tile8(dyv), 0.0).astype(BF16)
            gq = _dot(dyv.astype(BF16), xdbd, "nt") * mcat
            gt = _dot(xd.astype(BF16), dybd, "nt")
            gqt = gt * mtcat
            dcbt = _dot((gt * ltcat).astype(BF16), jnp.where(k["diag"], 1.0, 0.0).astype(BF16), "nt")
            dcbt = dcbt.astype(BF16)
            sg = st_ref[ch, :]
            dsg = ds_ref[ch, :]
            sgb, dsgb = sg.astype(BF16), dsg.astype(BF16)
            yoff = _dot(cm, sgb, "nt") * ecsx
            dye = (dyv * ecsx).astype(BF16)
            xdd = xd * dtex
            dxbc_ref[:, csl] = _dot(dcbt, bm, "tn") + _dot(dye, sgb)
            dxbc_ref[:, bsl] = _dot(dcbt, cm) + _dot(xdd.astype(BF16), dsgb)
            bds = _dot(bm, dsgb, "nt")
            dxd = _dot(mtcat.astype(BF16), dybd) + dtex * bds
            wx = xdd * bds
            ds_ref[ch, :] = t["rows"][ch, :] * dsg + _dot(dye, cm, "tn")
            f1_ref[:, ch] = gq - gqt + dyv * yoff - wx
            f2_ref[:, ch] = wx
            f3_ref[:, ch] = dxd * xs
            f4_ref[:, ch] = dyv * xs
            dxbc_ref[:, ch] = dxd * dtx + dx_ref[:, ch] * dyv
            fs_ref[:, ch] = _dot3_r(ones8, dsg * sg, "nt", parts=2)
        fold = lambda v, parts: _dot3_l(v, ev, "nt", parts=parts)
        f2 = fold(f2_ref[...], 1)
        last_row = jnp.sum(f2, axis=0, keepdims=True) + t["ecs"][q - 1:q, :] * fold(fs_ref[...], 2)[0:1, :]
        dcs = fold(f1_ref[...], 2) + jnp.where(k["last"], last_row, 0.0)
        da = _dot3_r(k["trit"], dcs)
        ddt = da * t["a_neg"] + fold(f3_ref[...], 2)
        ddt_raw = ddt * _sigmoid(dt_ref[...] + bias_ref[...])
        ddt_ref[...] = ddt_raw
        dbias_ref[...] += jnp.sum(ddt_raw, axis=0, keepdims=True)
        dalog_ref[...] += jnp.sum(da * t["dt"], axis=0, keepdims=True) * t["a_neg"]
        dd_ref[...] += jnp.sum(fold(f4_ref[...], 1), axis=0, keepdims=True)

    rev = lambda c: (nc - 1 - c, 0)
    hv = pl.BlockSpec((1, HEAD_LANES), lambda c: (0, 0))
    return pl.pallas_call(
        body, name=name, grid=(nc,),
        in_specs=[pl.BlockSpec((q, SSM_INNER), rev), pl.BlockSpec((q, SSM_CONV_DIM), rev),
                  pl.BlockSpec((q, HEAD_LANES), rev),
                  pl.BlockSpec((None, SSM_INNER, ns), lambda c: (nc - 1 - c, 0, 0)),
                  hv, hv, pl.BlockSpec((1, SSM_INNER), lambda c: (0, 0)),
                  pl.BlockSpec((HEAD_LANES, SSM_INNER), lambda c: (0, 0)),
                  pl.BlockSpec((SSM_INNER, HEAD_LANES), lambda c: (0, 0))],
        out_specs=[pl.BlockSpec((q, SSM_CONV_DIM), rev), pl.BlockSpec((q, HEAD_LANES), rev), hv, hv, hv],
        out_shape=[jax.ShapeDtypeStruct((L, SSM_CONV_DIM), F32), jax.ShapeDtypeStruct((L, HEAD_LANES), F32)]
        + [jax.ShapeDtypeStruct((1, HEAD_LANES), F32)] * 3,
        scratch_shapes=[pltpu.VMEM((SSM_INNER, ns), F32)] + [pltpu.VMEM((q, SSM_INNER), F32)] * 4
        + [pltpu.VMEM((8, SSM_INNER), F32)],
        compiler_params=_params(("arbitrary",)),
    )(dy, xbc, dt_raw, states, dt_bias, a_log, d_x, e, et)


def _my_index():
    return 4 * lax.axis_index("x") + 2 * lax.axis_index("y") + lax.axis_index("c")


def _peer(k):
    return (lax.axis_index("x") ^ ((k >> 2) & 1), lax.axis_index("y") ^ ((k >> 1) & 1), lax.axis_index("c") ^ (k & 1))


def _all_gather(shard, *, by_rows, name):
    nl, a, b = shard.shape
    out_shape = (nl, N_DEV * a, b) if by_rows else (nl, N_DEV, a, b)

    def body(src_ref, out_ref, send_sems, recv_sems, local_sems):
        me = _my_index()

        def mine(j):
            if by_rows:
                return out_ref.at[j, pl.ds(pl.multiple_of(me * a, 16), a), :]
            return out_ref.at[j, me]

        local = [pltpu.make_async_copy(src_ref.at[j], mine(j), local_sems.at[j]) for j in range(nl)]
        for cp in local:
            cp.start()
        copies = []
        for j in range(nl):
            for k in range(1, N_DEV):
                cp = pltpu.make_async_remote_copy(
                    src_ref=src_ref.at[j], dst_ref=mine(j), send_sem=send_sems.at[j, k - 1],
                    recv_sem=recv_sems.at[j, k - 1], device_id=_peer(k), device_id_type=MESH_ID)
                cp.start()
                copies.append(cp)
        for cp in copies:
            cp.wait()
        for cp in local:
            cp.wait()

    return pl.pallas_call(
        body, name=name, in_specs=[_HBM], out_specs=_HBM,
        out_shape=jax.ShapeDtypeStruct(out_shape, shard.dtype),
        scratch_shapes=[pltpu.SemaphoreType.DMA((nl, N_DEV - 1)), pltpu.SemaphoreType.DMA((nl, N_DEV - 1)),
                        pltpu.SemaphoreType.DMA((nl,))],
    )(shard)


_EFFECT = pltpu.SideEffectType.DATAFLOW_SIDE_EFFECTING


def _blk(ref, i, rows):
    if len(ref.shape) == 3:
        return ref.at[i]
    return ref.at[pl.ds(pl.multiple_of(i * rows, 16), rows), :]


def _remote_copies(scatter, src_ref, land_ref, send_sems, recv_sems, idx):
    me = _my_index()
    out = []
    for k in range(1, N_DEV):
        if scatter:
            src, dst = _blk(src_ref, me ^ k, land_ref.shape[1]), land_ref.at[me]
        else:
            src, dst = src_ref, _blk(land_ref, me, src_ref.shape[0])
        sem = idx * (N_DEV - 1) + k - 1
        out.append(pltpu.make_async_remote_copy(
            src_ref=src, dst_ref=dst, send_sem=send_sems.at[sem], recv_sem=recv_sems.at[sem],
            device_id=_peer(k), device_id_type=MESH_ID))
    return out


def _own_copy(scatter, src_ref, land_ref, own_sems, idx):
    me = _my_index()
    if scatter:
        src, dst = _blk(src_ref, me, land_ref.shape[1]), land_ref.at[me]
    else:
        src, dst = src_ref, _blk(land_ref, me, src_ref.shape[0])
    return pltpu.make_async_copy(src, dst, own_sems.at[idx])


def _copies_start(srcs, land_shapes, *, scatter, after, name):
    n = len(srcs)
    n_after = len(after)

    def body(*refs):
        src_refs, land_refs = refs[:n], refs[n:2 * n]
        send_sems, recv_sems, own_sems = refs[2 * n + n_after:2 * n + n_after + 3]
        for i in range(n):
            for cp in _remote_copies(scatter, src_refs[i], land_refs[i], send_sems, recv_sems, i):
                cp.start()
            _own_copy(scatter, src_refs[i], land_refs[i], own_sems, i).start()
        refs[-1][...] = jnp.zeros_like(refs[-1])

    lands = [lax.empty(shp, t.dtype) for shp, t in zip(land_shapes, srcs)]
    arrays = [pltpu.with_memory_space_constraint(t, pltpu.HBM) for t in list(srcs) + lands]
    sems = pltpu.SemaphoreType.DMA((n * (N_DEV - 1),))
    res = pl.pallas_call(
        body, name=name,
        out_shape=(sems, sems, pltpu.SemaphoreType.DMA((n,)), *[pltpu.HBM(t.shape, t.dtype) for t in arrays],
                   jax.ShapeDtypeStruct((8, 128), F32)),
        in_specs=[_HBM] * (2 * n) + [_ANY] * n_after,
        out_specs=(_SEM, _SEM, _SEM, *[_HBM] * (2 * n), pl.BlockSpec(memory_space=pltpu.VMEM)),
        input_output_aliases={i: 3 + i for i in range(2 * n)},
        compiler_params=pltpu.CompilerParams(has_side_effects=_EFFECT),
    )(*arrays, *after)
    return (res[:3], res[3:3 + n], res[3 + n:3 + 2 * n]), res[-1]


def _copies_wait(started, *, scatter, after, name):
    sems, srcs, lands = started
    n = len(srcs)

    def body(*refs):
        src_refs, land_refs = refs[:n], refs[n:2 * n]
        s_sems, r_sems, o_sems = refs[2 * n:2 * n + 3]
        for i in range(n):
            for cp in _remote_copies(scatter, src_refs[i], land_refs[i], s_sems, r_sems, i):
                cp.wait_send()
                cp.wait_recv()
            _own_copy(scatter, src_refs[i], land_refs[i], o_sems, i).wait()

    arrays = list(srcs) + list(lands)
    res = pl.pallas_call(
        body, name=name,
        out_shape=tuple(pltpu.HBM(t.shape, t.dtype) for t in arrays),
        in_specs=[_HBM] * (2 * n) + [_SEM] * 3 + [_ANY] * len(after), out_specs=tuple([_HBM] * (2 * n)),
        input_output_aliases={i: i for i in range(2 * n)},
        compiler_params=pltpu.CompilerParams(has_side_effects=_EFFECT),
    )(*arrays, *sems, *after)
    return res[n:]


def _adamw_math(w, g, m, v):
    m = ADAM_B1 * m + (1.0 - ADAM_B1) * g
    v = ADAM_B2 * v + (1.0 - ADAM_B2) * (g * g)
    m_hat = m / (1.0 - ADAM_B1 ** ADAM_STEP)
    v_hat = v / (1.0 - ADAM_B2 ** ADAM_STEP)
    delta = -ADAM_LR * (m_hat / (jnp.sqrt(v_hat) + ADAM_EPS) + ADAM_WD * w)
    return delta, m, v


def _adamw(recvs, w, m, v, *, name):
    nl, R, C = w.shape
    tr = max(t for t in range(8, (128 if C > 1024 else 512) + 1, 8) if R % t == 0)

    def body(*refs):
        r_refs = refs[:nl]
        w_ref, m_ref, v_ref, g_ref, d_ref, nm_ref, nv_ref = refs[nl:]
        for layer in range(nl):
            @pl.when(pl.program_id(0) == layer)
            def _(r_ref=r_refs[layer]):
                g = r_ref[0].astype(F32)
                for s in range(1, N_DEV):
                    g = g + r_ref[s].astype(F32)
                delta, nm, nv = _adamw_math(w_ref[...], g, m_ref[...], v_ref[...])
                g_ref[...] = g
                d_ref[...] = delta
                nm_ref[...] = nm
                nv_ref[...] = nv

    def recv_spec(layer):
        return pl.BlockSpec((N_DEV, tr, C), lambda j, i: (0, jnp.where(j == layer, i, 0), 0))

    row = pl.BlockSpec((None, tr, C), lambda j, i: (j, i, 0))
    return pl.pallas_call(
        body, name=name, grid=(nl, R // tr),
        in_specs=[recv_spec(layer) for layer in range(nl)] + [row, row, row],
        out_specs=[row] * 4, out_shape=[jax.ShapeDtypeStruct((nl, R, C), F32)] * 4,
        compiler_params=_params(("parallel", "parallel")),
    )(*recvs, w, m, v)


def _sum_slots(g8, *, name):
    _, P, C = g8.shape

    def body(r_ref, o_ref):
        g = r_ref[0]
        for s in range(1, N_DEV):
            g = g + r_ref[s]
        o_ref[...] = g

    return pl.pallas_call(
        body, name=name, grid=(1,),
        in_specs=[pl.BlockSpec((N_DEV, P, C), lambda i: (0, 0, 0))],
        out_specs=pl.BlockSpec((P, C), lambda i: (0, 0)),
        out_shape=jax.ShapeDtypeStruct((P, C), F32),
        compiler_params=_params(("arbitrary",)),
    )(g8)


def _adamw_small(g, w, m, v, *, name):
    P, C = w.shape

    def body(g_ref, w_ref, m_ref, v_ref, d_ref, nm_ref, nv_ref):
        delta, nm, nv = _adamw_math(w_ref[...], g_ref[...], m_ref[...], v_ref[...])
        d_ref[...] = delta
        nm_ref[...] = nm
        nv_ref[...] = nv

    full = pl.BlockSpec((P, C), lambda i: (0, 0))
    return pl.pallas_call(
        body, name=name, grid=(1,), in_specs=[full] * 4, out_specs=[full] * 3,
        out_shape=[jax.ShapeDtypeStruct((P, C), F32)] * 3,
        compiler_params=_params(("arbitrary",)),
    )(g, w, m, v)


def _pack(arrays):
    flat = jnp.concatenate([a.reshape(-1) for a in arrays])
    pad = (-flat.shape[0]) % (8 * 128)
    return jnp.pad(flat, (0, pad)).reshape(-1, 128)


def _unpack(buf, shapes):
    flat = buf.reshape(-1)
    out, off = [], 0
    for shp in shapes:
        n = 1
        for s in shp:
            n *= s
        out.append(flat[off:off + n].reshape(shp))
        off += n
    return out


def _expand_matrices():
    h = lax.broadcasted_iota(jnp.int32, (HEAD_LANES, SSM_INNER), 0)
    col = lax.broadcasted_iota(jnp.int32, (HEAD_LANES, SSM_INNER), 1) // SSM_HEAD_DIM
    e = (h == col).astype(BF16)
    return e, e.T


def _pad_heads(v):
    return jnp.pad(v.reshape(1, -1), ((0, 0), (0, HEAD_LANES - v.shape[-1])))


SHARDED_SMALL = ("cv_w_dw", "ssm_w_conv", "ssm_b_conv", "ssm_norm_g", "ffn_w_dw")
REPLICATED_SMALL = ("norm_mix_g", "norm_ffn_g", "norm_final_g", "cv_b_in", "cv_b_dw", "cv_ln_g", "cv_ln_b",
                    "cv_b_out", "ssm_dt_bias", "ssm_a_log", "ssm_d", "ffn_b_dw")
SMALL = REPLICATED_SMALL + SHARDED_SMALL
BIG_COLS = ("cv_w_in", "ssm_w_in", "ffn_w_up")
BIG_ROWS = ("cv_w_out", "ssm_w_out", "ffn_w_down")
WEIGHTS = ("norm_mix_g", "norm_ffn_g", "norm_final_g", "cv_w_in", "cv_b_in", "cv_w_dw", "cv_b_dw", "cv_ln_g",
           "cv_ln_b", "cv_w_out", "cv_b_out", "ssm_w_in", "ssm_w_conv", "ssm_b_conv", "ssm_dt_bias", "ssm_a_log",
           "ssm_d", "ssm_norm_g", "ssm_w_out", "ffn_w_up", "ffn_w_dw", "ffn_b_dw", "ffn_w_down")


N_STAGES = 2 * DEPTH


def _stage_weights(si):
    i = si // 2
    if si % 2:
        return "ffn_w_up", "ffn_w_down", i
    return ("cv_w_in", "cv_w_out", i // 2) if i % 2 == 0 else ("ssm_w_in", "ssm_w_out", i // 2)


def _gather_small_weights(w):
    small_local = _pack([w[n] for n in SHARDED_SMALL])
    small8 = _all_gather(small_local[None], by_rows=False, name="gather_small")[0]
    full = {}
    per_dev = [_unpack(small8[d], [w[n].shape for n in SHARDED_SMALL]) for d in range(N_DEV)]
    for idx, n in enumerate(SHARDED_SMALL):
        full[n] = jnp.concatenate([per_dev[d][idx] for d in range(N_DEV)], axis=-1)
    for n in REPLICATED_SMALL:
        full[n] = w[n]
    return full


def _forward_backward(x, target, weights_of, full, emit):
    e, et = _expand_matrices()
    grads = {}

    saved = []
    for i in range(DEPTH):
        j = i // 2
        s = {"x_in": x}
        wc, wr, behind = weights_of(2 * i, x)
        h1 = _rms_fwd(x, full["norm_mix_g"][i][None], name=f"rms_mix_{i}")
        s["h1"] = h1
        if i % 2 == 0:
            u = _mm_cols_nn(h1, wc, bias=full["cv_b_in"][j][None], after=behind, name=f"cv_in_{i}")
            sact, c = _cv_mid_fwd(u, full["cv_w_dw"][j], full["cv_b_dw"][j][None], full["cv_ln_g"][j][None],
                                  full["cv_ln_b"][j][None], name=f"cv_mid_{i}")
            x = _mm_rows_nn(sact, wr, res=x, bias=full["cv_b_out"][j][None], name=f"cv_out_{i}")
            s.update(u=u, c=c, sact=sact)
        else:
            zx = _mm_cols_nn(h1, wc, after=behind, name=f"ssm_in_{i}")
            z, xpre, dt_raw = _ssm_unpad(zx, name=f"ssm_unpad_{i}")
            xbc = _ssm_conv_fwd(xpre, full["ssm_w_conv"][j], full["ssm_b_conv"][j][None], name=f"ssm_conv_{i}")
            hp = dict(dt_bias=_pad_heads(full["ssm_dt_bias"][j]), a_log=_pad_heads(full["ssm_a_log"][j]),
                      d_x=jnp.repeat(full["ssm_d"][j], SSM_HEAD_DIM)[None])
            y, states = _ssd_fwd(xbc, dt_raw, hp["dt_bias"], hp["a_log"], hp["d_x"], e, et, name=f"ssd_fwd_{i}")
            yn = _gnorm_fwd(y, z, full["ssm_norm_g"][j][None], name=f"gnorm_{i}")
            x = _mm_rows_nn(yn, wr, res=x, name=f"ssm_out_{i}")
            s.update(z=z, xpre=xpre, dt_raw=dt_raw, xbc=xbc, y=y, states=states, yn=yn, hp=hp)
        s["x_mid"] = x
        wc2, wr2, behind = weights_of(2 * i + 1, x)
        h2 = _rms_fwd(x, full["norm_ffn_g"][i][None], name=f"rms_ffn_{i}")
        u0 = _mm_cols_nn(h2, wc2, after=behind, halves=True, name=f"ffn_up_{i}")
        ffn_w = full["ffn_w_dw"][i].reshape(FFN_K, 2, FFN_HIDDEN).transpose(1, 0, 2)
        ffn_b = full["ffn_b_dw"][i].reshape(2, 1, FFN_HIDDEN)
        gact = _ffn_mid_fwd(u0, ffn_w, ffn_b, name=f"ffn_mid_{i}")
        x = _mm_rows_nn(gact, wr2, res=x, name=f"ffn_down_{i}")
        s.update(h2=h2, u0=u0, gact=gact, ffn_w=ffn_w, ffn_b=ffn_b, weights=(wc, wr, wc2, wr2))
        saved.append(s)

    dx, dxb, dg_final, loss = _final_loss(x, full["norm_final_g"][None], target, name="final_loss")
    grads["norm_final_g"] = dg_final[0]

    small_layers = {n: [None] * full[n].shape[0] for n in SMALL if n != "norm_final_g"}
    behind = None
    for i in reversed(range(DEPTH)):
        j = i // 2
        s = saved[i]
        wc, wr, wc2, wr2 = s["weights"]
        dgact = _mm_rows_nt(dxb, wr2, after=behind, name=f"ffn_down_dx_{i}")
        dwr2 = _mm_rows_tn(s["gact"], dxb, name=f"ffn_down_dw_{i}")
        du0, dw_dw, db_dw = _ffn_bwd(dgact, s["u0"], s["ffn_w"], s["ffn_b"], name=f"ffn_bwd_{i}")
        small_layers["ffn_w_dw"][i] = dw_dw.transpose(1, 0, 2).reshape(FFN_K, 2 * FFN_HIDDEN)
        small_layers["ffn_b_dw"][i] = db_dw.reshape(2 * FFN_HIDDEN)
        dh2 = _mm_cols_nt(du0, wc2, name=f"ffn_up_dx_{i}")
        dwc2 = _mm_cols_tn(s["h2"], du0, name=f"ffn_up_dw_{i}")
        dx, dxb, dg, colsum = _rms_bwd(s["x_mid"], full["norm_ffn_g"][i][None], dh2, dx, name=f"rms_ffn_bwd_{i}")
        small_layers["norm_ffn_g"][i] = dg[0]
        behind = emit(2 * i + 1, dwc2, dwr2)
        if i % 2 == 0:
            small_layers["cv_b_out"][j] = colsum[0]
            dsact = _mm_rows_nt(dxb, wr, after=behind, name=f"cv_out_dx_{i}")
            dwr = _mm_rows_tn(s["sact"], dxb, name=f"cv_out_dw_{i}")
            du, dw_dw, db_dw, dlg, dlb, db_in = _cv_bwd(dsact, s["c"], s["u"], full["cv_w_dw"][j], full["cv_ln_g"][j][None],
                                                        full["cv_ln_b"][j][None], name=f"cv_bwd_{i}")
            small_layers["cv_w_dw"][j] = dw_dw
            small_layers["cv_b_dw"][j] = db_dw[0]
            small_layers["cv_ln_g"][j] = dlg[0]
            small_layers["cv_ln_b"][j] = dlb[0]
            small_layers["cv_b_in"][j] = db_in[0]
            dh1 = _mm_cols_nt(du, wc, name=f"cv_in_dx_{i}")
            dwc = _mm_cols_tn(s["h1"], du, name=f"cv_in_dw_{i}")
        else:
            hp = s["hp"]
            dyn = _mm_rows_nt(dxb, wr, after=behind, name=f"ssm_out_dx_{i}")
            dwr = _mm_rows_tn(s["yn"], dxb, name=f"ssm_out_dw_{i}")
            dy, dz, dng = _gnorm_bwd(dyn, s["y"], s["z"], full["ssm_norm_g"][j][None], name=f"gnorm_bwd_{i}")
            small_layers["ssm_norm_g"][j] = dng[0]
            dxbc, ddt_raw, dbias, dalog, dd = _ssd_bwd(dy, s["xbc"], s["dt_raw"], s["states"], hp["dt_bias"], hp["a_log"],
                                                      hp["d_x"], e, et, name=f"ssd_bwd_{i}")
            small_layers["ssm_dt_bias"][j] = dbias[0, :SSM_HEADS]
            small_layers["ssm_a_log"][j] = dalog[0, :SSM_HEADS]
            small_layers["ssm_d"][j] = dd[0, :SSM_HEADS]
            dxpre, dw_conv, db_conv = _ssm_conv_bwd(dxbc, s["xpre"], full["ssm_w_conv"][j], full["ssm_b_conv"][j][None],
                                                    name=f"ssm_conv_bwd_{i}")
            small_layers["ssm_w_conv"][j] = dw_conv
            small_layers["ssm_b_conv"][j] = db_conv[0]
            dzx = _ssm_pad(dz, dxpre, ddt_raw, name=f"ssm_pad_{i}")
            dh1 = _mm_cols_nt(dzx, wc, name=f"ssm_in_dx_{i}")
            dwc = _mm_cols_tn(s["h1"], dzx, name=f"ssm_in_dw_{i}")
        dx, dxb, dg, _ = _rms_bwd(s["x_in"], full["norm_mix_g"][i][None], dh1, dx, name=f"rms_mix_bwd_{i}")
        small_layers["norm_mix_g"][i] = dg[0]
        behind = emit(2 * i, dwc, dwr)
    for n, layers in small_layers.items():
        grads[n] = jnp.stack(layers)
    return loss, dx, grads


def _update_big(names, recvs, w, m, v):
    out = {}
    for n in names:
        b = recvs[n][0].shape[-1]
        pad = ((0, 0), (0, 0), (0, b - w[n].shape[-1]))
        res = _adamw(recvs[n], *[jnp.pad(t, pad) for t in (w[n], m[n], v[n])], name="adamw_" + n)
        out[n] = [r[..., :w[n].shape[-1]] for r in res]
    return out


def _update_small(grads, w, m, v):
    me = _my_index()
    out = {}
    small_part = _pack([grads[n] for n in SMALL])
    small_all = _all_gather(small_part[None], by_rows=False, name="gather_small_grads")[0]
    small_sum = _sum_slots(small_all, name="sum_small_grads")
    gfull = dict(zip(SMALL, _unpack(small_sum, [grads[n].shape for n in SMALL])))
    glocal = []
    for n in SMALL:
        g = gfull[n]
        if n in SHARDED_SMALL:
            cols = w[n].shape[-1]
            g = lax.dynamic_slice_in_dim(g, me * cols, cols, axis=g.ndim - 1)
        glocal.append(g)
    packed = [_pack(glocal)] + [_pack([t[n] for n in SMALL]) for t in (w, m, v)]
    res = _adamw_small(*packed, name="adamw_small")
    shapes = [w[n].shape for n in SMALL]
    unpacked = [_unpack(r, shapes) for r in res]
    for idx, n in enumerate(SMALL):
        out[n] = [glocal[idx]] + [u[idx] for u in unpacked]
    return out


GATHER_AHEAD = 3


def _train_step(x, target, w, m, v):
    full = _gather_small_weights(w)
    gathers, exchanges = {}, {}
    last_start = [jnp.zeros((8, 128), F32)]

    def start_gather(si, after):
        cn, rn, j = _stage_weights(si)
        wc, wr = w[cn][j].astype(BF16), w[rn][j].astype(BF16)
        if cn == "ssm_w_in":
            wc = jnp.pad(wc, ((0, 0), (0, SSM_SLAB_PAD - SSM_SLAB)))
        shapes = [(N_DEV,) + wc.shape, (N_DEV * wr.shape[0], wr.shape[1])]
        gathers[si], last_start[0] = _copies_start([wc, wr], shapes, scatter=False, after=after,
                                                   name=f"gather_start_{si}")

    for si in range(GATHER_AHEAD):
        start_gather(si, [last_start[0]])

    def weights_of(si, x_act):
        lc, lr = _copies_wait(gathers.pop(si), scatter=False, after=[x_act, last_start[0]], name=f"gather_wait_{si}")
        behind = None
        if si + GATHER_AHEAD < N_STAGES:
            start_gather(si + GATHER_AHEAD, [last_start[0], lc])
            behind = last_start[0]
        return lc, lr, behind

    def emit(si, dwc, dwr):
        shapes = [dwc.shape, (N_DEV, dwr.shape[0] // N_DEV, dwr.shape[1])]
        exchanges[si], token = _copies_start([dwc, dwr], shapes, scatter=True, after=[], name=f"exchange_start_{si}")
        return token

    loss, dx, grads = _forward_backward(x, target, weights_of, full, emit)

    out = _update_small(grads, w, m, v)
    recvs = {n: [None] * w[n].shape[0] for n in BIG_COLS + BIG_ROWS}
    behind = [dx, out[SMALL[0]][1]]
    for si in reversed(range(N_STAGES)):
        cn, rn, j = _stage_weights(si)
        if si == 0:
            out.update(_update_big([n for n in BIG_COLS + BIG_ROWS if not n.startswith("cv_")], recvs, w, m, v))
            behind = [out["ffn_w_up"][1], out["ffn_w_down"][1], out["ssm_w_in"][1], out["ssm_w_out"][1]]
        recvs[cn][j], recvs[rn][j] = _copies_wait(exchanges.pop(si), scatter=True, after=behind,
                                                  name=f"exchange_wait_{si}")
        behind = [recvs[cn][j]]
    out.update(_update_big(["cv_w_in", "cv_w_out"], recvs, w, m, v))
    return lax.psum(loss[0, 0], AXES), dx, out


def kernel(x, norm_mix_g, norm_ffn_g, norm_final_g, cv_w_in, cv_b_in, cv_w_dw, cv_b_dw, cv_ln_g, cv_ln_b, cv_w_out, cv_b_out, ssm_w_in, ssm_w_conv, ssm_b_conv, ssm_dt_bias, ssm_a_log, ssm_d, ssm_norm_g, ssm_w_out, ffn_w_up, ffn_w_dw, ffn_b_dw, ffn_w_down, loss_target, m_norm_mix_g, m_norm_ffn_g, m_norm_final_g, m_cv_w_in, m_cv_b_in, m_cv_w_dw, m_cv_b_dw, m_cv_ln_g, m_cv_ln_b, m_cv_w_out, m_cv_b_out, m_ssm_w_in, m_ssm_w_conv, m_ssm_b_conv, m_ssm_dt_bias, m_ssm_a_log, m_ssm_d, m_ssm_norm_g, m_ssm_w_out, m_ffn_w_up, m_ffn_w_dw, m_ffn_b_dw, m_ffn_w_down, v_norm_mix_g, v_norm_ffn_g, v_norm_final_g, v_cv_w_in, v_cv_b_in, v_cv_w_dw, v_cv_b_dw, v_cv_ln_g, v_cv_ln_b, v_cv_w_out, v_cv_b_out, v_ssm_w_in, v_ssm_w_conv, v_ssm_b_conv, v_ssm_dt_bias, v_ssm_a_log, v_ssm_d, v_ssm_norm_g, v_ssm_w_out, v_ffn_w_up, v_ffn_w_dw, v_ffn_b_dw, v_ffn_w_down):
    args = locals()
    w = {n: args[n] for n in WEIGHTS}
    m = {n: args["m_" + n] for n in WEIGHTS}
    v = {n: args["v_" + n] for n in WEIGHTS}
    loss, dx, out = _train_step(x[0], loss_target[0], w, m, v)
    return (loss, dx[None], *[out[n][0] for n in WEIGHTS], *[out[n][1] for n in WEIGHTS],
            *[out[n][2] for n in WEIGHTS], *[out[n][3] for n in WEIGHTS])
```

```python
import functools

import jax
import jax.numpy as jnp
from jax import lax
from jax.experimental import pallas as pl
from jax.experimental.pallas import tpu as pltpu

F32, BF16 = jnp.float32, jnp.bfloat16
AXES = ("x", "y", "c")
N_DEV = 8
MESH_ID = pl.DeviceIdType.MESH

D_MODEL = 2048
DEPTH = 4
CHUNK = 64
CONV_K = 31
SSM_INNER = 4096
SSM_HEADS = 64
SSM_HEAD_DIM = 64
SSM_GROUPS = 8
SSM_GROUP_W = SSM_INNER // SSM_GROUPS
SSM_STATE = 128
SSM_CONV_K = 4
SSM_CONV_DIM = SSM_INNER + 2 * SSM_GROUPS * SSM_STATE
SSM_IN_DIM = SSM_INNER + SSM_CONV_DIM + SSM_HEADS
SSM_SLAB = SSM_IN_DIM // N_DEV
SSM_SLAB_PAD = 1408
SSM_COMPACT = 1280 * (N_DEV - 1) + SSM_SLAB_PAD
FFN_HIDDEN = 5632
FFN_K = 3
HEAD_LANES = 128
RMS_EPS = 1e-6
LN_EPS = 1e-5
ADAM_LR, ADAM_B1, ADAM_B2, ADAM_EPS, ADAM_WD, ADAM_STEP = 0.001, 0.9, 0.999, 1e-08, 0.01, 10

_DIMS = {
    "nn": (((1,), (0,)), ((), ())),
    "nt": (((1,), (1,)), ((), ())),
    "tn": (((0,), (0,)), ((), ())),
}


def _params(sem=None, vmem_mb=48):
    return pltpu.CompilerParams(dimension_semantics=sem, vmem_limit_bytes=vmem_mb << 20)


_HBM = pl.BlockSpec(memory_space=pltpu.HBM)
_ANY = pl.BlockSpec(memory_space=pl.ANY)
_SEM = pl.BlockSpec(memory_space=pltpu.SEMAPHORE)


def _dot(a, b, mode="nn"):
    return lax.dot_general(a, b, _DIMS[mode], preferred_element_type=F32)


def _split3(x):
    hi = x.astype(BF16)
    r1 = x - hi.astype(F32)
    mid = r1.astype(BF16)
    lo = (r1 - mid.astype(F32)).astype(BF16)
    return hi, mid, lo


def _dot3_l(x, m, mode="nn", parts=3):
    return sum(_dot(p, m, mode) for p in _split3(x)[:parts])


def _dot3_r(m, x, mode="nn", parts=3):
    return sum(_dot(m, p, mode) for p in _split3(x)[:parts])


def _sigmoid(x):
    return jax.nn.sigmoid(x)


def _dsilu(x, sg):
    return sg * (1.0 + x * (1.0 - sg))


def _softplus(x):
    return jnp.maximum(x, 0.0) + jnp.log(1.0 + jnp.exp(-jnp.abs(x)))


def _tile(n, pref):
    return min(n, pref)


def _matmul(a, b, *, mode, grid, a_spec, b_spec, o_spec, o_block, out_shape, nk, name,
            extras=(), epilogue=None, after=None, vmem_mb=48):
    n_extra = len(extras)
    if after is not None:
        extras = tuple(extras) + ((after, _ANY),)
    n_in = len(extras)

    def body(a_ref, b_ref, *rest):
        extra_refs = rest[:n_extra]
        o_ref = rest[n_in]
        part = _dot(a_ref[...].astype(BF16), b_ref[...].astype(BF16), mode)

        def finish(acc):
            if epilogue is not None:
                acc = epilogue(acc, *[r[...] for r in extra_refs])
            o_ref[...] = acc.astype(o_ref.dtype)

        if nk == 1:
            finish(part)
        else:
            acc_ref = rest[n_in + 1]
            k = pl.program_id(len(grid) - 1)

            @pl.when(k == 0)
            def _():
                acc_ref[...] = part

            @pl.when(k > 0)
            def _():
                acc_ref[...] += part

            @pl.when(k == nk - 1)
            def _():
                finish(acc_ref[...])

    scratch = [] if nk == 1 else [pltpu.VMEM(o_block, F32)]
    sem = ("parallel",) * (len(grid) - 1) + ("arbitrary",)
    return pl.pallas_call(
        body, name=name, grid=grid,
        in_specs=[a_spec, b_spec] + [s for _, s in extras],
        out_specs=o_spec, out_shape=out_shape, scratch_shapes=scratch,
        compiler_params=_params(sem, vmem_mb),
    )(a, b, *[x for x, _ in extras])


HALF_SLABS = N_DEV // 2


def _mm_cols_nn(h, wg, *, name, bias=None, after=None, halves=False):
    L, K = h.shape
    n = wg.shape[-1]
    tm = _tile(L, 512)
    extras, epi = (), None
    if bias is not None:
        extras = ((bias, pl.BlockSpec((1, n), lambda s, i: (0, s))),)
        epi = lambda acc, b: acc + b
    if halves:
        o_spec = pl.BlockSpec((None, tm, n), lambda s, i: (s // HALF_SLABS, i, s % HALF_SLABS))
        out_shape = jax.ShapeDtypeStruct((2, L, HALF_SLABS * n), F32)
    else:
        o_spec = pl.BlockSpec((tm, n), lambda s, i: (i, s))
        out_shape = jax.ShapeDtypeStruct((L, N_DEV * n), F32)
    return _matmul(
        h, wg, mode="nn", grid=(N_DEV, L // tm),
        a_spec=pl.BlockSpec((tm, K), lambda s, i: (i, 0)),
        b_spec=pl.BlockSpec((None, K, n), lambda s, i: (s, 0, 0)),
        o_spec=o_spec, o_block=(tm, n), out_shape=out_shape, nk=1, name=name,
        extras=extras, epilogue=epi, after=after)


def _mm_cols_nt(du, wg, *, name):
    L = du.shape[-2]
    K, n = wg.shape[-2:]
    tm = _tile(L, 512)
    if du.ndim == 3:
        a_spec = pl.BlockSpec((None, tm, n), lambda i, s: (s // HALF_SLABS, i, s % HALF_SLABS))
    else:
        a_spec = pl.BlockSpec((tm, n), lambda i, s: (i, s))
    return _matmul(
        du, wg, mode="nt", grid=(L // tm, N_DEV), a_spec=a_spec,
        b_spec=pl.BlockSpec((None, K, n), lambda i, s: (s, 0, 0)),
        o_spec=pl.BlockSpec((tm, K), lambda i, s: (i, 0)), o_block=(tm, K),
        out_shape=jax.ShapeDtypeStruct((L, K), F32), nk=N_DEV, name=name)


def _mm_cols_tn(h, du, *, name):
    L, K = h.shape
    tm = _tile(L, 512)
    if du.ndim == 3:
        n = du.shape[2] // HALF_SLABS
        b_spec = pl.BlockSpec((None, L, n), lambda s, q: (s // HALF_SLABS, 0, s % HALF_SLABS))
    else:
        n = du.shape[1] // N_DEV
        b_spec = pl.BlockSpec((L, n), lambda s, q: (0, s))
    tq = 256 if n > 1024 else 512
    return _matmul(
        h, du, mode="tn", grid=(N_DEV, K // tq),
        a_spec=pl.BlockSpec((L, tq), lambda s, q: (0, q)), b_spec=b_spec,
        o_spec=pl.BlockSpec((None, tq, n), lambda s, q: (s, q, 0)), o_block=(tq, n),
        out_shape=jax.ShapeDtypeStruct((N_DEV, K, n), BF16), nk=1, name=name)


def _mm_rows_nn(a, wg, *, res, name, bias=None):
    L, Kw = a.shape
    N = wg.shape[-1]
    tm, tn = _tile(L, 512), 512
    extras = [(res, pl.BlockSpec((tm, tn), lambda j, i: (i, j)))]
    if bias is not None:
        extras.append((bias, pl.BlockSpec((1, tn), lambda j, i: (0, j))))
        epi = lambda acc, r, b: acc + r + b
    else:
        epi = lambda acc, r: acc + r
    return _matmul(
        a, wg, mode="nn", grid=(N // tn, L // tm),
        a_spec=pl.BlockSpec((tm, Kw), lambda j, i: (i, 0)),
        b_spec=pl.BlockSpec((Kw, tn), lambda j, i: (0, j)),
        o_spec=pl.BlockSpec((tm, tn), lambda j, i: (i, j)), o_block=(tm, tn),
        out_shape=jax.ShapeDtypeStruct((L, N), F32), nk=1, name=name,
        extras=tuple(extras), epilogue=epi)


def _mm_rows_nt(dy, wg, *, name, after=None):
    L, N = dy.shape
    Kw = wg.shape[-2]
    tm, tn = _tile(L, 512), 512
    return _matmul(
        dy, wg, mode="nt", grid=(L // tm, Kw // tn),
        a_spec=pl.BlockSpec((tm, N), lambda i, q: (i, 0)),
        b_spec=pl.BlockSpec((tn, N), lambda i, q: (q, 0)),
        o_spec=pl.BlockSpec((tm, tn), lambda i, q: (i, q)), o_block=(tm, tn),
        out_shape=jax.ShapeDtypeStruct((L, Kw), F32), nk=1, name=name, after=after)


def _mm_rows_tn(a, dy, *, name):
    L, Kw = a.shape
    N = dy.shape[1]
    tq, tn = 512, 1024
    return _matmul(
        a, dy, mode="tn", grid=(N // tn, Kw // tq),
        a_spec=pl.BlockSpec((L, tq), lambda p, q: (0, q)),
        b_spec=pl.BlockSpec((L, tn), lambda p, q: (0, p)),
        o_spec=pl.BlockSpec((tq, tn), lambda p, q: (q, p)), o_block=(tq, tn),
        out_shape=jax.ShapeDtypeStruct((Kw, N), BF16), nk=1, name=name)


def _rms_fwd(x, g, *, name):
    L, Dm = x.shape
    tm = _tile(L, 256)

    def body(x_ref, g_ref, h_ref):
        xv = x_ref[...]
        r = lax.rsqrt(jnp.mean(xv * xv, axis=-1, keepdims=True) + RMS_EPS)
        h_ref[...] = (xv * r * g_ref[...]).astype(BF16)

    return pl.pallas_call(
        body, name=name, grid=(L // tm,),
        in_specs=[pl.BlockSpec((tm, Dm), lambda i: (i, 0)), pl.BlockSpec((1, Dm), lambda i: (0, 0))],
        out_specs=pl.BlockSpec((tm, Dm), lambda i: (i, 0)),
        out_shape=jax.ShapeDtypeStruct((L, Dm), BF16),
        compiler_params=_params(("parallel",)),
    )(x, g)


def _rms_bwd(x, g, dh, dres, *, name):
    L, Dm = x.shape
    tm = _tile(L, 256)

    def body(x_ref, g_ref, dh_ref, dres_ref, dx_ref, dxb_ref, dg_ref, cs_ref):
        i = pl.program_id(0)
        xv = x_ref[...]
        r = lax.rsqrt(jnp.mean(xv * xv, axis=-1, keepdims=True) + RMS_EPS)
        xh = xv * r
        dh = dh_ref[...]
        dxh = dh * g_ref[...]
        dx = dres_ref[...] + r * (dxh - xh * jnp.mean(dxh * xh, axis=-1, keepdims=True))
        dx_ref[...] = dx
        dxb_ref[...] = dx.astype(BF16)

        @pl.when(i == 0)
        def _():
            dg_ref[...] = jnp.zeros_like(dg_ref)
            cs_ref[...] = jnp.zeros_like(cs_ref)

        dg_ref[...] += jnp.sum(dh * xh, axis=0, keepdims=True)
        cs_ref[...] += jnp.sum(dx, axis=0, keepdims=True)

    row = pl.BlockSpec((tm, Dm), lambda i: (i, 0))
    vec = pl.BlockSpec((1, Dm), lambda i: (0, 0))
    return pl.pallas_call(
        body, name=name, grid=(L // tm,),
        in_specs=[row, vec, row, row], out_specs=[row, row, vec, vec],
        out_shape=[jax.ShapeDtypeStruct((L, Dm), F32), jax.ShapeDtypeStruct((L, Dm), BF16),
                   jax.ShapeDtypeStruct((1, Dm), F32), jax.ShapeDtypeStruct((1, Dm), F32)],
        compiler_params=_params(("arbitrary",)),
    )(x, g, dh, dres)


def _final_loss(x, g, target, *, name):
    L, Dm = x.shape
    tm = _tile(L, 256)

    def body(x_ref, g_ref, t_ref, dx_ref, dxb_ref, dg_ref, loss_ref):
        i = pl.program_id(0)
        xv = x_ref[...]
        gv = g_ref[...]
        r = lax.rsqrt(jnp.mean(xv * xv, axis=-1, keepdims=True) + RMS_EPS)
        xh = xv * r
        err = xh * gv - t_ref[...]
        dy = err * (1.0 / Dm)
        dxh = dy * gv
        dx = r * (dxh - xh * jnp.mean(dxh * xh, axis=-1, keepdims=True))
        dx_ref[...] = dx
        dxb_ref[...] = dx.astype(BF16)

        @pl.when(i == 0)
        def _():
            dg_ref[...] = jnp.zeros_like(dg_ref)
            loss_ref[...] = jnp.zeros_like(loss_ref)

        dg_ref[...] += jnp.sum(dy * xh, axis=0, keepdims=True)
        loss_ref[...] += 0.5 * jnp.sum(jnp.mean(err * err, axis=-1, keepdims=True), axis=0, keepdims=True)

    row = pl.BlockSpec((tm, Dm), lambda i: (i, 0))
    vec = pl.BlockSpec((1, Dm), lambda i: (0, 0))
    return pl.pallas_call(
        body, name=name, grid=(L // tm,),
        in_specs=[row, vec, row],
        out_specs=[row, row, vec, pl.BlockSpec((1, 128), lambda i: (0, 0))],
        out_shape=[jax.ShapeDtypeStruct((L, Dm), F32), jax.ShapeDtypeStruct((L, Dm), BF16),
                   jax.ShapeDtypeStruct((1, Dm), F32), jax.ShapeDtypeStruct((1, 128), F32)],
        compiler_params=_params(("arbitrary",)),
    )(x, g, target)


def _conv_from_ext(ext_ref, w_ref, bias, taps, halo, rows):
    acc = jnp.broadcast_to(bias, (rows, ext_ref.shape[1]))
    for j in range(taps):
        acc = acc + w_ref[taps - 1 - j:taps - j, :] * ext_ref[halo - j:halo - j + rows, :]
    return acc


def _conv_back(dext_ref, x_cur, w_ref, dw_ref, taps, rows):
    dx = jnp.zeros((rows, dext_ref.shape[1]), F32)
    for j in range(taps):
        sh = dext_ref[j:j + rows, :]
        dx = dx + w_ref[taps - 1 - j:taps - j, :] * sh
        dw_ref[taps - 1 - j:taps - j, :] += jnp.sum(x_cur * sh, axis=0, keepdims=True)
    return dx


STRIP = 128
ROW_BLOCK = 64


def _strips(width):
    return [slice(s, s + STRIP) for s in range(0, width, STRIP)]


def _row_blocks(n):
    return [(lo, min(ROW_BLOCK, n - lo)) for lo in range(0, n, ROW_BLOCK)]


def _conv_strip(ext_ref, taps, bias, halo, lo, n, ls):
    acc = jnp.broadcast_to(bias, (n, STRIP))
    for j in range(len(taps)):
        acc = acc + taps[len(taps) - 1 - j] * ext_ref[halo + lo - j:halo + lo - j + n, ls]
    return acc


def _conv_back_strip(dext_ref, x_blk, taps, dw_acc, lo, n, ls):
    dx = jnp.zeros((n, STRIP), F32)
    for j in range(len(taps)):
        k = len(taps) - 1 - j
        sh = dext_ref[lo + j:lo + j + n, ls]
        dx = dx + taps[k] * sh
        dw_acc[k] = dw_acc[k] + jnp.sum(x_blk * sh, axis=0, keepdims=True)
    return dx


def _prev_blk(i, r):
    return jnp.maximum(i * r - 1, 0)


def _next_blk(i, r, nblk):
    return jnp.minimum((i + 1) * r, nblk - 1)


CV_HALO = 32


def _cv_mid_fwd(u, w_dw, b_dw, ln_g, ln_b, *, name):
    L = u.shape[0]
    Dm = D_MODEL
    tm = _tile(L, 256)
    r = tm // CV_HALO

    def body(a_ref, g_ref, ah_ref, gh_ref, w_ref, bdw_ref, lg_ref, lb_ref, s_ref, c_ref, ext_ref):
        i = pl.program_id(0)
        keep = (i > 0).astype(F32)
        ext_ref[0:CV_HALO, :] = ah_ref[...] * _sigmoid(gh_ref[...]) * keep
        ext_ref[CV_HALO:CV_HALO + tm, :] = a_ref[...] * _sigmoid(g_ref[...])
        c = _conv_from_ext(ext_ref, w_ref, bdw_ref[...], CONV_K, CV_HALO, tm)
        c_ref[...] = c
        mu = jnp.mean(c, axis=-1, keepdims=True)
        xc = c - mu
        rstd = lax.rsqrt(jnp.mean(xc * xc, axis=-1, keepdims=True) + LN_EPS)
        l = xc * rstd * lg_ref[...] + lb_ref[...]
        s_ref[...] = (l * _sigmoid(l)).astype(BF16)

    vec = pl.BlockSpec((1, Dm), lambda i: (0, 0))
    return pl.pallas_call(
        body, name=name, grid=(L // tm,),
        in_specs=[pl.BlockSpec((tm, Dm), lambda i: (i, 0)), pl.BlockSpec((tm, Dm), lambda i: (i, 1)),
                  pl.BlockSpec((CV_HALO, Dm), lambda i: (_prev_blk(i, r), 0)),
                  pl.BlockSpec((CV_HALO, Dm), lambda i: (_prev_blk(i, r), 1)),
                  pl.BlockSpec((CONV_K, Dm), lambda i: (0, 0)), vec, vec, vec],
        out_specs=[pl.BlockSpec((tm, Dm), lambda i: (i, 0)), pl.BlockSpec((tm, Dm), lambda i: (i, 0))],
        out_shape=[jax.ShapeDtypeStruct((L, Dm), BF16), jax.ShapeDtypeStruct((L, Dm), F32)],
        scratch_shapes=[pltpu.VMEM((CV_HALO + tm, Dm), F32)],
        compiler_params=_params(("parallel",)),
    )(u, u, u, u, w_dw, b_dw, ln_g, ln_b)


def _cv_bwd(ds, c, u, w_dw, ln_g, ln_b, *, name):
    L = u.shape[0]
    Dm = D_MODEL
    tm = _tile(L, 128)
    r = tm // CV_HALO
    nt = L // tm
    nblk = L // CV_HALO
    ext_rows = tm + CV_HALO

    def body(ds_ref, dsn_ref, c_ref, cn_ref, a_ref, g_ref, w_ref, lg_ref, lb_ref,
             du_ref, dw_ref, dbdw_ref, dlg_ref, dlb_ref, dbin_ref, dsx_ref, cx_ref, dext_ref):
        i = pl.program_id(0)

        @pl.when(i == 0)
        def _():
            for ref in (dw_ref, dbdw_ref, dlg_ref, dlb_ref, dbin_ref):
                ref[...] = jnp.zeros_like(ref)

        keep = (i < nt - 1).astype(F32)
        dsx_ref[0:tm, :] = ds_ref[...]
        dsx_ref[tm:ext_rows, :] = dsn_ref[...] * keep
        cx_ref[0:tm, :] = c_ref[...]
        cx_ref[tm:ext_rows, :] = cn_ref[...]
        cv = cx_ref[...]
        mu = jnp.mean(cv, axis=-1, keepdims=True)
        xc = cv - mu
        rstd = lax.rsqrt(jnp.mean(xc * xc, axis=-1, keepdims=True) + LN_EPS)
        nrm = xc * rstd
        lg = lg_ref[...]
        l = nrm * lg + lb_ref[...]
        dl = dsx_ref[...] * _dsilu(l, _sigmoid(l))
        dn = dl * lg
        dc = rstd * (dn - jnp.mean(dn, axis=-1, keepdims=True)
                     - nrm * jnp.mean(dn * nrm, axis=-1, keepdims=True))
        dext_ref[...] = dc
        dlg_ref[...] += jnp.sum((dl * nrm)[0:tm], axis=0, keepdims=True)
        dlb_ref[...] += jnp.sum(dl[0:tm], axis=0, keepdims=True)
        dbdw_ref[...] += jnp.sum(dc[0:tm], axis=0, keepdims=True)

        av, gv = a_ref[...], g_ref[...]
        sg = _sigmoid(gv)
        dv = _conv_back(dext_ref, av * sg, w_ref, dw_ref, CONV_K, tm)
        da = dv * sg
        dgate = dv * av * sg * (1.0 - sg)
        du_ref[:, 0:Dm] = da.astype(BF16)
        du_ref[:, Dm:2 * Dm] = dgate.astype(BF16)
        dbin_ref[:, 0:Dm] += jnp.sum(da, axis=0, keepdims=True)
        dbin_ref[:, Dm:2 * Dm] += jnp.sum(dgate, axis=0, keepdims=True)

    row = pl.BlockSpec((tm, Dm), lambda i: (i, 0))
    nxt = pl.BlockSpec((CV_HALO, Dm), lambda i: (_next_blk(i, r, nblk), 0))
    vec = pl.BlockSpec((1, Dm), lambda i: (0, 0))
    return pl.pallas_call(
        body, name=name, grid=(nt,),
        in_specs=[row, nxt, row, nxt, row, pl.BlockSpec((tm, Dm), lambda i: (i, 1)),
                  pl.BlockSpec((CONV_K, Dm), lambda i: (0, 0)), vec, vec],
        out_specs=[pl.BlockSpec((tm, 2 * Dm), lambda i: (i, 0)), pl.BlockSpec((CONV_K, Dm), lambda i: (0, 0)),
                   vec, vec, vec, pl.BlockSpec((1, 2 * Dm), lambda i: (0, 0))],
        out_shape=[jax.ShapeDtypeStruct((L, 2 * Dm), BF16), jax.ShapeDtypeStruct((CONV_K, Dm), F32),
                   jax.ShapeDtypeStruct((1, Dm), F32), jax.ShapeDtypeStruct((1, Dm), F32),
                   jax.ShapeDtypeStruct((1, Dm), F32), jax.ShapeDtypeStruct((1, 2 * Dm), F32)],
        scratch_shapes=[pltpu.VMEM((ext_rows, Dm), F32)] * 3,
        compiler_params=_params(("arbitrary",)),
    )(ds, ds, c, c, u, u, w_dw, ln_g, ln_b)


FFN_TC = 512
FFN_NJ = FFN_HIDDEN // FFN_TC
HALO8 = 8


def _ffn_mid_fwd(u0, w_dw, b_dw, *, name):
    L = u0.shape[1]
    tm = _tile(L, 256)
    r = tm // HALO8
    tc, nj = FFN_TC, FFN_NJ

    def body(u_ref, uh_ref, w_ref, b_ref, o_ref, ext_ref):
        keep = (pl.program_id(0) > 0).astype(F32)
        for h in range(2):
            ext = ext_ref.at[h]
            ext[0:HALO8, :] = uh_ref[h] * keep
            ext[HALO8:HALO8 + tm, :] = u_ref[h]
        for ls in _strips(tc):
            taps = [[w_ref[h, k:k + 1, ls] for k in range(FFN_K)] for h in range(2)]
            for lo, n in _row_blocks(tm):
                ug = _conv_strip(ext_ref.at[0], taps[0], b_ref[0, :, ls], HALO8, lo, n, ls)
                uv = _conv_strip(ext_ref.at[1], taps[1], b_ref[1, :, ls], HALO8, lo, n, ls)
                o_ref[lo:lo + n, ls] = (ug * _sigmoid(ug) * uv).astype(BF16)

    return pl.pallas_call(
        body, name=name, grid=(L // tm, nj),
        in_specs=[pl.BlockSpec((2, tm, tc), lambda i, j: (0, i, j)),
                  pl.BlockSpec((2, HALO8, tc), lambda i, j: (0, _prev_blk(i, r), j)),
                  pl.BlockSpec((2, FFN_K, tc), lambda i, j: (0, 0, j)), pl.BlockSpec((2, 1, tc), lambda i, j: (0, 0, j))],
        out_specs=pl.BlockSpec((tm, tc), lambda i, j: (i, j)),
        out_shape=jax.ShapeDtypeStruct((L, FFN_HIDDEN), BF16),
        scratch_shapes=[pltpu.VMEM((2, HALO8 + tm, tc), F32)],
        compiler_params=_params(("parallel", "parallel")),
    )(u0, u0, w_dw, b_dw)


def _ffn_bwd(dg, u0, w_dw, b_dw, *, name):
    L = u0.shape[1]
    tm = _tile(L, 256)
    r = tm // HALO8
    nt = L // tm
    nblk = L // HALO8
    tc, nj = FFN_TC, FFN_NJ
    rows = tm + HALO8

    def body(up_ref, uc_ref, un_ref, dg_ref, dgn_ref, w_ref, b_ref, du0_ref, dw_ref, db_ref, ext_ref, dext_ref):
        i = pl.program_id(1)

        @pl.when(i == 0)
        def _():
            dw_ref[...] = jnp.zeros_like(dw_ref)
            db_ref[...] = jnp.zeros_like(db_ref)

        keep_prev = (i > 0).astype(F32)
        keep_next = (i < nt - 1).astype(F32)
        for h in range(2):
            ext = ext_ref.at[h]
            ext[0:HALO8, :] = up_ref[h] * keep_prev
            ext[HALO8:HALO8 + tm, :] = uc_ref[h]
            ext[HALO8 + tm:HALO8 + rows, :] = un_ref[h]
        for ls in _strips(tc):
            taps = [[w_ref[h, k:k + 1, ls] for k in range(FFN_K)] for h in range(2)]
            dw_acc = [[jnp.zeros((1, STRIP), F32)] * FFN_K for _ in range(2)]
            db_acc = [jnp.zeros((1, STRIP), F32)] * 2
            for lo, n in _row_blocks(rows):
                ug = _conv_strip(ext_ref.at[0], taps[0], b_ref[0, :, ls], HALO8, lo, n, ls)
                uv = _conv_strip(ext_ref.at[1], taps[1], b_ref[1, :, ls], HALO8, lo, n, ls)
                dgx = dg_ref[lo:lo + n, ls] if lo < tm else dgn_ref[:, ls] * keep_next
                sg = _sigmoid(ug)
                for h, du in ((0, dgx * uv * _dsilu(ug, sg)), (1, dgx * (ug * sg))):
                    dext_ref[h, lo:lo + n, ls] = du
                    if lo < tm:
                        db_acc[h] = db_acc[h] + jnp.sum(du, axis=0, keepdims=True)
            for lo, n in _row_blocks(tm):
                for h in range(2):
                    dx = _conv_back_strip(dext_ref.at[h], uc_ref[h, lo:lo + n, ls], taps[h], dw_acc[h], lo, n, ls)
                    du0_ref[h, lo:lo + n, ls] = dx.astype(BF16)
            for h in range(2):
                db_ref[h, :, ls] += db_acc[h]
                for k in range(FFN_K):
                    dw_ref[h, k:k + 1, ls] += dw_acc[h][k]

    cur = pl.BlockSpec((2, tm, tc), lambda j, i: (0, i, j))
    prv = pl.BlockSpec((2, HALO8, tc), lambda j, i: (0, _prev_blk(i, r), j))
    nxt = pl.BlockSpec((2, HALO8, tc), lambda j, i: (0, _next_blk(i, r, nblk), j))
    wsp = pl.BlockSpec((2, FFN_K, tc), lambda j, i: (0, 0, j))
    bsp = pl.BlockSpec((2, 1, tc), lambda j, i: (0, 0, j))
    return pl.pallas_call(
        body, name=name, grid=(nj, nt),
        in_specs=[prv, cur, nxt, pl.BlockSpec((tm, tc), lambda j, i: (i, j)),
                  pl.BlockSpec((HALO8, tc), lambda j, i: (_next_blk(i, r, nblk), j)), wsp, bsp],
        out_specs=[cur, wsp, bsp],
        out_shape=[jax.ShapeDtypeStruct((2, L, FFN_HIDDEN), BF16), jax.ShapeDtypeStruct((2, FFN_K, FFN_HIDDEN), F32),
                   jax.ShapeDtypeStruct((2, 1, FFN_HIDDEN), F32)],
        scratch_shapes=[pltpu.VMEM((2, HALO8 + rows, tc), F32), pltpu.VMEM((2, rows, tc), F32)],
        compiler_params=_params(("parallel", "arbitrary")),
    )(u0, u0, u0, dg, dg, w_dw, b_dw)


def _ssm_unpad(zx_pad, *, name):
    L = zx_pad.shape[0]
    tm = _tile(L, 128)

    def body(p_ref, z_ref, xbc_ref, dt_ref, comp_ref):
        comp_ref[...] = jnp.zeros_like(comp_ref)
        for d in range(N_DEV):
            slab = p_ref[:, d * SSM_SLAB_PAD:(d + 1) * SSM_SLAB_PAD]
            if d:
                slab = pltpu.roll(slab, 8 * d, axis=1)
            comp_ref[:, 1280 * d:1280 * d + SSM_SLAB_PAD] += slab
        z_ref[...] = comp_ref[:, 0:SSM_INNER]
        xbc_ref[...] = comp_ref[:, SSM_INNER:SSM_INNER + SSM_CONV_DIM]
        dt_ref[...] = comp_ref[:, SSM_INNER + SSM_CONV_DIM:SSM_COMPACT]

    return pl.pallas_call(
        body, name=name, grid=(L // tm,),
        in_specs=[pl.BlockSpec((tm, N_DEV * SSM_SLAB_PAD), lambda i: (i, 0))],
        out_specs=[pl.BlockSpec((tm, SSM_INNER), lambda i: (i, 0)), pl.BlockSpec((tm, SSM_CONV_DIM), lambda i: (i, 0)),
                   pl.BlockSpec((tm, HEAD_LANES), lambda i: (i, 0))],
        out_shape=[jax.ShapeDtypeStruct((L, SSM_INNER), F32), jax.ShapeDtypeStruct((L, SSM_CONV_DIM), F32),
                   jax.ShapeDtypeStruct((L, HEAD_LANES), F32)],
        scratch_shapes=[pltpu.VMEM((tm, SSM_COMPACT), F32)],
        compiler_params=_params(("parallel",)),
    )(zx_pad)


def _ssm_pad(dz, dxbc, ddt, *, name):
    L = dz.shape[0]
    tm = _tile(L, 128)

    def body(dz_ref, dx_ref, dt_ref, p_ref, comp_ref):
        comp_ref[:, 0:SSM_INNER] = dz_ref[...]
        comp_ref[:, SSM_INNER:SSM_INNER + SSM_CONV_DIM] = dx_ref[...]
        lane = lax.broadcasted_iota(jnp.int32, (tm, HEAD_LANES), 1)
        comp_ref[:, SSM_INNER + SSM_CONV_DIM:SSM_COMPACT] = jnp.where(lane < SSM_HEADS, dt_ref[...], 0.0)
        col = lax.broadcasted_iota(jnp.int32, (tm, SSM_SLAB_PAD), 1)
        for d in range(N_DEV):
            win = comp_ref[:, 1280 * d:1280 * d + SSM_SLAB_PAD]
            if d:
                win = pltpu.roll(win, SSM_SLAB_PAD - 8 * d, axis=1)
            p_ref[:, d * SSM_SLAB_PAD:(d + 1) * SSM_SLAB_PAD] = jnp.where(col < SSM_SLAB, win, 0.0).astype(BF16)

    return pl.pallas_call(
        body, name=name, grid=(L // tm,),
        in_specs=[pl.BlockSpec((tm, SSM_INNER), lambda i: (i, 0)), pl.BlockSpec((tm, SSM_CONV_DIM), lambda i: (i, 0)),
                  pl.BlockSpec((tm, HEAD_LANES), lambda i: (i, 0))],
        out_specs=pl.BlockSpec((tm, N_DEV * SSM_SLAB_PAD), lambda i: (i, 0)),
        out_shape=jax.ShapeDtypeStruct((L, N_DEV * SSM_SLAB_PAD), BF16),
        scratch_shapes=[pltpu.VMEM((tm, SSM_COMPACT), F32)],
        compiler_params=_params(("parallel",)),
    )(dz, dxbc, ddt)


SSM_TC = 1024


def _ssm_conv_fwd(xpre, w, b, *, name):
    L, C = xpre.shape
    tm = _tile(L, 256)
    r = tm // HALO8
    tc = SSM_TC

    def body(x_ref, xh_ref, w_ref, b_ref, o_ref, ext_ref):
        keep = (pl.program_id(0) > 0).astype(F32)
        ext_ref[0:HALO8, :] = xh_ref[...] * keep
        ext_ref[HALO8:HALO8 + tm, :] = x_ref[...]
        for ls in _strips(tc):
            taps = [w_ref[k:k + 1, ls] for k in range(SSM_CONV_K)]
            for lo, n in _row_blocks(tm):
                pre = _conv_strip(ext_ref, taps, b_ref[:, ls], HALO8, lo, n, ls)
                o_ref[lo:lo + n, ls] = pre * _sigmoid(pre)

    return pl.pallas_call(
        body, name=name, grid=(L // tm, C // tc),
        in_specs=[pl.BlockSpec((tm, tc), lambda i, j: (i, j)), pl.BlockSpec((HALO8, tc), lambda i, j: (_prev_blk(i, r), j)),
                  pl.BlockSpec((SSM_CONV_K, tc), lambda i, j: (0, j)), pl.BlockSpec((1, tc), lambda i, j: (0, j))],
        out_specs=pl.BlockSpec((tm, tc), lambda i, j: (i, j)),
        out_shape=jax.ShapeDtypeStruct((L, C), F32),
        scratch_shapes=[pltpu.VMEM((HALO8 + tm, tc), F32)],
        compiler_params=_params(("parallel", "parallel")),
    )(xpre, xpre, w, b)


def _ssm_conv_bwd(dy, xpre, w, b, *, name):
    L, C = xpre.shape
    tm = _tile(L, 256)
    r = tm // HALO8
    nt = L // tm
    nblk = L // HALO8
    tc = SSM_TC
    rows = tm + HALO8

    def body(xp_ref, xc_ref, xn_ref, dy_ref, dyn_ref, w_ref, b_ref, dx_ref, dw_ref, db_ref, ext_ref, dext_ref):
        i = pl.program_id(1)

        @pl.when(i == 0)
        def _():
            dw_ref[...] = jnp.zeros_like(dw_ref)
            db_ref[...] = jnp.zeros_like(db_ref)

        ext_ref[0:HALO8, :] = xp_ref[...] * (i > 0).astype(F32)
        ext_ref[HALO8:HALO8 + tm, :] = xc_ref[...]
        ext_ref[HALO8 + tm:HALO8 + rows, :] = xn_ref[...]
        keep_next = (i < nt - 1).astype(F32)
        for ls in _strips(tc):
            taps = [w_ref[k:k + 1, ls] for k in range(SSM_CONV_K)]
            dw_acc = [jnp.zeros((1, STRIP), F32)] * SSM_CONV_K
            db_acc = jnp.zeros((1, STRIP), F32)
            for lo, n in _row_blocks(rows):
                pre = _conv_strip(ext_ref, taps, b_ref[:, ls], HALO8, lo, n, ls)
                dyx = dy_ref[lo:lo + n, ls] if lo < tm else dyn_ref[:, ls] * keep_next
                dpre = dyx * _dsilu(pre, _sigmoid(pre))
                dext_ref[lo:lo + n, ls] = dpre
                if lo < tm:
                    db_acc = db_acc + jnp.sum(dpre, axis=0, keepdims=True)
            for lo, n in _row_blocks(tm):
                dx_ref[lo:lo + n, ls] = _conv_back_strip(dext_ref, xc_ref[lo:lo + n, ls], taps, dw_acc, lo, n, ls)
            db_ref[:, ls] += db_acc
            for k in range(SSM_CONV_K):
                dw_ref[k:k + 1, ls] += dw_acc[k]

    cur = pl.BlockSpec((tm, tc), lambda j, i: (i, j))
    prv = pl.BlockSpec((HALO8, tc), lambda j, i: (_prev_blk(i, r), j))
    nxt = pl.BlockSpec((HALO8, tc), lambda j, i: (_next_blk(i, r, nblk), j))
    return pl.pallas_call(
        body, name=name, grid=(C // tc, nt),
        in_specs=[prv, cur, nxt, cur, nxt, pl.BlockSpec((SSM_CONV_K, tc), lambda j, i: (0, j)),
                  pl.BlockSpec((1, tc), lambda j, i: (0, j))],
        out_specs=[cur, pl.BlockSpec((SSM_CONV_K, tc), lambda j, i: (0, j)), pl.BlockSpec((1, tc), lambda j, i: (0, j))],
        out_shape=[jax.ShapeDtypeStruct((L, C), F32), jax.ShapeDtypeStruct((SSM_CONV_K, C), F32),
                   jax.ShapeDtypeStruct((1, C), F32)],
        scratch_shapes=[pltpu.VMEM((HALO8 + rows, tc), F32), pltpu.VMEM((rows, tc), F32)],
        compiler_params=_params(("parallel", "arbitrary")),
    )(xpre, xpre, xpre, dy, dy, w, b)


def _gnorm_fwd(y, z, g, *, name):
    L, C = y.shape
    tm = _tile(L, 256)
    gw = SSM_GROUP_W

    def body(y_ref, z_ref, g_ref, o_ref):
        for k in range(SSM_GROUPS):
            sl = slice(k * gw, (k + 1) * gw)
            zv = z_ref[:, sl]
            yz = y_ref[:, sl] * zv * _sigmoid(zv)
            r = lax.rsqrt(jnp.mean(yz * yz, axis=-1, keepdims=True) + RMS_EPS)
            o_ref[:, sl] = (yz * r * g_ref[:, sl]).astype(BF16)

    row = pl.BlockSpec((tm, C), lambda i: (i, 0))
    return pl.pallas_call(
        body, name=name, grid=(L // tm,),
        in_specs=[row, row, pl.BlockSpec((1, C), lambda i: (0, 0))], out_specs=row,
        out_shape=jax.ShapeDtypeStruct((L, C), BF16),
        compiler_params=_params(("parallel",)),
    )(y, z, g)


def _gnorm_bwd(dn, y, z, g, *, name):
    L, C = y.shape
    tm = _tile(L, 256)
    gw = SSM_GROUP_W

    def body(dn_ref, y_ref, z_ref, g_ref, dy_ref, dz_ref, dg_ref):
        @pl.when(pl.program_id(0) == 0)
        def _():
            dg_ref[...] = jnp.zeros_like(dg_ref)

        for k in range(SSM_GROUPS):
            sl = slice(k * gw, (k + 1) * gw)
            zv, yv = z_ref[:, sl], y_ref[:, sl]
            sz = _sigmoid(zv)
            silu = zv * sz
            yz = yv * silu
            r = lax.rsqrt(jnp.mean(yz * yz, axis=-1, keepdims=True) + RMS_EPS)
            nrm = yz * r
            dnv = dn_ref[:, sl]
            dg_ref[:, sl] += jnp.sum(dnv * nrm, axis=0, keepdims=True)
            dh = dnv * g_ref[:, sl]
            dyz = r * (dh - nrm * jnp.mean(dh * nrm, axis=-1, keepdims=True))
            dy_ref[:, sl] = dyz * silu
            dz_ref[:, sl] = dyz * yv * _dsilu(zv, sz)

    row = pl.BlockSpec((tm, C), lambda i: (i, 0))
    vec = pl.BlockSpec((1, C), lambda i: (0, 0))
    return pl.pallas_call(
        body, name=name, grid=(L // tm,),
        in_specs=[row, row, row, vec], out_specs=[row, row, vec],
        out_shape=[jax.ShapeDtypeStruct((L, C), F32), jax.ShapeDtypeStruct((L, C), F32),
                   jax.ShapeDtypeStruct((1, C), F32)],
        compiler_params=_params(("arbitrary",)),
    )(dn, y, z, g)


def _ssd_consts():
    q, gw = CHUNK, SSM_GROUP_W
    sub = lax.broadcasted_iota(jnp.int32, (q, gw), 0)
    lane_in = lax.broadcasted_iota(jnp.int32, (q, gw), 1) % q
    sub2 = lax.broadcasted_iota(jnp.int32, (gw, gw), 0) // q
    lane2 = lax.broadcasted_iota(jnp.int32, (gw, gw), 1) // q
    t = lax.broadcasted_iota(jnp.int32, (q, q), 0)
    u = lax.broadcasted_iota(jnp.int32, (q, q), 1)
    return dict(
        diag=(lane_in == sub), low=(lane_in <= sub), up=(sub <= lane_in), block=(sub2 == lane2),
        tri=(u <= t).astype(BF16), trit=(u >= t).astype(BF16), ones=jnp.ones((q, q), BF16),
        last=(lax.broadcasted_iota(jnp.int32, (q, HEAD_LANES), 0) == q - 1))


def _ssd_chunk_terms(dt_raw, bias, a_log, e, et, k):
    dt = _softplus(dt_raw + bias)
    a_neg = -jnp.exp(a_log)
    cs = _dot3_r(k["tri"], dt * a_neg)
    cs_last = cs[CHUNK - 1:CHUNK, :]
    ecs = jnp.exp(cs)
    dte = jnp.exp(cs_last - cs)
    ecl_hn = _dot3_l(ecs, k["last"].astype(BF16), "tn")
    rows = _dot3_r(et, ecl_hn, parts=2)
    return dict(dt=dt, a_neg=a_neg, cs=cs, ecs=ecs, dte=dte, rows=rows,
                dtx=_dot3_l(dt, e, parts=1), csx=_dot3_l(cs, e, parts=2), ecsx=_dot3_l(ecs, e, parts=1),
                dtex=_dot3_l(dte, e, parts=1))


def _tile8(x):
    return jnp.concatenate([x] * 8, axis=0)


def _ssd_fwd(xbc, dt_raw, dt_bias, a_log, d_x, e, et, *, name):
    L = xbc.shape[0]
    nc = L // CHUNK
    q, gw, ns = CHUNK, SSM_GROUP_W, SSM_STATE

    def body(xbc_ref, dt_ref, bias_ref, alog_ref, dx_ref, e_ref, et_ref, y_ref, st_ref, s_ref):
        @pl.when(pl.program_id(0) == 0)
        def _():
            s_ref[...] = jnp.zeros_like(s_ref)

        k = _ssd_consts()
        t = _ssd_chunk_terms(dt_ref[...], bias_ref[...], alog_ref[...], e_ref[...], et_ref[...], k)
        st_ref[...] = s_ref[...]
        for g in range(SSM_GROUPS):
            ch = slice(g * gw, (g + 1) * gw)
            xs = xbc_ref[:, ch]
            bm = xbc_ref[:, SSM_INNER + g * ns:SSM_INNER + (g + 1) * ns].astype(BF16)
            cm = xbc_ref[:, SSM_INNER + (SSM_GROUPS + g) * ns:SSM_INNER + (SSM_GROUPS + g + 1) * ns].astype(BF16)
            xd = xs * t["dtx"][:, ch]
            csx = t["csx"][:, ch]
            csrow = _dot3_r(k["ones"], jnp.where(k["diag"], csx, 0.0), parts=2)
            lcat = jnp.where(k["low"], jnp.exp(jnp.minimum(csx - csrow, 0.0)), 0.0)
            mcat = _dot(cm, _tile8(bm), "nt") * lcat
            xdbd = jnp.where(k["block"], _tile8(xd), 0.0).astype(BF16)
            sg = s_ref[ch, :]
            y = _dot(mcat.astype(BF16), xdbd)
            y = y + _dot(cm, sg.astype(BF16), "nt") * t["ecsx"][:, ch]
            y_ref[:, ch] = y + dx_ref[:, ch] * xs
            s_ref[ch, :] = sg * t["rows"][ch, :] + _dot((xd * t["dtex"][:, ch]).astype(BF16), bm, "tn")

    hv = pl.BlockSpec((1, HEAD_LANES), lambda c: (0, 0))
    return pl.pallas_call(
        body, name=name, grid=(nc,),
        in_specs=[pl.BlockSpec((q, SSM_CONV_DIM), lambda c: (c, 0)), pl.BlockSpec((q, HEAD_LANES), lambda c: (c, 0)),
                  hv, hv, pl.BlockSpec((1, SSM_INNER), lambda c: (0, 0)),
                  pl.BlockSpec((HEAD_LANES, SSM_INNER), lambda c: (0, 0)),
                  pl.BlockSpec((SSM_INNER, HEAD_LANES), lambda c: (0, 0))],
        out_specs=[pl.BlockSpec((q, SSM_INNER), lambda c: (c, 0)),
                   pl.BlockSpec((None, SSM_INNER, ns), lambda c: (c, 0, 0))],
        out_shape=[jax.ShapeDtypeStruct((L, SSM_INNER), F32), jax.ShapeDtypeStruct((nc, SSM_INNER, ns), F32)],
        scratch_shapes=[pltpu.VMEM((SSM_INNER, ns), F32)],
        compiler_params=_params(("arbitrary",)),
    )(xbc, dt_raw, dt_bias, a_log, d_x, e, et)


def _ssd_bwd(dy, xbc, dt_raw, states, dt_bias, a_log, d_x, e, et, *, name):
    L = xbc.shape[0]
    nc = L // CHUNK
    q, gw, ns = CHUNK, SSM_GROUP_W, SSM_STATE

    def body(dy_ref, xbc_ref, dt_ref, st_ref, bias_ref, alog_ref, dx_ref, e_ref, et_ref,
             dxbc_ref, ddt_ref, dbias_ref, dalog_ref, dd_ref, ds_ref, f1_ref, f2_ref, f3_ref, f4_ref, fs_ref):
        @pl.when(pl.program_id(0) == 0)
        def _():
            ds_ref[...] = jnp.zeros_like(ds_ref)
            for ref in (dbias_ref, dalog_ref, dd_ref):
                ref[...] = jnp.zeros_like(ref)

        k = _ssd_consts()
        ev = e_ref[...]
        t = _ssd_chunk_terms(dt_ref[...], bias_ref[...], alog_ref[...], ev, et_ref[...], k)
        ones8 = jnp.ones((8, ns), BF16)
        for g in range(SSM_GROUPS):
            ch = slice(g * gw, (g + 1) * gw)
            bsl = slice(SSM_INNER + g * ns, SSM_INNER + (g + 1) * ns)
            csl = slice(SSM_INNER + (SSM_GROUPS + g) * ns, SSM_INNER + (SSM_GROUPS + g + 1) * ns)
            xs = xbc_ref[:, ch]
            bm = xbc_ref[:, bsl].astype(BF16)
            cm = xbc_ref[:, csl].astype(BF16)
            dyv = dy_ref[:, ch]
            dtx, ecsx, dtex, csx = t["dtx"][:, ch], t["ecsx"][:, ch], t["dtex"][:, ch], t["csx"][:, ch]
            xd = xs * dtx
            csrow = _dot3_r(k["ones"], jnp.where(k["diag"], csx, 0.0), parts=2)
            lcat = jnp.where(k["low"], jnp.exp(jnp.minimum(csx - csrow, 0.0)), 0.0)
            ltcat = jnp.where(k["up"], jnp.exp(jnp.minimum(csrow - csx, 0.0)), 0.0)
            mcat = _dot(cm, _tile8(bm), "nt") * lcat
            mtcat = _dot(bm, _tile8(cm), "nt") * ltcat
            xdbd = jnp.where(k["block"], _tile8(xd), 0.0).astype(BF16)
            dybd = jnp.where(k["block"], _tile8(dyv), 0.0).astype(BF16)
            gq = _dot(dyv.astype(BF16), xdbd, "nt") * mcat
            gt = _dot(xd.astype(BF16), dybd, "nt")
            gqt = gt * mtcat
            dcbt = _dot((gt * ltcat).astype(BF16), jnp.where(k["diag"], 1.0, 0.0).astype(BF16), "nt")
            dcbt = dcbt.astype(BF16)
            sg = st_ref[ch, :]
            dsg = ds_ref[ch, :]
            sgb, dsgb = sg.astype(BF16), dsg.astype(BF16)
            yoff = _dot(cm, sgb, "nt") * ecsx
            dye = (dyv * ecsx).astype(BF16)
            xdd = xd * dtex
            dxbc_ref[:, csl] = _dot(dcbt, bm, "tn") + _dot(dye, sgb)
            dxbc_ref[:, bsl] = _dot(dcbt, cm) + _dot(xdd.astype(BF16), dsgb)
            bds = _dot(bm, dsgb, "nt")
            dxd = _dot(mtcat.astype(BF16), dybd) + dtex * bds
            wx = xdd * bds
            ds_ref[ch, :] = t["rows"][ch, :] * dsg + _dot(dye, cm, "tn")
            f1_ref[:, ch] = gq - gqt + dyv * yoff - wx
            f2_ref[:, ch] = wx
            f3_ref[:, ch] = dxd * xs
            f4_ref[:, ch] = dyv * xs
            dxbc_ref[:, ch] = dxd * dtx + dx_ref[:, ch] * dyv
            fs_ref[:, ch] = _dot3_r(ones8, dsg * sg, "nt", parts=2)
        fold = lambda v, parts: _dot3_l(v, ev, "nt", parts=parts)
        f2 = fold(f2_ref[...], 1)
        last_row = jnp.sum(f2, axis=0, keepdims=True) + t["ecs"][q - 1:q, :] * fold(fs_ref[...], 2)[0:1, :]
        dcs = fold(f1_ref[...], 2) + jnp.where(k["last"], last_row, 0.0)
        da = _dot3_r(k["trit"], dcs)
        ddt = da * t["a_neg"] + fold(f3_ref[...], 2)
        ddt_raw = ddt * _sigmoid(dt_ref[...] + bias_ref[...])
        ddt_ref[...] = ddt_raw
        dbias_ref[...] += jnp.sum(ddt_raw, axis=0, keepdims=True)
        dalog_ref[...] += jnp.sum(da * t["dt"], axis=0, keepdims=True) * t["a_neg"]
        dd_ref[...] += jnp.sum(fold(f4_ref[...], 1), axis=0, keepdims=True)

    rev = lambda c: (nc - 1 - c, 0)
    hv = pl.BlockSpec((1, HEAD_LANES), lambda c: (0, 0))
    return pl.pallas_call(
        body, name=name, grid=(nc,),
        in_specs=[pl.BlockSpec((q, SSM_INNER), rev), pl.BlockSpec((q, SSM_CONV_DIM), rev),
                  pl.BlockSpec((q, HEAD_LANES), rev),
                  pl.BlockSpec((None, SSM_INNER, ns), lambda c: (nc - 1 - c, 0, 0)),
                  hv, hv, pl.BlockSpec((1, SSM_INNER), lambda c: (0, 0)),
                  pl.BlockSpec((HEAD_LANES, SSM_INNER), lambda c: (0, 0)),
                  pl.BlockSpec((SSM_INNER, HEAD_LANES), lambda c: (0, 0))],
        out_specs=[pl.BlockSpec((q, SSM_CONV_DIM), rev), pl.BlockSpec((q, HEAD_LANES), rev), hv, hv, hv],
        out_shape=[jax.ShapeDtypeStruct((L, SSM_CONV_DIM), F32), jax.ShapeDtypeStruct((L, HEAD_LANES), F32)]
        + [jax.ShapeDtypeStruct((1, HEAD_LANES), F32)] * 3,
        scratch_shapes=[pltpu.VMEM((SSM_INNER, ns), F32)] + [pltpu.VMEM((q, SSM_INNER), F32)] * 4
        + [pltpu.VMEM((8, SSM_INNER), F32)],
        compiler_params=_params(("arbitrary",)),
    )(dy, xbc, dt_raw, states, dt_bias, a_log, d_x, e, et)


def _my_index():
    return 4 * lax.axis_index("x") + 2 * lax.axis_index("y") + lax.axis_index("c")


def _peer(k):
    return (lax.axis_index("x") ^ ((k >> 2) & 1), lax.axis_index("y") ^ ((k >> 1) & 1), lax.axis_index("c") ^ (k & 1))


def _all_gather(shard, *, by_rows, name):
    nl, a, b = shard.shape
    out_shape = (nl, N_DEV * a, b) if by_rows else (nl, N_DEV, a, b)

    def body(src_ref, out_ref, send_sems, recv_sems, local_sems):
        me = _my_index()

        def mine(j):
            if by_rows:
                return out_ref.at[j, pl.ds(pl.multiple_of(me * a, 16), a), :]
            return out_ref.at[j, me]

        local = [pltpu.make_async_copy(src_ref.at[j], mine(j), local_sems.at[j]) for j in range(nl)]
        for cp in local:
            cp.start()
        copies = []
        for j in range(nl):
            for k in range(1, N_DEV):
                cp = pltpu.make_async_remote_copy(
                    src_ref=src_ref.at[j], dst_ref=mine(j), send_sem=send_sems.at[j, k - 1],
                    recv_sem=recv_sems.at[j, k - 1], device_id=_peer(k), device_id_type=MESH_ID)
                cp.start()
                copies.append(cp)
        for cp in copies:
            cp.wait()
        for cp in local:
            cp.wait()

    return pl.pallas_call(
        body, name=name, in_specs=[_HBM], out_specs=_HBM,
        out_shape=jax.ShapeDtypeStruct(out_shape, shard.dtype),
        scratch_shapes=[pltpu.SemaphoreType.DMA((nl, N_DEV - 1)), pltpu.SemaphoreType.DMA((nl, N_DEV - 1)),
                        pltpu.SemaphoreType.DMA((nl,))],
    )(shard)


_EFFECT = pltpu.SideEffectType.DATAFLOW_SIDE_EFFECTING


def _blk(ref, i, rows):
    if len(ref.shape) == 3:
        return ref.at[i]
    return ref.at[pl.ds(pl.multiple_of(i * rows, 16), rows), :]


def _remote_copies(scatter, src_ref, land_ref, send_sems, recv_sems, idx):
    me = _my_index()
    out = []
    for k in range(1, N_DEV):
        if scatter:
            src, dst = _blk(src_ref, me ^ k, land_ref.shape[1]), land_ref.at[me]
        else:
            src, dst = src_ref, _blk(land_ref, me, src_ref.shape[0])
        sem = idx * (N_DEV - 1) + k - 1
        out.append(pltpu.make_async_remote_copy(
            src_ref=src, dst_ref=dst, send_sem=send_sems.at[sem], recv_sem=recv_sems.at[sem],
            device_id=_peer(k), device_id_type=MESH_ID))
    return out


def _own_copy(scatter, src_ref, land_ref, own_sems, idx):
    me = _my_index()
    if scatter:
        src, dst = _blk(src_ref, me, land_ref.shape[1]), land_ref.at[me]
    else:
        src, dst = src_ref, _blk(land_ref, me, src_ref.shape[0])
    return pltpu.make_async_copy(src, dst, own_sems.at[idx])


def _copies_start(srcs, land_shapes, *, scatter, after, name):
    n = len(srcs)
    n_after = len(after)

    def body(*refs):
        src_refs, land_refs = refs[:n], refs[n:2 * n]
        send_sems, recv_sems, own_sems = refs[2 * n + n_after:2 * n + n_after + 3]
        for i in range(n):
            for cp in _remote_copies(scatter, src_refs[i], land_refs[i], send_sems, recv_sems, i):
                cp.start()
            _own_copy(scatter, src_refs[i], land_refs[i], own_sems, i).start()
        refs[-1][...] = jnp.zeros_like(refs[-1])

    lands = [lax.empty(shp, t.dtype) for shp, t in zip(land_shapes, srcs)]
    arrays = [pltpu.with_memory_space_constraint(t, pltpu.HBM) for t in list(srcs) + lands]
    sems = pltpu.SemaphoreType.DMA((n * (N_DEV - 1),))
    res = pl.pallas_call(
        body, name=name,
        out_shape=(sems, sems, pltpu.SemaphoreType.DMA((n,)), *[pltpu.HBM(t.shape, t.dtype) for t in arrays],
                   jax.ShapeDtypeStruct((8, 128), F32)),
        in_specs=[_HBM] * (2 * n) + [_ANY] * n_after,
        out_specs=(_SEM, _SEM, _SEM, *[_HBM] * (2 * n), pl.BlockSpec(memory_space=pltpu.VMEM)),
        input_output_aliases={i: 3 + i for i in range(2 * n)},
        compiler_params=pltpu.CompilerParams(has_side_effects=_EFFECT),
    )(*arrays, *after)
    return (res[:3], res[3:3 + n], res[3 + n:3 + 2 * n]), res[-1]


def _copies_wait(started, *, scatter, after, name):
    sems, srcs, lands = started
    n = len(srcs)

    def body(*refs):
        src_refs, land_refs = refs[:n], refs[n:2 * n]
        s_sems, r_sems, o_sems = refs[2 * n:2 * n + 3]
        for i in range(n):
            for cp in _remote_copies(scatter, src_refs[i], land_refs[i], s_sems, r_sems, i):
                cp.wait_send()
                cp.wait_recv()
            _own_copy(scatter, src_refs[i], land_refs[i], o_sems, i).wait()

    arrays = list(srcs) + list(lands)
    res = pl.pallas_call(
        body, name=name,
        out_shape=tuple(pltpu.HBM(t.shape, t.dtype) for t in arrays),
        in_specs=[_HBM] * (2 * n) + [_SEM] * 3 + [_ANY] * len(after), out_specs=tuple([_HBM] * (2 * n)),
        input_output_aliases={i: i for i in range(2 * n)},
        compiler_params=pltpu.CompilerParams(has_side_effects=_EFFECT),
    )(*arrays, *sems, *after)
    return res[n:]


def _adamw_math(w, g, m, v):
    m = ADAM_B1 * m + (1.0 - ADAM_B1) * g
    v = ADAM_B2 * v + (1.0 - ADAM_B2) * (g * g)
    m_hat = m / (1.0 - ADAM_B1 ** ADAM_STEP)
    v_hat = v / (1.0 - ADAM_B2 ** ADAM_STEP)
    delta = -ADAM_LR * (m_hat / (jnp.sqrt(v_hat) + ADAM_EPS) + ADAM_WD * w)
    return delta, m, v


def _adamw(recvs, w, m, v, *, name):
    nl, R, C = w.shape
    tr = max(t for t in range(8, (128 if C > 1024 else 512) + 1, 8) if R % t == 0)

    def body(*refs):
        r_refs = refs[:nl]
        w_ref, m_ref, v_ref, g_ref, d_ref, nm_ref, nv_ref = refs[nl:]
        for layer in range(nl):
            @pl.when(pl.program_id(0) == layer)
            def _(r_ref=r_refs[layer]):
                g = r_ref[0].astype(F32)
                for s in range(1, N_DEV):
                    g = g + r_ref[s].astype(F32)
                delta, nm, nv = _adamw_math(w_ref[...], g, m_ref[...], v_ref[...])
                g_ref[...] = g
                d_ref[...] = delta
                nm_ref[...] = nm
                nv_ref[...] = nv

    def recv_spec(layer):
        return pl.BlockSpec((N_DEV, tr, C), lambda j, i: (0, jnp.where(j == layer, i, 0), 0))

    row = pl.BlockSpec((None, tr, C), lambda j, i: (j, i, 0))
    return pl.pallas_call(
        body, name=name, grid=(nl, R // tr),
        in_specs=[recv_spec(layer) for layer in range(nl)] + [row, row, row],
        out_specs=[row] * 4, out_shape=[jax.ShapeDtypeStruct((nl, R, C), F32)] * 4,
        compiler_params=_params(("parallel", "parallel")),
    )(*recvs, w, m, v)


def _sum_slots(g8, *, name):
    _, P, C = g8.shape

    def body(r_ref, o_ref):
        g = r_ref[0]
        for s in range(1, N_DEV):
            g = g + r_ref[s]
        o_ref[...] = g

    return pl.pallas_call(
        body, name=name, grid=(1,),
        in_specs=[pl.BlockSpec((N_DEV, P, C), lambda i: (0, 0, 0))],
        out_specs=pl.BlockSpec((P, C), lambda i: (0, 0)),
        out_shape=jax.ShapeDtypeStruct((P, C), F32),
        compiler_params=_params(("arbitrary",)),
    )(g8)


def _adamw_small(g, w, m, v, *, name):
    P, C = w.shape

    def body(g_ref, w_ref, m_ref, v_ref, d_ref, nm_ref, nv_ref):
        delta, nm, nv = _adamw_math(w_ref[...], g_ref[...], m_ref[...], v_ref[...])
        d_ref[...] = delta
        nm_ref[...] = nm
        nv_ref[...] = nv

    full = pl.BlockSpec((P, C), lambda i: (0, 0))
    return pl.pallas_call(
        body, name=name, grid=(1,), in_specs=[full] * 4, out_specs=[full] * 3,
        out_shape=[jax.ShapeDtypeStruct((P, C), F32)] * 3,
        compiler_params=_params(("arbitrary",)),
    )(g, w, m, v)


def _pack(arrays):
    flat = jnp.concatenate([a.reshape(-1) for a in arrays])
    pad = (-flat.shape[0]) % (8 * 128)
    return jnp.pad(flat, (0, pad)).reshape(-1, 128)


def _unpack(buf, shapes):
    flat = buf.reshape(-1)
    out, off = [], 0
    for shp in shapes:
        n = 1
        for s in shp:
            n *= s
        out.append(flat[off:off + n].reshape(shp))
        off += n
    return out


def _expand_matrices():
    h = lax.broadcasted_iota(jnp.int32, (HEAD_LANES, SSM_INNER), 0)
    col = lax.broadcasted_iota(jnp.int32, (HEAD_LANES, SSM_INNER), 1) // SSM_HEAD_DIM
    e = (h == col).astype(BF16)
    return e, e.T


def _pad_heads(v):
    return jnp.pad(v.reshape(1, -1), ((0, 0), (0, HEAD_LANES - v.shape[-1])))


SHARDED_SMALL = ("cv_w_dw", "ssm_w_conv", "ssm_b_conv", "ssm_norm_g", "ffn_w_dw")
REPLICATED_SMALL = ("norm_mix_g", "norm_ffn_g", "norm_final_g", "cv_b_in", "cv_b_dw", "cv_ln_g", "cv_ln_b",
                    "cv_b_out", "ssm_dt_bias", "ssm_a_log", "ssm_d", "ffn_b_dw")
SMALL = REPLICATED_SMALL + SHARDED_SMALL
BIG_COLS = ("cv_w_in", "ssm_w_in", "ffn_w_up")
BIG_ROWS = ("cv_w_out", "ssm_w_out", "ffn_w_down")
WEIGHTS = ("norm_mix_g", "norm_ffn_g", "norm_final_g", "cv_w_in", "cv_b_in", "cv_w_dw", "cv_b_dw", "cv_ln_g",
           "cv_ln_b", "cv_w_out", "cv_b_out", "ssm_w_in", "ssm_w_conv", "ssm_b_conv", "ssm_dt_bias", "ssm_a_log",
           "ssm_d", "ssm_norm_g", "ssm_w_out", "ffn_w_up", "ffn_w_dw", "ffn_b_dw", "ffn_w_down")


N_STAGES = 2 * DEPTH


def _stage_weights(si):
    i = si // 2
    if si % 2:
        return "ffn_w_up", "ffn_w_down", i
    return ("cv_w_in", "cv_w_out", i // 2) if i % 2 == 0 else ("ssm_w_in", "ssm_w_out", i // 2)


def _gather_small_weights(w):
    small_local = _pack([w[n] for n in SHARDED_SMALL])
    small8 = _all_gather(small_local[None], by_rows=False, name="gather_small")[0]
    full = {}
    per_dev = [_unpack(small8[d], [w[n].shape for n in SHARDED_SMALL]) for d in range(N_DEV)]
    for idx, n in enumerate(SHARDED_SMALL):
        full[n] = jnp.concatenate([per_dev[d][idx] for d in range(N_DEV)], axis=-1)
    for n in REPLICATED_SMALL:
        full[n] = w[n]
    return full


def _forward_backward(x, target, weights_of, full, emit):
    e, et = _expand_matrices()
    grads = {}

    saved = []
    for i in range(DEPTH):
        j = i // 2
        s = {"x_in": x}
        wc, wr, behind = weights_of(2 * i, x)
        h1 = _rms_fwd(x, full["norm_mix_g"][i][None], name=f"rms_mix_{i}")
        s["h1"] = h1
        if i % 2 == 0:
            u = _mm_cols_nn(h1, wc, bias=full["cv_b_in"][j][None], after=behind, name=f"cv_in_{i}")
            sact, c = _cv_mid_fwd(u, full["cv_w_dw"][j], full["cv_b_dw"][j][None], full["cv_ln_g"][j][None],
                                  full["cv_ln_b"][j][None], name=f"cv_mid_{i}")
            x = _mm_rows_nn(sact, wr, res=x, bias=full["cv_b_out"][j][None], name=f"cv_out_{i}")
            s.update(u=u, c=c, sact=sact)
        else:
            zx = _mm_cols_nn(h1, wc, after=behind, name=f"ssm_in_{i}")
            z, xpre, dt_raw = _ssm_unpad(zx, name=f"ssm_unpad_{i}")
            xbc = _ssm_conv_fwd(xpre, full["ssm_w_conv"][j], full["ssm_b_conv"][j][None], name=f"ssm_conv_{i}")
            hp = dict(dt_bias=_pad_heads(full["ssm_dt_bias"][j]), a_log=_pad_heads(full["ssm_a_log"][j]),
                      d_x=jnp.repeat(full["ssm_d"][j], SSM_HEAD_DIM)[None])
            y, states = _ssd_fwd(xbc, dt_raw, hp["dt_bias"], hp["a_log"], hp["d_x"], e, et, name=f"ssd_fwd_{i}")
            yn = _gnorm_fwd(y, z, full["ssm_norm_g"][j][None], name=f"gnorm_{i}")
            x = _mm_rows_nn(yn, wr, res=x, name=f"ssm_out_{i}")
            s.update(z=z, xpre=xpre, dt_raw=dt_raw, xbc=xbc, y=y, states=states, yn=yn, hp=hp)
        s["x_mid"] = x
        wc2, wr2, behind = weights_of(2 * i + 1, x)
        h2 = _rms_fwd(x, full["norm_ffn_g"][i][None], name=f"rms_ffn_{i}")
        u0 = _mm_cols_nn(h2, wc2, after=behind, halves=True, name=f"ffn_up_{i}")
        ffn_w = full["ffn_w_dw"][i].reshape(FFN_K, 2, FFN_HIDDEN).transpose(1, 0, 2)
        ffn_b = full["ffn_b_dw"][i].reshape(2, 1, FFN_HIDDEN)
        gact = _ffn_mid_fwd(u0, ffn_w, ffn_b, name=f"ffn_mid_{i}")
        x = _mm_rows_nn(gact, wr2, res=x, name=f"ffn_down_{i}")
        s.update(h2=h2, u0=u0, gact=gact, ffn_w=ffn_w, ffn_b=ffn_b, weights=(wc, wr, wc2, wr2))
        saved.append(s)

    dx, dxb, dg_final, loss = _final_loss(x, full["norm_final_g"][None], target, name="final_loss")
    grads["norm_final_g"] = dg_final[0]

    small_layers = {n: [None] * full[n].shape[0] for n in SMALL if n != "norm_final_g"}
    behind = None
    for i in reversed(range(DEPTH)):
        j = i // 2
        s = saved[i]
        wc, wr, wc2, wr2 = s["weights"]
        dgact = _mm_rows_nt(dxb, wr2, after=behind, name=f"ffn_down_dx_{i}")
        dwr2 = _mm_rows_tn(s["gact"], dxb, name=f"ffn_down_dw_{i}")
        du0, dw_dw, db_dw = _ffn_bwd(dgact, s["u0"], s["ffn_w"], s["ffn_b"], name=f"ffn_bwd_{i}")
        small_layers["ffn_w_dw"][i] = dw_dw.transpose(1, 0, 2).reshape(FFN_K, 2 * FFN_HIDDEN)
        small_layers["ffn_b_dw"][i] = db_dw.reshape(2 * FFN_HIDDEN)
        dh2 = _mm_cols_nt(du0, wc2, name=f"ffn_up_dx_{i}")
        dwc2 = _mm_cols_tn(s["h2"], du0, name=f"ffn_up_dw_{i}")
        dx, dxb, dg, colsum = _rms_bwd(s["x_mid"], full["norm_ffn_g"][i][None], dh2, dx, name=f"rms_ffn_bwd_{i}")
        small_layers["norm_ffn_g"][i] = dg[0]
        behind = emit(2 * i + 1, dwc2, dwr2)
        if i % 2 == 0:
            small_layers["cv_b_out"][j] = colsum[0]
            dsact = _mm_rows_nt(dxb, wr, after=behind, name=f"cv_out_dx_{i}")
            dwr = _mm_rows_tn(s["sact"], dxb, name=f"cv_out_dw_{i}")
            du, dw_dw, db_dw, dlg, dlb, db_in = _cv_bwd(dsact, s["c"], s["u"], full["cv_w_dw"][j], full["cv_ln_g"][j][None],
                                                        full["cv_ln_b"][j][None], name=f"cv_bwd_{i}")
            small_layers["cv_w_dw"][j] = dw_dw
            small_layers["cv_b_dw"][j] = db_dw[0]
            small_layers["cv_ln_g"][j] = dlg[0]
            small_layers["cv_ln_b"][j] = dlb[0]
            small_layers["cv_b_in"][j] = db_in[0]
            dh1 = _mm_cols_nt(du, wc, name=f"cv_in_dx_{i}")
            dwc = _mm_cols_tn(s["h1"], du, name=f"cv_in_dw_{i}")
        else:
            hp = s["hp"]
            dyn = _mm_rows_nt(dxb, wr, after=behind, name=f"ssm_out_dx_{i}")
            dwr = _mm_rows_tn(s["yn"], dxb, name=f"ssm_out_dw_{i}")
            dy, dz, dng = _gnorm_bwd(dyn, s["y"], s["z"], full["ssm_norm_g"][j][None], name=f"gnorm_bwd_{i}")
            small_layers["ssm_norm_g"][j] = dng[0]
            dxbc, ddt_raw, dbias, dalog, dd = _ssd_bwd(dy, s["xbc"], s["dt_raw"], s["states"], hp["dt_bias"], hp["a_log"],
                                                      hp["d_x"], e, et, name=f"ssd_bwd_{i}")
            small_layers["ssm_dt_bias"][j] = dbias[0, :SSM_HEADS]
            small_layers["ssm_a_log"][j] = dalog[0, :SSM_HEADS]
            small_layers["ssm_d"][j] = dd[0, :SSM_HEADS]
            dxpre, dw_conv, db_conv = _ssm_conv_bwd(dxbc, s["xpre"], full["ssm_w_conv"][j], full["ssm_b_conv"][j][None],
                                                    name=f"ssm_conv_bwd_{i}")
            small_layers["ssm_w_conv"][j] = dw_conv
            small_layers["ssm_b_conv"][j] = db_conv[0]
            dzx = _ssm_pad(dz, dxpre, ddt_raw, name=f"ssm_pad_{i}")
            dh1 = _mm_cols_nt(dzx, wc, name=f"ssm_in_dx_{i}")
            dwc = _mm_cols_tn(s["h1"], dzx, name=f"ssm_in_dw_{i}")
        dx, dxb, dg, _ = _rms_bwd(s["x_in"], full["norm_mix_g"][i][None], dh1, dx, name=f"rms_mix_bwd_{i}")
        small_layers["norm_mix_g"][i] = dg[0]
        behind = emit(2 * i, dwc, dwr)
    for n, layers in small_layers.items():
        grads[n] = jnp.stack(layers)
    return loss, dx, grads


def _update_big(names, recvs, w, m, v):
    out = {}
    for n in names:
        b = recvs[n][0].shape[-1]
        pad = ((0, 0), (0, 0), (0, b - w[n].shape[-1]))
        res = _adamw(recvs[n], *[jnp.pad(t, pad) for t in (w[n], m[n], v[n])], name="adamw_" + n)
        out[n] = [r[..., :w[n].shape[-1]] for r in res]
    return out


def _update_small(grads, w, m, v):
    me = _my_index()
    out = {}
    small_part = _pack([grads[n] for n in SMALL])
    small_all = _all_gather(small_part[None], by_rows=False, name="gather_small_grads")[0]
    small_sum = _sum_slots(small_all, name="sum_small_grads")
    gfull = dict(zip(SMALL, _unpack(small_sum, [grads[n].shape for n in SMALL])))
    glocal = []
    for n in SMALL:
        g = gfull[n]
        if n in SHARDED_SMALL:
            cols = w[n].shape[-1]
            g = lax.dynamic_slice_in_dim(g, me * cols, cols, axis=g.ndim - 1)
        glocal.append(g)
    packed = [_pack(glocal)] + [_pack([t[n] for n in SMALL]) for t in (w, m, v)]
    res = _adamw_small(*packed, name="adamw_small")
    shapes = [w[n].shape for n in SMALL]
    unpacked = [_unpack(r, shapes) for r in res]
    for idx, n in enumerate(SMALL):
        out[n] = [glocal[idx]] + [u[idx] for u in unpacked]
    return out


GATHER_AHEAD = 3


def _train_step(x, target, w, m, v):
    full = _gather_small_weights(w)
    gathers, exchanges = {}, {}
    last_start = [jnp.zeros((8, 128), F32)]

    def start_gather(si, after):
        cn, rn, j = _stage_weights(si)
        wc, wr = w[cn][j].astype(BF16), w[rn][j].astype(BF16)
        if cn == "ssm_w_in":
            wc = jnp.pad(wc, ((0, 0), (0, SSM_SLAB_PAD - SSM_SLAB)))
        shapes = [(N_DEV,) + wc.shape, (N_DEV * wr.shape[0], wr.shape[1])]
        gathers[si], last_start[0] = _copies_start([wc, wr], shapes, scatter=False, after=after,
                                                   name=f"gather_start_{si}")

    for si in range(GATHER_AHEAD):
        start_gather(si, [last_start[0]])

    def weights_of(si, x_act):
        lc, lr = _copies_wait(gathers.pop(si), scatter=False, after=[x_act, last_start[0]], name=f"gather_wait_{si}")
        behind = None
        if si + GATHER_AHEAD < N_STAGES:
            start_gather(si + GATHER_AHEAD, [last_start[0], lc])
            behind = last_start[0]
        return lc, lr, behind

    def emit(si, dwc, dwr):
        shapes = [dwc.shape, (N_DEV, dwr.shape[0] // N_DEV, dwr.shape[1])]
        exchanges[si], token = _copies_start([dwc, dwr], shapes, scatter=True, after=[], name=f"exchange_start_{si}")
        return token

    loss, dx, grads = _forward_backward(x, target, weights_of, full, emit)

    out = _update_small(grads, w, m, v)
    recvs = {n: [None] * w[n].shape[0] for n in BIG_COLS + BIG_ROWS}
    behind = [dx, out[SMALL[0]][1]]
    for si in reversed(range(N_STAGES)):
        cn, rn, j = _stage_weights(si)
        if si == 0:
            out.update(_update_big([n for n in BIG_COLS + BIG_ROWS if not n.startswith("cv_")], recvs, w, m, v))
            behind = [out["ffn_w_up"][1], out["ffn_w_down"][1], out["ssm_w_in"][1], out["ssm_w_out"][1]]
        recvs[cn][j], recvs[rn][j] = _copies_wait(exchanges.pop(si), scatter=True, after=behind,
                                                  name=f"exchange_wait_{si}")
        behind = [recvs[cn][j]]
    out.update(_update_big(["cv_w_in", "cv_w_out"], recvs, w, m, v))
    return lax.psum(loss[0, 0], AXES), dx, out


def kernel(x, norm_mix_g, norm_ffn_g, norm_final_g, cv_w_in, cv_b_in, cv_w_dw, cv_b_dw, cv_ln_g, cv_ln_b, cv_w_out, cv_b_out, ssm_w_in, ssm_w_conv, ssm_b_conv, ssm_dt_bias, ssm_a_log, ssm_d, ssm_norm_g, ssm_w_out, ffn_w_up, ffn_w_dw, ffn_b_dw, ffn_w_down, loss_target, m_norm_mix_g, m_norm_ffn_g, m_norm_final_g, m_cv_w_in, m_cv_b_in, m_cv_w_dw, m_cv_b_dw, m_cv_ln_g, m_cv_ln_b, m_cv_w_out, m_cv_b_out, m_ssm_w_in, m_ssm_w_conv, m_ssm_b_conv, m_ssm_dt_bias, m_ssm_a_log, m_ssm_d, m_ssm_norm_g, m_ssm_w_out, m_ffn_w_up, m_ffn_w_dw, m_ffn_b_dw, m_ffn_w_down, v_norm_mix_g, v_norm_ffn_g, v_norm_final_g, v_cv_w_in, v_cv_b_in, v_cv_w_dw, v_cv_b_dw, v_cv_ln_g, v_cv_ln_b, v_cv_w_out, v_cv_b_out, v_ssm_w_in, v_ssm_w_conv, v_ssm_b_conv, v_ssm_dt_bias, v_ssm_a_log, v_ssm_d, v_ssm_norm_g, v_ssm_w_out, v_ffn_w_up, v_ffn_w_dw, v_ffn_b_dw, v_ffn_w_down):
    args = locals()
    w = {n: args[n] for n in WEIGHTS}
    m = {n: args["m_" + n] for n in WEIGHTS}
    v = {n: args["v_" + n] for n in WEIGHTS}
    loss, dx, out = _train_step(x[0], loss_target[0], w, m, v)
    return (loss, dx[None], *[out[n][0] for n in WEIGHTS], *[out[n][1] for n in WEIGHTS],
            *[out[n][2] for n in WEIGHTS], *[out[n][3] for n in WEIGHTS])
```

```python
import functools

import jax
import jax.numpy as jnp
from jax import lax
from jax.experimental import pallas as pl
from jax.experimental.pallas import tpu as pltpu

F32, BF16 = jnp.float32, jnp.bfloat16
AXES = ("x", "y", "c")
N_DEV = 8
MESH_ID = pl.DeviceIdType.MESH

D_MODEL = 2048
DEPTH = 4
CHUNK = 64
CONV_K = 31
SSM_INNER = 4096
SSM_HEADS = 64
SSM_HEAD_DIM = 64
SSM_GROUPS = 8
SSM_GROUP_W = SSM_INNER // SSM_GROUPS
SSM_STATE = 128
SSM_CONV_K = 4
SSM_CONV_DIM = SSM_INNER + 2 * SSM_GROUPS * SSM_STATE
SSM_IN_DIM = SSM_INNER + SSM_CONV_DIM + SSM_HEADS
SSM_SLAB = SSM_IN_DIM // N_DEV
SSM_SLAB_PAD = 1408
SSM_COMPACT = 1280 * (N_DEV - 1) + SSM_SLAB_PAD
FFN_HIDDEN = 5632
FFN_K = 3
HEAD_LANES = 128
RMS_EPS = 1e-6
LN_EPS = 1e-5
ADAM_LR, ADAM_B1, ADAM_B2, ADAM_EPS, ADAM_WD, ADAM_STEP = 0.001, 0.9, 0.999, 1e-08, 0.01, 10

_DIMS = {
    "nn": (((1,), (0,)), ((), ())),
    "nt": (((1,), (1,)), ((), ())),
    "tn": (((0,), (0,)), ((), ())),
}


def _params(sem=None, vmem_mb=48):
    return pltpu.CompilerParams(dimension_semantics=sem, vmem_limit_bytes=vmem_mb << 20)


_HBM = pl.BlockSpec(memory_space=pltpu.HBM)
_ANY = pl.BlockSpec(memory_space=pl.ANY)
_SEM = pl.BlockSpec(memory_space=pltpu.SEMAPHORE)


def _dot(a, b, mode="nn"):
    return lax.dot_general(a, b, _DIMS[mode], preferred_element_type=F32)


def _split3(x):
    hi = x.astype(BF16)
    r1 = x - hi.astype(F32)
    mid = r1.astype(BF16)
    lo = (r1 - mid.astype(F32)).astype(BF16)
    return hi, mid, lo


def _dot3_l(x, m, mode="nn", parts=3):
    return sum(_dot(p, m, mode) for p in _split3(x)[:parts])


def _dot3_r(m, x, mode="nn", parts=3):
    return sum(_dot(m, p, mode) for p in _split3(x)[:parts])


def _sigmoid(x):
    return jax.nn.sigmoid(x)


def _dsilu(x, sg):
    return sg * (1.0 + x * (1.0 - sg))


def _softplus(x):
    return jnp.maximum(x, 0.0) + jnp.log(1.0 + jnp.exp(-jnp.abs(x)))


def _tile(n, pref):
    return min(n, pref)


def _matmul(a, b, *, mode, grid, a_spec, b_spec, o_spec, o_block, out_shape, nk, name,
            extras=(), epilogue=None, after=None, vmem_mb=48):
    n_extra = len(extras)
    if after is not None:
        extras = tuple(extras) + ((after, _ANY),)
    n_in = len(extras)

    def body(a_ref, b_ref, *rest):
        extra_refs = rest[:n_extra]
        o_ref = rest[n_in]
        if len(b_ref.shape) == 3:
            n = b_ref.shape[-1]
            part = (_dot(a_ref[:, 0:n].astype(BF16), b_ref[0].astype(BF16), mode)
                    + _dot(a_ref[:, n:2 * n].astype(BF16), b_ref[1].astype(BF16), mode))
        else:
            part = _dot(a_ref[...].astype(BF16), b_ref[...].astype(BF16), mode)

        def finish(acc):
            if epilogue is not None:
                acc = epilogue(acc, *[r[...] for r in extra_refs])
            o_ref[...] = acc.astype(o_ref.dtype)

        if nk == 1:
            finish(part)
        else:
            acc_ref = rest[n_in + 1]
            k = pl.program_id(len(grid) - 1)

            @pl.when(k == 0)
            def _():
                acc_ref[...] = part

            @pl.when(k > 0)
            def _():
                acc_ref[...] += part

            @pl.when(k == nk - 1)
            def _():
                finish(acc_ref[...])

    scratch = [] if nk == 1 else [pltpu.VMEM(o_block, F32)]
    sem = ("parallel",) * (len(grid) - 1) + ("arbitrary",)
    return pl.pallas_call(
        body, name=name, grid=grid,
        in_specs=[a_spec, b_spec] + [s for _, s in extras],
        out_specs=o_spec, out_shape=out_shape, scratch_shapes=scratch,
        compiler_params=_params(sem, vmem_mb),
    )(a, b, *[x for x, _ in extras])


HALF_SLABS = N_DEV // 2


def _mm_cols_nn(h, wg, *, name, bias=None, after=None, halves=False):
    L, K = h.shape
    n = wg.shape[-1]
    tm = _tile(L, 512)
    extras, epi = (), None
    if bias is not None:
        extras = ((bias, pl.BlockSpec((1, n), lambda s, i: (0, s))),)
        epi = lambda acc, b: acc + b
    if halves:
        o_spec = pl.BlockSpec((None, tm, n), lambda s, i: (s // HALF_SLABS, i, s % HALF_SLABS))
        out_shape = jax.ShapeDtypeStruct((2, L, HALF_SLABS * n), F32)
    else:
        o_spec = pl.BlockSpec((tm, n), lambda s, i: (i, s))
        out_shape = jax.ShapeDtypeStruct((L, N_DEV * n), F32)
    return _matmul(
        h, wg, mode="nn", grid=(N_DEV, L // tm),
        a_spec=pl.BlockSpec((tm, K), lambda s, i: (i, 0)),
        b_spec=pl.BlockSpec((None, K, n), lambda s, i: (s, 0, 0)),
        o_spec=o_spec, o_block=(tm, n), out_shape=out_shape, nk=1, name=name,
        extras=extras, epilogue=epi, after=after)


def _mm_cols_nt(du, wg, *, name):
    L = du.shape[-2]
    K, n = wg.shape[-2:]
    tm = _tile(L, 512)
    pairs = HALF_SLABS // 2
    if du.ndim == 3:
        a_spec = pl.BlockSpec((None, tm, 2 * n), lambda i, s: (s // pairs, i, s % pairs))
    else:
        a_spec = pl.BlockSpec((tm, 2 * n), lambda i, s: (i, s))
    return _matmul(
        du, wg, mode="nt", grid=(L // tm, N_DEV // 2), a_spec=a_spec,
        b_spec=pl.BlockSpec((2, K, n), lambda i, s: (s, 0, 0)),
        o_spec=pl.BlockSpec((tm, K), lambda i, s: (i, 0)), o_block=(tm, K),
        out_shape=jax.ShapeDtypeStruct((L, K), F32), nk=N_DEV // 2, name=name)


def _mm_cols_tn(h, du, *, name):
    L, K = h.shape
    tm = _tile(L, 512)
    if du.ndim == 3:
        n = du.shape[2] // HALF_SLABS
        b_spec = pl.BlockSpec((None, L, n), lambda s, q: (s // HALF_SLABS, 0, s % HALF_SLABS))
    else:
        n = du.shape[1] // N_DEV
        b_spec = pl.BlockSpec((L, n), lambda s, q: (0, s))
    tq = 256 if n > 1024 else 512
    return _matmul(
        h, du, mode="tn", grid=(N_DEV, K // tq),
        a_spec=pl.BlockSpec((L, tq), lambda s, q: (0, q)), b_spec=b_spec,
        o_spec=pl.BlockSpec((None, tq, n), lambda s, q: (s, q, 0)), o_block=(tq, n),
        out_shape=jax.ShapeDtypeStruct((N_DEV, K, n), BF16), nk=1, name=name)


def _mm_rows_nn(a, wg, *, res, name, bias=None):
    L, Kw = a.shape
    N = wg.shape[-1]
    tm, tn = _tile(L, 512), 512
    extras = [(res, pl.BlockSpec((tm, tn), lambda j, i: (i, j)))]
    if bias is not None:
        extras.append((bias, pl.BlockSpec((1, tn), lambda j, i: (0, j))))
        epi = lambda acc, r, b: acc + r + b
    else:
        epi = lambda acc, r: acc + r
    return _matmul(
        a, wg, mode="nn", grid=(N // tn, L // tm),
        a_spec=pl.BlockSpec((tm, Kw), lambda j, i: (i, 0)),
        b_spec=pl.BlockSpec((Kw, tn), lambda j, i: (0, j)),
        o_spec=pl.BlockSpec((tm, tn), lambda j, i: (i, j)), o_block=(tm, tn),
        out_shape=jax.ShapeDtypeStruct((L, N), F32), nk=1, name=name,
        extras=tuple(extras), epilogue=epi)


def _mm_rows_nt(dy, wg, *, name, after=None):
    L, N = dy.shape
    Kw = wg.shape[-2]
    tm, tn = _tile(L, 512), 512
    return _matmul(
        dy, wg, mode="nt", grid=(L // tm, Kw // tn),
        a_spec=pl.BlockSpec((tm, N), lambda i, q: (i, 0)),
        b_spec=pl.BlockSpec((tn, N), lambda i, q: (q, 0)),
        o_spec=pl.BlockSpec((tm, tn), lambda i, q: (i, q)), o_block=(tm, tn),
        out_shape=jax.ShapeDtypeStruct((L, Kw), F32), nk=1, name=name, after=after)


def _mm_rows_tn(a, dy, *, name):
    L, Kw = a.shape
    N = dy.shape[1]
    tq, tn = 512, 1024
    return _matmul(
        a, dy, mode="tn", grid=(N // tn, Kw // tq),
        a_spec=pl.BlockSpec((L, tq), lambda p, q: (0, q)),
        b_spec=pl.BlockSpec((L, tn), lambda p, q: (0, p)),
        o_spec=pl.BlockSpec((tq, tn), lambda p, q: (q, p)), o_block=(tq, tn),
        out_shape=jax.ShapeDtypeStruct((Kw, N), BF16), nk=1, name=name)


def _rms_fwd(x, g, *, name):
    L, Dm = x.shape
    tm = _tile(L, 256)

    def body(x_ref, g_ref, h_ref):
        xv = x_ref[...]
        r = lax.rsqrt(jnp.mean(xv * xv, axis=-1, keepdims=True) + RMS_EPS)
        h_ref[...] = (xv * r * g_ref[...]).astype(BF16)

    return pl.pallas_call(
        body, name=name, grid=(L // tm,),
        in_specs=[pl.BlockSpec((tm, Dm), lambda i: (i, 0)), pl.BlockSpec((1, Dm), lambda i: (0, 0))],
        out_specs=pl.BlockSpec((tm, Dm), lambda i: (i, 0)),
        out_shape=jax.ShapeDtypeStruct((L, Dm), BF16),
        compiler_params=_params(("parallel",)),
    )(x, g)


def _rms_bwd(x, g, dh, dres, *, name):
    L, Dm = x.shape
    tm = _tile(L, 256)

    def body(x_ref, g_ref, dh_ref, dres_ref, dx_ref, dxb_ref, dg_ref, cs_ref):
        i = pl.program_id(0)
        xv = x_ref[...]
        r = lax.rsqrt(jnp.mean(xv * xv, axis=-1, keepdims=True) + RMS_EPS)
        xh = xv * r
        dh = dh_ref[...]
        dxh = dh * g_ref[...]
        dx = dres_ref[...] + r * (dxh - xh * jnp.mean(dxh * xh, axis=-1, keepdims=True))
        dx_ref[...] = dx
        dxb_ref[...] = dx.astype(BF16)

        @pl.when(i == 0)
        def _():
            dg_ref[...] = jnp.zeros_like(dg_ref)
            cs_ref[...] = jnp.zeros_like(cs_ref)

        dg_ref[...] += jnp.sum(dh * xh, axis=0, keepdims=True)
        cs_ref[...] += jnp.sum(dx, axis=0, keepdims=True)

    row = pl.BlockSpec((tm, Dm), lambda i: (i, 0))
    vec = pl.BlockSpec((1, Dm), lambda i: (0, 0))
    return pl.pallas_call(
        body, name=name, grid=(L // tm,),
        in_specs=[row, vec, row, row], out_specs=[row, row, vec, vec],
        out_shape=[jax.ShapeDtypeStruct((L, Dm), F32), jax.ShapeDtypeStruct((L, Dm), BF16),
                   jax.ShapeDtypeStruct((1, Dm), F32), jax.ShapeDtypeStruct((1, Dm), F32)],
        compiler_params=_params(("arbitrary",)),
    )(x, g, dh, dres)


def _final_loss(x, g, target, *, name):
    L, Dm = x.shape
    tm = _tile(L, 256)

    def body(x_ref, g_ref, t_ref, dx_ref, dxb_ref, dg_ref, loss_ref):
        i = pl.program_id(0)
        xv = x_ref[...]
        gv = g_ref[...]
        r = lax.rsqrt(jnp.mean(xv * xv, axis=-1, keepdims=True) + RMS_EPS)
        xh = xv * r
        err = xh * gv - t_ref[...]
        dy = err * (1.0 / Dm)
        dxh = dy * gv
        dx = r * (dxh - xh * jnp.mean(dxh * xh, axis=-1, keepdims=True))
        dx_ref[...] = dx
        dxb_ref[...] = dx.astype(BF16)

        @pl.when(i == 0)
        def _():
            dg_ref[...] = jnp.zeros_like(dg_ref)
            loss_ref[...] = jnp.zeros_like(loss_ref)

        dg_ref[...] += jnp.sum(dy * xh, axis=0, keepdims=True)
        loss_ref[...] += 0.5 * jnp.sum(jnp.mean(err * err, axis=-1, keepdims=True), axis=0, keepdims=True)

    row = pl.BlockSpec((tm, Dm), lambda i: (i, 0))
    vec = pl.BlockSpec((1, Dm), lambda i: (0, 0))
    return pl.pallas_call(
        body, name=name, grid=(L // tm,),
        in_specs=[row, vec, row],
        out_specs=[row, row, vec, pl.BlockSpec((1, 128), lambda i: (0, 0))],
        out_shape=[jax.ShapeDtypeStruct((L, Dm), F32), jax.ShapeDtypeStruct((L, Dm), BF16),
                   jax.ShapeDtypeStruct((1, Dm), F32), jax.ShapeDtypeStruct((1, 128), F32)],
        compiler_params=_params(("arbitrary",)),
    )(x, g, target)


def _conv_from_ext(ext_ref, w_ref, bias, taps, halo, rows):
    acc = jnp.broadcast_to(bias, (rows, ext_ref.shape[1]))
    for j in range(taps):
        acc = acc + w_ref[taps - 1 - j:taps - j, :] * ext_ref[halo - j:halo - j + rows, :]
    return acc


def _conv_back(dext_ref, x_cur, w_ref, dw_ref, taps, rows):
    dx = jnp.zeros((rows, dext_ref.shape[1]), F32)
    for j in range(taps):
        sh = dext_ref[j:j + rows, :]
        dx = dx + w_ref[taps - 1 - j:taps - j, :] * sh
        dw_ref[taps - 1 - j:taps - j, :] += jnp.sum(x_cur * sh, axis=0, keepdims=True)
    return dx


STRIP = 128
ROW_BLOCK = 64


def _strips(width):
    return [slice(s, s + STRIP) for s in range(0, width, STRIP)]


def _row_blocks(n):
    return [(lo, min(ROW_BLOCK, n - lo)) for lo in range(0, n, ROW_BLOCK)]


def _conv_strip(ext_ref, taps, bias, halo, lo, n, ls):
    acc = jnp.broadcast_to(bias, (n, STRIP))
    for j in range(len(taps)):
        acc = acc + taps[len(taps) - 1 - j] * ext_ref[halo + lo - j:halo + lo - j + n, ls]
    return acc


def _conv_back_strip(dext_ref, x_blk, taps, dw_acc, lo, n, ls):
    dx = jnp.zeros((n, STRIP), F32)
    for j in range(len(taps)):
        k = len(taps) - 1 - j
        sh = dext_ref[lo + j:lo + j + n, ls]
        dx = dx + taps[k] * sh
        dw_acc[k] = dw_acc[k] + jnp.sum(x_blk * sh, axis=0, keepdims=True)
    return dx


def _prev_blk(i, r):
    return jnp.maximum(i * r - 1, 0)


def _next_blk(i, r, nblk):
    return jnp.minimum((i + 1) * r, nblk - 1)


CV_HALO = 32
LN_ROWS = 16


def _cv_mid_fwd(u, w_dw, b_dw, ln_g, ln_b, *, name):
    L = u.shape[0]
    Dm = D_MODEL
    tm = _tile(L, 256)
    r = tm // CV_HALO

    def body(a_ref, g_ref, ah_ref, gh_ref, w_ref, bdw_ref, lg_ref, lb_ref, s_ref, c_ref, ext_ref):
        i = pl.program_id(0)
        keep = (i > 0).astype(F32)
        ext_ref[0:CV_HALO, :] = ah_ref[...] * _sigmoid(gh_ref[...]) * keep
        ext_ref[CV_HALO:CV_HALO + tm, :] = a_ref[...] * _sigmoid(g_ref[...])
        for ls in _strips(Dm):
            taps = [w_ref[k:k + 1, ls] for k in range(CONV_K)]
            for lo, n in _row_blocks(tm):
                c_ref[lo:lo + n, ls] = _conv_strip(ext_ref, taps, bdw_ref[:, ls], CV_HALO, lo, n, ls)
        for lo in range(0, tm, LN_ROWS):
            c = c_ref[lo:lo + LN_ROWS, :]
            xc = c - jnp.mean(c, axis=-1, keepdims=True)
            rstd = lax.rsqrt(jnp.mean(xc * xc, axis=-1, keepdims=True) + LN_EPS)
            l = xc * rstd * lg_ref[...] + lb_ref[...]
            s_ref[lo:lo + LN_ROWS, :] = (l * _sigmoid(l)).astype(BF16)

    vec = pl.BlockSpec((1, Dm), lambda i: (0, 0))
    return pl.pallas_call(
        body, name=name, grid=(L // tm,),
        in_specs=[pl.BlockSpec((tm, Dm), lambda i: (i, 0)), pl.BlockSpec((tm, Dm), lambda i: (i, 1)),
                  pl.BlockSpec((CV_HALO, Dm), lambda i: (_prev_blk(i, r), 0)),
                  pl.BlockSpec((CV_HALO, Dm), lambda i: (_prev_blk(i, r), 1)),
                  pl.BlockSpec((CONV_K, Dm), lambda i: (0, 0)), vec, vec, vec],
        out_specs=[pl.BlockSpec((tm, Dm), lambda i: (i, 0)), pl.BlockSpec((tm, Dm), lambda i: (i, 0))],
        out_shape=[jax.ShapeDtypeStruct((L, Dm), BF16), jax.ShapeDtypeStruct((L, Dm), F32)],
        scratch_shapes=[pltpu.VMEM((CV_HALO + tm, Dm), F32)],
        compiler_params=_params(("parallel",)),
    )(u, u, u, u, w_dw, b_dw, ln_g, ln_b)


def _cv_bwd(ds, c, u, w_dw, ln_g, ln_b, *, name):
    L = u.shape[0]
    Dm = D_MODEL
    tm = _tile(L, 128)
    r = tm // CV_HALO
    nt = L // tm
    nblk = L // CV_HALO
    ext_rows = tm + CV_HALO

    def body(ds_ref, dsn_ref, c_ref, cn_ref, a_ref, g_ref, w_ref, lg_ref, lb_ref,
             du_ref, dw_ref, dbdw_ref, dlg_ref, dlb_ref, dbin_ref, dext_ref):
        i = pl.program_id(0)

        @pl.when(i == 0)
        def _():
            for ref in (dw_ref, dbdw_ref, dlg_ref, dlb_ref, dbin_ref):
                ref[...] = jnp.zeros_like(ref)

        keep = (i < nt - 1).astype(F32)
        for lo in range(0, ext_rows, LN_ROWS):
            if lo < tm:
                dsv, cv = ds_ref[lo:lo + LN_ROWS, :], c_ref[lo:lo + LN_ROWS, :]
            else:
                dsv, cv = dsn_ref[lo - tm:lo - tm + LN_ROWS, :] * keep, cn_ref[lo - tm:lo - tm + LN_ROWS, :]
            xc = cv - jnp.mean(cv, axis=-1, keepdims=True)
            rstd = lax.rsqrt(jnp.mean(xc * xc, axis=-1, keepdims=True) + LN_EPS)
            nrm = xc * rstd
            lg = lg_ref[...]
            l = nrm * lg + lb_ref[...]
            dl = dsv * _dsilu(l, _sigmoid(l))
            dn = dl * lg
            dc = rstd * (dn - jnp.mean(dn, axis=-1, keepdims=True)
                         - nrm * jnp.mean(dn * nrm, axis=-1, keepdims=True))
            dext_ref[lo:lo + LN_ROWS, :] = dc
            if lo < tm:
                dlg_ref[...] += jnp.sum(dl * nrm, axis=0, keepdims=True)
                dlb_ref[...] += jnp.sum(dl, axis=0, keepdims=True)
                dbdw_ref[...] += jnp.sum(dc, axis=0, keepdims=True)

        for ls in _strips(Dm):
            gs = slice(Dm + ls.start, Dm + ls.stop)
            for lo, n in _row_blocks(tm):
                av, sg = a_ref[lo:lo + n, ls], _sigmoid(g_ref[lo:lo + n, ls])
                v = av * sg
                dv = jnp.zeros((n, STRIP), F32)
                for j in range(CONV_K):
                    k = CONV_K - 1 - j
                    sh = dext_ref[lo + j:lo + j + n, ls]
                    dv = dv + w_ref[k:k + 1, ls] * sh
                    dw_ref[k:k + 1, ls] += jnp.sum(v * sh, axis=0, keepdims=True)
                da = dv * sg
                dgate = dv * av * sg * (1.0 - sg)
                du_ref[lo:lo + n, ls] = da.astype(BF16)
                du_ref[lo:lo + n, gs] = dgate.astype(BF16)
                dbin_ref[:, ls] += jnp.sum(da, axis=0, keepdims=True)
                dbin_ref[:, gs] += jnp.sum(dgate, axis=0, keepdims=True)

    row = pl.BlockSpec((tm, Dm), lambda i: (i, 0))
    nxt = pl.BlockSpec((CV_HALO, Dm), lambda i: (_next_blk(i, r, nblk), 0))
    vec = pl.BlockSpec((1, Dm), lambda i: (0, 0))
    return pl.pallas_call(
        body, name=name, grid=(nt,),
        in_specs=[row, nxt, row, nxt, row, pl.BlockSpec((tm, Dm), lambda i: (i, 1)),
                  pl.BlockSpec((CONV_K, Dm), lambda i: (0, 0)), vec, vec],
        out_specs=[pl.BlockSpec((tm, 2 * Dm), lambda i: (i, 0)), pl.BlockSpec((CONV_K, Dm), lambda i: (0, 0)),
                   vec, vec, vec, pl.BlockSpec((1, 2 * Dm), lambda i: (0, 0))],
        out_shape=[jax.ShapeDtypeStruct((L, 2 * Dm), BF16), jax.ShapeDtypeStruct((CONV_K, Dm), F32),
                   jax.ShapeDtypeStruct((1, Dm), F32), jax.ShapeDtypeStruct((1, Dm), F32),
                   jax.ShapeDtypeStruct((1, Dm), F32), jax.ShapeDtypeStruct((1, 2 * Dm), F32)],
        scratch_shapes=[pltpu.VMEM((ext_rows, Dm), F32)],
        compiler_params=_params(("arbitrary",)),
    )(ds, ds, c, c, u, u, w_dw, ln_g, ln_b)


FFN_TC = 512
FFN_NJ = FFN_HIDDEN // FFN_TC
HALO8 = 8


def _ffn_mid_fwd(u0, w_dw, b_dw, *, name):
    L = u0.shape[1]
    tm = _tile(L, 256)
    r = tm // HALO8
    tc, nj = FFN_TC, FFN_NJ

    def body(u_ref, uh_ref, w_ref, b_ref, o_ref, ext_ref):
        keep = (pl.program_id(0) > 0).astype(F32)
        for h in range(2):
            ext = ext_ref.at[h]
            ext[0:HALO8, :] = uh_ref[h] * keep
            ext[HALO8:HALO8 + tm, :] = u_ref[h]
        for ls in _strips(tc):
            taps = [[w_ref[h, k:k + 1, ls] for k in range(FFN_K)] for h in range(2)]
            for lo, n in _row_blocks(tm):
                ug = _conv_strip(ext_ref.at[0], taps[0], b_ref[0, :, ls], HALO8, lo, n, ls)
                uv = _conv_strip(ext_ref.at[1], taps[1], b_ref[1, :, ls], HALO8, lo, n, ls)
                o_ref[lo:lo + n, ls] = (ug * _sigmoid(ug) * uv).astype(BF16)

    return pl.pallas_call(
        body, name=name, grid=(L // tm, nj),
        in_specs=[pl.BlockSpec((2, tm, tc), lambda i, j: (0, i, j)),
                  pl.BlockSpec((2, HALO8, tc), lambda i, j: (0, _prev_blk(i, r), j)),
                  pl.BlockSpec((2, FFN_K, tc), lambda i, j: (0, 0, j)), pl.BlockSpec((2, 1, tc), lambda i, j: (0, 0, j))],
        out_specs=pl.BlockSpec((tm, tc), lambda i, j: (i, j)),
        out_shape=jax.ShapeDtypeStruct((L, FFN_HIDDEN), BF16),
        scratch_shapes=[pltpu.VMEM((2, HALO8 + tm, tc), F32)],
        compiler_params=_params(("parallel", "parallel")),
    )(u0, u0, w_dw, b_dw)


def _ffn_bwd(dg, u0, w_dw, b_dw, *, name):
    L = u0.shape[1]
    tm = _tile(L, 256)
    r = tm // HALO8
    nt = L // tm
    nblk = L // HALO8
    tc, nj = FFN_TC, FFN_NJ
    rows = tm + HALO8

    def body(up_ref, uc_ref, un_ref, dg_ref, dgn_ref, w_ref, b_ref, du0_ref, dw_ref, db_ref, ext_ref, dext_ref):
        i = pl.program_id(1)

        @pl.when(i == 0)
        def _():
            dw_ref[...] = jnp.zeros_like(dw_ref)
            db_ref[...] = jnp.zeros_like(db_ref)

        keep_prev = (i > 0).astype(F32)
        keep_next = (i < nt - 1).astype(F32)
        for h in range(2):
            ext = ext_ref.at[h]
            ext[0:HALO8, :] = up_ref[h] * keep_prev
            ext[HALO8:HALO8 + tm, :] = uc_ref[h]
            ext[HALO8 + tm:HALO8 + rows, :] = un_ref[h]
        for ls in _strips(tc):
            taps = [[w_ref[h, k:k + 1, ls] for k in range(FFN_K)] for h in range(2)]
            dw_acc = [[jnp.zeros((1, STRIP), F32)] * FFN_K for _ in range(2)]
            db_acc = [jnp.zeros((1, STRIP), F32)] * 2
            for lo, n in _row_blocks(rows):
                ug = _conv_strip(ext_ref.at[0], taps[0], b_ref[0, :, ls], HALO8, lo, n, ls)
                uv = _conv_strip(ext_ref.at[1], taps[1], b_ref[1, :, ls], HALO8, lo, n, ls)
                dgx = dg_ref[lo:lo + n, ls] if lo < tm else dgn_ref[:, ls] * keep_next
                sg = _sigmoid(ug)
                for h, du in ((0, dgx * uv * _dsilu(ug, sg)), (1, dgx * (ug * sg))):
                    dext_ref[h, lo:lo + n, ls] = du
                    if lo < tm:
                        db_acc[h] = db_acc[h] + jnp.sum(du, axis=0, keepdims=True)
            for lo, n in _row_blocks(tm):
                for h in range(2):
                    dx = _conv_back_strip(dext_ref.at[h], uc_ref[h, lo:lo + n, ls], taps[h], dw_acc[h], lo, n, ls)
                    du0_ref[h, lo:lo + n, ls] = dx.astype(BF16)
            for h in range(2):
                db_ref[h, :, ls] += db_acc[h]
                for k in range(FFN_K):
                    dw_ref[h, k:k + 1, ls] += dw_acc[h][k]

    cur = pl.BlockSpec((2, tm, tc), lambda j, i: (0, i, j))
    prv = pl.BlockSpec((2, HALO8, tc), lambda j, i: (0, _prev_blk(i, r), j))
    nxt = pl.BlockSpec((2, HALO8, tc), lambda j, i: (0, _next_blk(i, r, nblk), j))
    wsp = pl.BlockSpec((2, FFN_K, tc), lambda j, i: (0, 0, j))
    bsp = pl.BlockSpec((2, 1, tc), lambda j, i: (0, 0, j))
    return pl.pallas_call(
        body, name=name, grid=(nj, nt),
        in_specs=[prv, cur, nxt, pl.BlockSpec((tm, tc), lambda j, i: (i, j)),
                  pl.BlockSpec((HALO8, tc), lambda j, i: (_next_blk(i, r, nblk), j)), wsp, bsp],
        out_specs=[cur, wsp, bsp],
        out_shape=[jax.ShapeDtypeStruct((2, L, FFN_HIDDEN), BF16), jax.ShapeDtypeStruct((2, FFN_K, FFN_HIDDEN), F32),
                   jax.ShapeDtypeStruct((2, 1, FFN_HIDDEN), F32)],
        scratch_shapes=[pltpu.VMEM((2, HALO8 + rows, tc), F32), pltpu.VMEM((2, rows, tc), F32)],
        compiler_params=_params(("parallel", "arbitrary")),
    )(u0, u0, u0, dg, dg, w_dw, b_dw)


def _ssm_unpad(zx_pad, *, name):
    L = zx_pad.shape[0]
    tm = _tile(L, 128)

    def body(p_ref, z_ref, xbc_ref, dt_ref, comp_ref):
        comp_ref[...] = jnp.zeros_like(comp_ref)
        for d in range(N_DEV):
            slab = p_ref[:, d * SSM_SLAB_PAD:(d + 1) * SSM_SLAB_PAD]
            if d:
                slab = pltpu.roll(slab, 8 * d, axis=1)
            comp_ref[:, 1280 * d:1280 * d + SSM_SLAB_PAD] += slab
        z_ref[...] = comp_ref[:, 0:SSM_INNER]
        xbc_ref[...] = comp_ref[:, SSM_INNER:SSM_INNER + SSM_CONV_DIM]
        dt_ref[...] = comp_ref[:, SSM_INNER + SSM_CONV_DIM:SSM_COMPACT]

    return pl.pallas_call(
        body, name=name, grid=(L // tm,),
        in_specs=[pl.BlockSpec((tm, N_DEV * SSM_SLAB_PAD), lambda i: (i, 0))],
        out_specs=[pl.BlockSpec((tm, SSM_INNER), lambda i: (i, 0)), pl.BlockSpec((tm, SSM_CONV_DIM), lambda i: (i, 0)),
                   pl.BlockSpec((tm, HEAD_LANES), lambda i: (i, 0))],
        out_shape=[jax.ShapeDtypeStruct((L, SSM_INNER), F32), jax.ShapeDtypeStruct((L, SSM_CONV_DIM), F32),
                   jax.ShapeDtypeStruct((L, HEAD_LANES), F32)],
        scratch_shapes=[pltpu.VMEM((tm, SSM_COMPACT), F32)],
        compiler_params=_params(("parallel",)),
    )(zx_pad)


def _ssm_pad(dz, dxbc, ddt, *, name):
    L = dz.shape[0]
    tm = _tile(L, 128)

    def body(dz_ref, dx_ref, dt_ref, p_ref, comp_ref):
        comp_ref[:, 0:SSM_INNER] = dz_ref[...]
        comp_ref[:, SSM_INNER:SSM_INNER + SSM_CONV_DIM] = dx_ref[...]
        lane = lax.broadcasted_iota(jnp.int32, (tm, HEAD_LANES), 1)
        comp_ref[:, SSM_INNER + SSM_CONV_DIM:SSM_COMPACT] = jnp.where(lane < SSM_HEADS, dt_ref[...], 0.0)
        col = lax.broadcasted_iota(jnp.int32, (tm, SSM_SLAB_PAD), 1)
        for d in range(N_DEV):
            win = comp_ref[:, 1280 * d:1280 * d + SSM_SLAB_PAD]
            if d:
                win = pltpu.roll(win, SSM_SLAB_PAD - 8 * d, axis=1)
            p_ref[:, d * SSM_SLAB_PAD:(d + 1) * SSM_SLAB_PAD] = jnp.where(col < SSM_SLAB, win, 0.0).astype(BF16)

    return pl.pallas_call(
        body, name=name, grid=(L // tm,),
        in_specs=[pl.BlockSpec((tm, SSM_INNER), lambda i: (i, 0)), pl.BlockSpec((tm, SSM_CONV_DIM), lambda i: (i, 0)),
                  pl.BlockSpec((tm, HEAD_LANES), lambda i: (i, 0))],
        out_specs=pl.BlockSpec((tm, N_DEV * SSM_SLAB_PAD), lambda i: (i, 0)),
        out_shape=jax.ShapeDtypeStruct((L, N_DEV * SSM_SLAB_PAD), BF16),
        scratch_shapes=[pltpu.VMEM((tm, SSM_COMPACT), F32)],
        compiler_params=_params(("parallel",)),
    )(dz, dxbc, ddt)


SSM_TC = 1024


def _ssm_conv_fwd(xpre, w, b, *, name):
    L, C = xpre.shape
    tm = _tile(L, 256)
    r = tm // HALO8
    tc = SSM_TC

    def body(x_ref, xh_ref, w_ref, b_ref, o_ref, ext_ref):
        keep = (pl.program_id(0) > 0).astype(F32)
        ext_ref[0:HALO8, :] = xh_ref[...] * keep
        ext_ref[HALO8:HALO8 + tm, :] = x_ref[...]
        for ls in _strips(tc):
            taps = [w_ref[k:k + 1, ls] for k in range(SSM_CONV_K)]
            for lo, n in _row_blocks(tm):
                pre = _conv_strip(ext_ref, taps, b_ref[:, ls], HALO8, lo, n, ls)
                o_ref[lo:lo + n, ls] = pre * _sigmoid(pre)

    return pl.pallas_call(
        body, name=name, grid=(L // tm, C // tc),
        in_specs=[pl.BlockSpec((tm, tc), lambda i, j: (i, j)), pl.BlockSpec((HALO8, tc), lambda i, j: (_prev_blk(i, r), j)),
                  pl.BlockSpec((SSM_CONV_K, tc), lambda i, j: (0, j)), pl.BlockSpec((1, tc), lambda i, j: (0, j))],
        out_specs=pl.BlockSpec((tm, tc), lambda i, j: (i, j)),
        out_shape=jax.ShapeDtypeStruct((L, C), F32),
        scratch_shapes=[pltpu.VMEM((HALO8 + tm, tc), F32)],
        compiler_params=_params(("parallel", "parallel")),
    )(xpre, xpre, w, b)


def _ssm_conv_bwd(dy, xpre, w, b, *, name):
    L, C = xpre.shape
    tm = _tile(L, 256)
    r = tm // HALO8
    nt = L // tm
    nblk = L // HALO8
    tc = SSM_TC
    rows = tm + HALO8

    def body(xp_ref, xc_ref, xn_ref, dy_ref, dyn_ref, w_ref, b_ref, dx_ref, dw_ref, db_ref, ext_ref, dext_ref):
        i = pl.program_id(1)

        @pl.when(i == 0)
        def _():
            dw_ref[...] = jnp.zeros_like(dw_ref)
            db_ref[...] = jnp.zeros_like(db_ref)

        ext_ref[0:HALO8, :] = xp_ref[...] * (i > 0).astype(F32)
        ext_ref[HALO8:HALO8 + tm, :] = xc_ref[...]
        ext_ref[HALO8 + tm:HALO8 + rows, :] = xn_ref[...]
        keep_next = (i < nt - 1).astype(F32)
        for ls in _strips(tc):
            taps = [w_ref[k:k + 1, ls] for k in range(SSM_CONV_K)]
            dw_acc = [jnp.zeros((1, STRIP), F32)] * SSM_CONV_K
            db_acc = jnp.zeros((1, STRIP), F32)
            for lo, n in _row_blocks(rows):
                pre = _conv_strip(ext_ref, taps, b_ref[:, ls], HALO8, lo, n, ls)
                dyx = dy_ref[lo:lo + n, ls] if lo < tm else dyn_ref[:, ls] * keep_next
                dpre = dyx * _dsilu(pre, _sigmoid(pre))
                dext_ref[lo:lo + n, ls] = dpre
                if lo < tm:
                    db_acc = db_acc + jnp.sum(dpre, axis=0, keepdims=True)
            for lo, n in _row_blocks(tm):
                dx_ref[lo:lo + n, ls] = _conv_back_strip(dext_ref, xc_ref[lo:lo + n, ls], taps, dw_acc, lo, n, ls)
            db_ref[:, ls] += db_acc
            for k in range(SSM_CONV_K):
                dw_ref[k:k + 1, ls] += dw_acc[k]

    cur = pl.BlockSpec((tm, tc), lambda j, i: (i, j))
    prv = pl.BlockSpec((HALO8, tc), lambda j, i: (_prev_blk(i, r), j))
    nxt = pl.BlockSpec((HALO8, tc), lambda j, i: (_next_blk(i, r, nblk), j))
    return pl.pallas_call(
        body, name=name, grid=(C // tc, nt),
        in_specs=[prv, cur, nxt, cur, nxt, pl.BlockSpec((SSM_CONV_K, tc), lambda j, i: (0, j)),
                  pl.BlockSpec((1, tc), lambda j, i: (0, j))],
        out_specs=[cur, pl.BlockSpec((SSM_CONV_K, tc), lambda j, i: (0, j)), pl.BlockSpec((1, tc), lambda j, i: (0, j))],
        out_shape=[jax.ShapeDtypeStruct((L, C), F32), jax.ShapeDtypeStruct((SSM_CONV_K, C), F32),
                   jax.ShapeDtypeStruct((1, C), F32)],
        scratch_shapes=[pltpu.VMEM((HALO8 + rows, tc), F32), pltpu.VMEM((rows, tc), F32)],
        compiler_params=_params(("parallel", "arbitrary")),
    )(xpre, xpre, xpre, dy, dy, w, b)


def _gnorm_fwd(y, z, g, *, name):
    L, C = y.shape
    tm = _tile(L, 256)
    gw = SSM_GROUP_W

    def body(y_ref, z_ref, g_ref, o_ref):
        for k in range(SSM_GROUPS):
            sl = slice(k * gw, (k + 1) * gw)
            zv = z_ref[:, sl]
            yz = y_ref[:, sl] * zv * _sigmoid(zv)
            r = lax.rsqrt(jnp.mean(yz * yz, axis=-1, keepdims=True) + RMS_EPS)
            o_ref[:, sl] = (yz * r * g_ref[:, sl]).astype(BF16)

    row = pl.BlockSpec((tm, C), lambda i: (i, 0))
    return pl.pallas_call(
        body, name=name, grid=(L // tm,),
        in_specs=[row, row, pl.BlockSpec((1, C), lambda i: (0, 0))], out_specs=row,
        out_shape=jax.ShapeDtypeStruct((L, C), BF16),
        compiler_params=_params(("parallel",)),
    )(y, z, g)


def _gnorm_bwd(dn, y, z, g, *, name):
    L, C = y.shape
    tm = _tile(L, 256)
    gw = SSM_GROUP_W

    def body(dn_ref, y_ref, z_ref, g_ref, dy_ref, dz_ref, dg_ref):
        @pl.when(pl.program_id(0) == 0)
        def _():
            dg_ref[...] = jnp.zeros_like(dg_ref)

        for k in range(SSM_GROUPS):
            sl = slice(k * gw, (k + 1) * gw)
            zv, yv = z_ref[:, sl], y_ref[:, sl]
            sz = _sigmoid(zv)
            silu = zv * sz
            yz = yv * silu
            r = lax.rsqrt(jnp.mean(yz * yz, axis=-1, keepdims=True) + RMS_EPS)
            nrm = yz * r
            dnv = dn_ref[:, sl]
            dg_ref[:, sl] += jnp.sum(dnv * nrm, axis=0, keepdims=True)
            dh = dnv * g_ref[:, sl]
            dyz = r * (dh - nrm * jnp.mean(dh * nrm, axis=-1, keepdims=True))
            dy_ref[:, sl] = dyz * silu
            dz_ref[:, sl] = dyz * yv * _dsilu(zv, sz)

    row = pl.BlockSpec((tm, C), lambda i: (i, 0))
    vec = pl.BlockSpec((1, C), lambda i: (0, 0))
    return pl.pallas_call(
        body, name=name, grid=(L // tm,),
        in_specs=[row, row, row, vec], out_specs=[row, row, vec],
        out_shape=[jax.ShapeDtypeStruct((L, C), F32), jax.ShapeDtypeStruct((L, C), F32),
                   jax.ShapeDtypeStruct((1, C), F32)],
        compiler_params=_params(("arbitrary",)),
    )(dn, y, z, g)


def _ssd_consts():
    q, gw = CHUNK, SSM_GROUP_W
    sub = lax.broadcasted_iota(jnp.int32, (q, gw), 0)
    lane_in = lax.broadcasted_iota(jnp.int32, (q, gw), 1) % q
    sub2 = lax.broadcasted_iota(jnp.int32, (gw, gw), 0) // q
    lane2 = lax.broadcasted_iota(jnp.int32, (gw, gw), 1) // q
    t = lax.broadcasted_iota(jnp.int32, (q, q), 0)
    u = lax.broadcasted_iota(jnp.int32, (q, q), 1)
    return dict(
        diag=(lane_in == sub), low=(lane_in <= sub), up=(sub <= lane_in), block=(sub2 == lane2),
        tri=(u <= t).astype(BF16), trit=(u >= t).astype(BF16), ones=jnp.ones((q, q), BF16),
        last=(lax.broadcasted_iota(jnp.int32, (q, HEAD_LANES), 0) == q - 1))


def _ssd_chunk_terms(dt_raw, bias, a_log, e, et, k):
    dt = _softplus(dt_raw + bias)
    a_neg = -jnp.exp(a_log)
    cs = _dot3_r(k["tri"], dt * a_neg)
    cs_last = cs[CHUNK - 1:CHUNK, :]
    ecs = jnp.exp(cs)
    dte = jnp.exp(cs_last - cs)
    ecl_hn = _dot3_l(ecs, k["last"].astype(BF16), "tn")
    rows = _dot3_r(et, ecl_hn, parts=2)
    return dict(dt=dt, a_neg=a_neg, cs=cs, ecs=ecs, dte=dte, rows=rows,
                dtx=_dot3_l(dt, e, parts=1), csx=_dot3_l(cs, e, parts=2), ecsx=_dot3_l(ecs, e, parts=1),
                dtex=_dot3_l(dte, e, parts=1))


def _tile8(x):
    return jnp.concatenate([x] * 8, axis=0)


def _ssd_fwd(xbc, dt_raw, dt_bias, a_log, d_x, e, et, *, name):
    L = xbc.shape[0]
    nc = L // CHUNK
    q, gw, ns = CHUNK, SSM_GROUP_W, SSM_STATE

    def body(xbc_ref, dt_ref, bias_ref, alog_ref, dx_ref, e_ref, et_ref, y_ref, st_ref, s_ref):
        @pl.when(pl.program_id(0) == 0)
        def _():
            s_ref[...] = jnp.zeros_like(s_ref)

        k = _ssd_consts()
        t = _ssd_chunk_terms(dt_ref[...], bias_ref[...], alog_ref[...], e_ref[...], et_ref[...], k)
        st_ref[...] = s_ref[...]
        for g in range(SSM_GROUPS):
            ch = slice(g * gw, (g + 1) * gw)
            xs = xbc_ref[:, ch]
            bm = xbc_ref[:, SSM_INNER + g * ns:SSM_INNER + (g + 1) * ns].astype(BF16)
            cm = xbc_ref[:, SSM_INNER + (SSM_GROUPS + g) * ns:SSM_INNER + (SSM_GROUPS + g + 1) * ns].astype(BF16)
            xd = xs * t["dtx"][:, ch]
            csx = t["csx"][:, ch]
            csrow = _dot3_r(k["ones"], jnp.where(k["diag"], csx, 0.0), parts=2)
            lcat = jnp.where(k["low"], jnp.exp(jnp.minimum(csx - csrow, 0.0)), 0.0)
            mcat = _dot(cm, _tile8(bm), "nt") * lcat
            xdbd = jnp.where(k["block"], _tile8(xd), 0.0).astype(BF16)
            sg = s_ref[ch, :]
            y = _dot(mcat.astype(BF16), xdbd)
            y = y + _dot(cm, sg.astype(BF16), "nt") * t["ecsx"][:, ch]
            y_ref[:, ch] = y + dx_ref[:, ch] * xs
            s_ref[ch, :] = sg * t["rows"][ch, :] + _dot((xd * t["dtex"][:, ch]).astype(BF16), bm, "tn")

    hv = pl.BlockSpec((1, HEAD_LANES), lambda c: (0, 0))
    return pl.pallas_call(
        body, name=name, grid=(nc,),
        in_specs=[pl.BlockSpec((q, SSM_CONV_DIM), lambda c: (c, 0)), pl.BlockSpec((q, HEAD_LANES), lambda c: (c, 0)),
                  hv, hv, pl.BlockSpec((1, SSM_INNER), lambda c: (0, 0)),
                  pl.BlockSpec((HEAD_LANES, SSM_INNER), lambda c: (0, 0)),
                  pl.BlockSpec((SSM_INNER, HEAD_LANES), lambda c: (0, 0))],
        out_specs=[pl.BlockSpec((q, SSM_INNER), lambda c: (c, 0)),
                   pl.BlockSpec((None, SSM_INNER, ns), lambda c: (c, 0, 0))],
        out_shape=[jax.ShapeDtypeStruct((L, SSM_INNER), F32), jax.ShapeDtypeStruct((nc, SSM_INNER, ns), F32)],
        scratch_shapes=[pltpu.VMEM((SSM_INNER, ns), F32)],
        compiler_params=_params(("arbitrary",)),
    )(xbc, dt_raw, dt_bias, a_log, d_x, e, et)


def _ssd_bwd(dy, xbc, dt_raw, states, dt_bias, a_log, d_x, e, et, *, name):
    L = xbc.shape[0]
    nc = L // CHUNK
    q, gw, ns = CHUNK, SSM_GROUP_W, SSM_STATE

    def body(dy_ref, xbc_ref, dt_ref, st_ref, bias_ref, alog_ref, dx_ref, e_ref, et_ref,
             dxbc_ref, ddt_ref, dbias_ref, dalog_ref, dd_ref, ds_ref, f1_ref, f2_ref, f3_ref, f4_ref, fs_ref):
        @pl.when(pl.program_id(0) == 0)
        def _():
            ds_ref[...] = jnp.zeros_like(ds_ref)
            for ref in (dbias_ref, dalog_ref, dd_ref):
                ref[...] = jnp.zeros_like(ref)

        k = _ssd_consts()
        ev = e_ref[...]
        t = _ssd_chunk_terms(dt_ref[...], bias_ref[...], alog_ref[...], ev, et_ref[...], k)
        ones8 = jnp.ones((8, ns), BF16)
        for g in range(SSM_GROUPS):
            ch = slice(g * gw, (g + 1) * gw)
            bsl = slice(SSM_INNER + g * ns, SSM_INNER + (g + 1) * ns)
            csl = slice(SSM_INNER + (SSM_GROUPS + g) * ns, SSM_INNER + (SSM_GROUPS + g + 1) * ns)
            xs = xbc_ref[:, ch]
            bm = xbc_ref[:, bsl].astype(BF16)
            cm = xbc_ref[:, csl].astype(BF16)
            dyv = dy_ref[:, ch]
            dtx, ecsx, dtex, csx = t["dtx"][:, ch], t["ecsx"][:, ch], t["dtex"][:, ch], t["csx"][:, ch]
            xd = xs * dtx
            csrow = _dot3_r(k["ones"], jnp.where(k["diag"], csx, 0.0), parts=2)
            lcat = jnp.where(k["low"], jnp.exp(jnp.minimum(csx - csrow, 0.0)), 0.0)
            ltcat = jnp.where(k["up"], jnp.exp(jnp.minimum(csrow - csx, 0.0)), 0.0)
            mcat = _dot(cm, _tile8(bm), "nt") * lcat
            mtcat = _dot(bm, _tile8(cm), "nt") * ltcat
            xdbd = jnp.where(k["block"], _tile8(xd), 0.0).astype(BF16)
            dybd = jnp.where(k["block"], _tile8(dyv), 0.0).astype(BF16)
            gq = _dot(dyv.astype(BF16), xdbd, "nt") * mcat
            gt = _dot(xd.astype(BF16), dybd, "nt")
            gqt = gt * mtcat
            dcbt = _dot((gt * ltcat).astype(BF16), jnp.where(k["diag"], 1.0, 0.0).astype(BF16), "nt")
            dcbt = dcbt.astype(BF16)
            sg = st_ref[ch, :]
            dsg = ds_ref[ch, :]
            sgb, dsgb = sg.astype(BF16), dsg.astype(BF16)
            yoff = _dot(cm, sgb, "nt") * ecsx
            dye = (dyv * ecsx).astype(BF16)
            xdd = xd * dtex
            dxbc_ref[:, csl] = _dot(dcbt, bm, "tn") + _dot(dye, sgb)
            dxbc_ref[:, bsl] = _dot(dcbt, cm) + _dot(xdd.astype(BF16), dsgb)
            bds = _dot(bm, dsgb, "nt")
            dxd = _dot(mtcat.astype(BF16), dybd) + dtex * bds
            wx = xdd * bds
            ds_ref[ch, :] = t["rows"][ch, :] * dsg + _dot(dye, cm, "tn")
            f1_ref[:, ch] = gq - gqt + dyv * yoff - wx
            f2_ref[:, ch] = wx
            f3_ref[:, ch] = dxd * xs
            f4_ref[:, ch] = dyv * xs
            dxbc_ref[:, ch] = dxd * dtx + dx_ref[:, ch] * dyv
            fs_ref[:, ch] = _dot3_r(ones8, dsg * sg, "nt", parts=2)
        fold = lambda v, parts: _dot3_l(v, ev, "nt", parts=parts)
        f2 = fold(f2_ref[...], 1)
        last_row = jnp.sum(f2, axis=0, keepdims=True) + t["ecs"][q - 1:q, :] * fold(fs_ref[...], 2)[0:1, :]
        dcs = fold(f1_ref[...], 2) + jnp.where(k["last"], last_row, 0.0)
        da = _dot3_r(k["trit"], dcs)
        ddt = da * t["a_neg"] + fold(f3_ref[...], 2)
        ddt_raw = ddt * _sigmoid(dt_ref[...] + bias_ref[...])
        ddt_ref[...] = ddt_raw
        dbias_ref[...] += jnp.sum(ddt_raw, axis=0, keepdims=True)
        dalog_ref[...] += jnp.sum(da * t["dt"], axis=0, keepdims=True) * t["a_neg"]
        dd_ref[...] += jnp.sum(fold(f4_ref[...], 1), axis=0, keepdims=True)

    rev = lambda c: (nc - 1 - c, 0)
    hv = pl.BlockSpec((1, HEAD_LANES), lambda c: (0, 0))
    return pl.pallas_call(
        body, name=name, grid=(nc,),
        in_specs=[pl.BlockSpec((q, SSM_INNER), rev), pl.BlockSpec((q, SSM_CONV_DIM), rev),
                  pl.BlockSpec((q, HEAD_LANES), rev),
                  pl.BlockSpec((None, SSM_INNER, ns), lambda c: (nc - 1 - c, 0, 0)),
                  hv, hv, pl.BlockSpec((1, SSM_INNER), lambda c: (0, 0)),
                  pl.BlockSpec((HEAD_LANES, SSM_INNER), lambda c: (0, 0)),
                  pl.BlockSpec((SSM_INNER, HEAD_LANES), lambda c: (0, 0))],
        out_specs=[pl.BlockSpec((q, SSM_CONV_DIM), rev), pl.BlockSpec((q, HEAD_LANES), rev), hv, hv, hv],
        out_shape=[jax.ShapeDtypeStruct((L, SSM_CONV_DIM), F32), jax.ShapeDtypeStruct((L, HEAD_LANES), F32)]
        + [jax.ShapeDtypeStruct((1, HEAD_LANES), F32)] * 3,
        scratch_shapes=[pltpu.VMEM((SSM_INNER, ns), F32)] + [pltpu.VMEM((q, SSM_INNER), F32)] * 4
        + [pltpu.VMEM((8, SSM_INNER), F32)],
        compiler_params=_params(("arbitrary",)),
    )(dy, xbc, dt_raw, states, dt_bias, a_log, d_x, e, et)


def _my_index():
    return 4 * lax.axis_index("x") + 2 * lax.axis_index("y") + lax.axis_index("c")


def _peer(k):
    return (lax.axis_index("x") ^ ((k >> 2) & 1), lax.axis_index("y") ^ ((k >> 1) & 1), lax.axis_index("c") ^ (k & 1))


def _all_gather(shard, *, by_rows, name):
    nl, a, b = shard.shape
    out_shape = (nl, N_DEV * a, b) if by_rows else (nl, N_DEV, a, b)

    def body(src_ref, out_ref, send_sems, recv_sems, local_sems):
        me = _my_index()

        def mine(j):
            if by_rows:
                return out_ref.at[j, pl.ds(pl.multiple_of(me * a, 16), a), :]
            return out_ref.at[j, me]

        local = [pltpu.make_async_copy(src_ref.at[j], mine(j), local_sems.at[j]) for j in range(nl)]
        for cp in local:
            cp.start()
        copies = []
        for j in range(nl):
            for k in range(1, N_DEV):
                cp = pltpu.make_async_remote_copy(
                    src_ref=src_ref.at[j], dst_ref=mine(j), send_sem=send_sems.at[j, k - 1],
                    recv_sem=recv_sems.at[j, k - 1], device_id=_peer(k), device_id_type=MESH_ID)
                cp.start()
                copies.append(cp)
        for cp in copies:
            cp.wait()
        for cp in local:
            cp.wait()

    return pl.pallas_call(
        body, name=name, in_specs=[_HBM], out_specs=_HBM,
        out_shape=jax.ShapeDtypeStruct(out_shape, shard.dtype),
        scratch_shapes=[pltpu.SemaphoreType.DMA((nl, N_DEV - 1)), pltpu.SemaphoreType.DMA((nl, N_DEV - 1)),
                        pltpu.SemaphoreType.DMA((nl,))],
    )(shard)


_EFFECT = pltpu.SideEffectType.DATAFLOW_SIDE_EFFECTING


def _blk(ref, i, rows):
    if len(ref.shape) == 3:
        return ref.at[i]
    return ref.at[pl.ds(pl.multiple_of(i * rows, 16), rows), :]


def _remote_copies(scatter, src_ref, land_ref, send_sems, recv_sems, idx):
    me = _my_index()
    out = []
    for k in range(1, N_DEV):
        if scatter:
            src, dst = _blk(src_ref, me ^ k, land_ref.shape[1]), land_ref.at[me]
        else:
            src, dst = src_ref, _blk(land_ref, me, src_ref.shape[0])
        sem = idx * (N_DEV - 1) + k - 1
        out.append(pltpu.make_async_remote_copy(
            src_ref=src, dst_ref=dst, send_sem=send_sems.at[sem], recv_sem=recv_sems.at[sem],
            device_id=_peer(k), device_id_type=MESH_ID))
    return out


def _own_copy(scatter, src_ref, land_ref, own_sems, idx):
    me = _my_index()
    if scatter:
        src, dst = _blk(src_ref, me, land_ref.shape[1]), land_ref.at[me]
    else:
        src, dst = src_ref, _blk(land_ref, me, src_ref.shape[0])
    return pltpu.make_async_copy(src, dst, own_sems.at[idx])


def _copies_start(srcs, land_shapes, *, scatter, after, name):
    n = len(srcs)
    n_after = len(after)

    def body(*refs):
        src_refs, land_refs = refs[:n], refs[n:2 * n]
        send_sems, recv_sems, own_sems = refs[2 * n + n_after:2 * n + n_after + 3]
        for i in range(n):
            for cp in _remote_copies(scatter, src_refs[i], land_refs[i], send_sems, recv_sems, i):
                cp.start()
            _own_copy(scatter, src_refs[i], land_refs[i], own_sems, i).start()
        refs[-1][...] = jnp.zeros_like(refs[-1])

    lands = [lax.empty(shp, t.dtype) for shp, t in zip(land_shapes, srcs)]
    arrays = [pltpu.with_memory_space_constraint(t, pltpu.HBM) for t in list(srcs) + lands]
    sems = pltpu.SemaphoreType.DMA((n * (N_DEV - 1),))
    res = pl.pallas_call(
        body, name=name,
        out_shape=(sems, sems, pltpu.SemaphoreType.DMA((n,)), *[pltpu.HBM(t.shape, t.dtype) for t in arrays],
                   jax.ShapeDtypeStruct((8, 128), F32)),
        in_specs=[_HBM] * (2 * n) + [_ANY] * n_after,
        out_specs=(_SEM, _SEM, _SEM, *[_HBM] * (2 * n), pl.BlockSpec(memory_space=pltpu.VMEM)),
        input_output_aliases={i: 3 + i for i in range(2 * n)},
        compiler_params=pltpu.CompilerParams(has_side_effects=_EFFECT),
    )(*arrays, *after)
    return (res[:3], res[3:3 + n], res[3 + n:3 + 2 * n]), res[-1]


def _copies_wait(started, *, scatter, after, name):
    sems, srcs, lands = started
    n = len(srcs)

    def body(*refs):
        src_refs, land_refs = refs[:n], refs[n:2 * n]
        s_sems, r_sems, o_sems = refs[2 * n:2 * n + 3]
        for i in range(n):
            for cp in _remote_copies(scatter, src_refs[i], land_refs[i], s_sems, r_sems, i):
                cp.wait_send()
                cp.wait_recv()
            _own_copy(scatter, src_refs[i], land_refs[i], o_sems, i).wait()

    arrays = list(srcs) + list(lands)
    res = pl.pallas_call(
        body, name=name,
        out_shape=tuple(pltpu.HBM(t.shape, t.dtype) for t in arrays),
        in_specs=[_HBM] * (2 * n) + [_SEM] * 3 + [_ANY] * len(after), out_specs=tuple([_HBM] * (2 * n)),
        input_output_aliases={i: i for i in range(2 * n)},
        compiler_params=pltpu.CompilerParams(has_side_effects=_EFFECT),
    )(*arrays, *sems, *after)
    return res[n:]


def _adamw_math(w, g, m, v):
    m = ADAM_B1 * m + (1.0 - ADAM_B1) * g
    v = ADAM_B2 * v + (1.0 - ADAM_B2) * (g * g)
    m_hat = m / (1.0 - ADAM_B1 ** ADAM_STEP)
    v_hat = v / (1.0 - ADAM_B2 ** ADAM_STEP)
    delta = -ADAM_LR * (m_hat / (jnp.sqrt(v_hat) + ADAM_EPS) + ADAM_WD * w)
    return delta, m, v


def _adamw(recvs, w, m, v, *, name):
    nl, R, C = w.shape
    tr = max(t for t in range(8, (128 if C > 1024 else 512) + 1, 8) if R % t == 0)

    def body(*refs):
        r_refs = refs[:nl]
        w_ref, m_ref, v_ref, g_ref, d_ref, nm_ref, nv_ref = refs[nl:]
        for layer in range(nl):
            @pl.when(pl.program_id(0) == layer)
            def _(r_ref=r_refs[layer]):
                g = r_ref[0].astype(F32)
                for s in range(1, N_DEV):
                    g = g + r_ref[s].astype(F32)
                delta, nm, nv = _adamw_math(w_ref[...], g, m_ref[...], v_ref[...])
                g_ref[...] = g
                d_ref[...] = delta
                nm_ref[...] = nm
                nv_ref[...] = nv

    def recv_spec(layer):
        return pl.BlockSpec((N_DEV, tr, C), lambda j, i: (0, jnp.where(j == layer, i, 0), 0))

    row = pl.BlockSpec((None, tr, C), lambda j, i: (j, i, 0))
    return pl.pallas_call(
        body, name=name, grid=(nl, R // tr),
        in_specs=[recv_spec(layer) for layer in range(nl)] + [row, row, row],
        out_specs=[row] * 4, out_shape=[jax.ShapeDtypeStruct((nl, R, C), F32)] * 4,
        compiler_params=_params(("parallel", "parallel")),
    )(*recvs, w, m, v)


def _sum_slots(g8, *, name):
    _, P, C = g8.shape

    def body(r_ref, o_ref):
        g = r_ref[0]
        for s in range(1, N_DEV):
            g = g + r_ref[s]
        o_ref[...] = g

    return pl.pallas_call(
        body, name=name, grid=(1,),
        in_specs=[pl.BlockSpec((N_DEV, P, C), lambda i: (0, 0, 0))],
        out_specs=pl.BlockSpec((P, C), lambda i: (0, 0)),
        out_shape=jax.ShapeDtypeStruct((P, C), F32),
        compiler_params=_params(("arbitrary",)),
    )(g8)


def _adamw_small(g, w, m, v, *, name):
    P, C = w.shape

    def body(g_ref, w_ref, m_ref, v_ref, d_ref, nm_ref, nv_ref):
        delta, nm, nv = _adamw_math(w_ref[...], g_ref[...], m_ref[...], v_ref[...])
        d_ref[...] = delta
        nm_ref[...] = nm
        nv_ref[...] = nv

    full = pl.BlockSpec((P, C), lambda i: (0, 0))
    return pl.pallas_call(
        body, name=name, grid=(1,), in_specs=[full] * 4, out_specs=[full] * 3,
        out_shape=[jax.ShapeDtypeStruct((P, C), F32)] * 3,
        compiler_params=_params(("arbitrary",)),
    )(g, w, m, v)


def _pack(arrays):
    flat = jnp.concatenate([a.reshape(-1) for a in arrays])
    pad = (-flat.shape[0]) % (8 * 128)
    return jnp.pad(flat, (0, pad)).reshape(-1, 128)


def _unpack(buf, shapes):
    flat = buf.reshape(-1)
    out, off = [], 0
    for shp in shapes:
        n = 1
        for s in shp:
            n *= s
        out.append(flat[off:off + n].reshape(shp))
        off += n
    return out


def _expand_matrices():
    h = lax.broadcasted_iota(jnp.int32, (HEAD_LANES, SSM_INNER), 0)
    col = lax.broadcasted_iota(jnp.int32, (HEAD_LANES, SSM_INNER), 1) // SSM_HEAD_DIM
    e = (h == col).astype(BF16)
    return e, e.T


def _pad_heads(v):
    return jnp.pad(v.reshape(1, -1), ((0, 0), (0, HEAD_LANES - v.shape[-1])))


SHARDED_SMALL = ("cv_w_dw", "ssm_w_conv", "ssm_b_conv", "ssm_norm_g", "ffn_w_dw")
REPLICATED_SMALL = ("norm_mix_g", "norm_ffn_g", "norm_final_g", "cv_b_in", "cv_b_dw", "cv_ln_g", "cv_ln_b",
                    "cv_b_out", "ssm_dt_bias", "ssm_a_log", "ssm_d", "ffn_b_dw")
SMALL = REPLICATED_SMALL + SHARDED_SMALL
BIG_COLS = ("cv_w_in", "ssm_w_in", "ffn_w_up")
BIG_ROWS = ("cv_w_out", "ssm_w_out", "ffn_w_down")
WEIGHTS = ("norm_mix_g", "norm_ffn_g", "norm_final_g", "cv_w_in", "cv_b_in", "cv_w_dw", "cv_b_dw", "cv_ln_g",
           "cv_ln_b", "cv_w_out", "cv_b_out", "ssm_w_in", "ssm_w_conv", "ssm_b_conv", "ssm_dt_bias", "ssm_a_log",
           "ssm_d", "ssm_norm_g", "ssm_w_out", "ffn_w_up", "ffn_w_dw", "ffn_b_dw", "ffn_w_down")


N_STAGES = 2 * DEPTH


def _stage_weights(si):
    i = si // 2
    if si % 2:
        return "ffn_w_up", "ffn_w_down", i
    return ("cv_w_in", "cv_w_out", i // 2) if i % 2 == 0 else ("ssm_w_in", "ssm_w_out", i // 2)


def _gather_small_weights(w):
    small_local = _pack([w[n] for n in SHARDED_SMALL])
    small8 = _all_gather(small_local[None], by_rows=False, name="gather_small")[0]
    full = {}
    per_dev = [_unpack(small8[d], [w[n].shape for n in SHARDED_SMALL]) for d in range(N_DEV)]
    for idx, n in enumerate(SHARDED_SMALL):
        full[n] = jnp.concatenate([per_dev[d][idx] for d in range(N_DEV)], axis=-1)
    for n in REPLICATED_SMALL:
        full[n] = w[n]
    return full


def _forward_backward(x, target, weights_of, full, emit):
    e, et = _expand_matrices()
    grads = {}

    saved = []
    for i in range(DEPTH):
        j = i // 2
        s = {"x_in": x}
        wc, wr, behind = weights_of(2 * i, x)
        h1 = _rms_fwd(x, full["norm_mix_g"][i][None], name=f"rms_mix_{i}")
        s["h1"] = h1
        if i % 2 == 0:
            u = _mm_cols_nn(h1, wc, bias=full["cv_b_in"][j][None], after=behind, name=f"cv_in_{i}")
            sact, c = _cv_mid_fwd(u, full["cv_w_dw"][j], full["cv_b_dw"][j][None], full["cv_ln_g"][j][None],
                                  full["cv_ln_b"][j][None], name=f"cv_mid_{i}")
            x = _mm_rows_nn(sact, wr, res=x, bias=full["cv_b_out"][j][None], name=f"cv_out_{i}")
            s.update(u=u, c=c, sact=sact)
        else:
            zx = _mm_cols_nn(h1, wc, after=behind, name=f"ssm_in_{i}")
            z, xpre, dt_raw = _ssm_unpad(zx, name=f"ssm_unpad_{i}")
            xbc = _ssm_conv_fwd(xpre, full["ssm_w_conv"][j], full["ssm_b_conv"][j][None], name=f"ssm_conv_{i}")
            hp = dict(dt_bias=_pad_heads(full["ssm_dt_bias"][j]), a_log=_pad_heads(full["ssm_a_log"][j]),
                      d_x=jnp.repeat(full["ssm_d"][j], SSM_HEAD_DIM)[None])
            y, states = _ssd_fwd(xbc, dt_raw, hp["dt_bias"], hp["a_log"], hp["d_x"], e, et, name=f"ssd_fwd_{i}")
            yn = _gnorm_fwd(y, z, full["ssm_norm_g"][j][None], name=f"gnorm_{i}")
            x = _mm_rows_nn(yn, wr, res=x, name=f"ssm_out_{i}")
            s.update(z=z, xpre=xpre, dt_raw=dt_raw, xbc=xbc, y=y, states=states, yn=yn, hp=hp)
        s["x_mid"] = x
        wc2, wr2, behind = weights_of(2 * i + 1, x)
        h2 = _rms_fwd(x, full["norm_ffn_g"][i][None], name=f"rms_ffn_{i}")
        u0 = _mm_cols_nn(h2, wc2, after=behind, halves=True, name=f"ffn_up_{i}")
        ffn_w = full["ffn_w_dw"][i].reshape(FFN_K, 2, FFN_HIDDEN).transpose(1, 0, 2)
        ffn_b = full["ffn_b_dw"][i].reshape(2, 1, FFN_HIDDEN)
        gact = _ffn_mid_fwd(u0, ffn_w, ffn_b, name=f"ffn_mid_{i}")
        x = _mm_rows_nn(gact, wr2, res=x, name=f"ffn_down_{i}")
        s.update(h2=h2, u0=u0, gact=gact, ffn_w=ffn_w, ffn_b=ffn_b, weights=(wc, wr, wc2, wr2))
        saved.append(s)

    dx, dxb, dg_final, loss = _final_loss(x, full["norm_final_g"][None], target, name="final_loss")
    grads["norm_final_g"] = dg_final[0]

    small_layers = {n: [None] * full[n].shape[0] for n in SMALL if n != "norm_final_g"}
    behind = None
    for i in reversed(range(DEPTH)):
        j = i // 2
        s = saved[i]
        wc, wr, wc2, wr2 = s["weights"]
        dgact = _mm_rows_nt(dxb, wr2, after=behind, name=f"ffn_down_dx_{i}")
        dwr2 = _mm_rows_tn(s["gact"], dxb, name=f"ffn_down_dw_{i}")
        du0, dw_dw, db_dw = _ffn_bwd(dgact, s["u0"], s["ffn_w"], s["ffn_b"], name=f"ffn_bwd_{i}")
        small_layers["ffn_w_dw"][i] = dw_dw.transpose(1, 0, 2).reshape(FFN_K, 2 * FFN_HIDDEN)
        small_layers["ffn_b_dw"][i] = db_dw.reshape(2 * FFN_HIDDEN)
        dh2 = _mm_cols_nt(du0, wc2, name=f"ffn_up_dx_{i}")
        dwc2 = _mm_cols_tn(s["h2"], du0, name=f"ffn_up_dw_{i}")
        dx, dxb, dg, colsum = _rms_bwd(s["x_mid"], full["norm_ffn_g"][i][None], dh2, dx, name=f"rms_ffn_bwd_{i}")
        small_layers["norm_ffn_g"][i] = dg[0]
        behind = emit(2 * i + 1, dwc2, dwr2)
        if i % 2 == 0:
            small_layers["cv_b_out"][j] = colsum[0]
            dsact = _mm_rows_nt(dxb, wr, after=behind, name=f"cv_out_dx_{i}")
            dwr = _mm_rows_tn(s["sact"], dxb, name=f"cv_out_dw_{i}")
            du, dw_dw, db_dw, dlg, dlb, db_in = _cv_bwd(dsact, s["c"], s["u"], full["cv_w_dw"][j], full["cv_ln_g"][j][None],
                                                        full["cv_ln_b"][j][None], name=f"cv_bwd_{i}")
            small_layers["cv_w_dw"][j] = dw_dw
            small_layers["cv_b_dw"][j] = db_dw[0]
            small_layers["cv_ln_g"][j] = dlg[0]
            small_layers["cv_ln_b"][j] = dlb[0]
            small_layers["cv_b_in"][j] = db_in[0]
            dh1 = _mm_cols_nt(du, wc, name=f"cv_in_dx_{i}")
            dwc = _mm_cols_tn(s["h1"], du, name=f"cv_in_dw_{i}")
        else:
            hp = s["hp"]
            dyn = _mm_rows_nt(dxb, wr, after=behind, name=f"ssm_out_dx_{i}")
            dwr = _mm_rows_tn(s["yn"], dxb, name=f"ssm_out_dw_{i}")
            dy, dz, dng = _gnorm_bwd(dyn, s["y"], s["z"], full["ssm_norm_g"][j][None], name=f"gnorm_bwd_{i}")
            small_layers["ssm_norm_g"][j] = dng[0]
            dxbc, ddt_raw, dbias, dalog, dd = _ssd_bwd(dy, s["xbc"], s["dt_raw"], s["states"], hp["dt_bias"], hp["a_log"],
                                                      hp["d_x"], e, et, name=f"ssd_bwd_{i}")
            small_layers["ssm_dt_bias"][j] = dbias[0, :SSM_HEADS]
            small_layers["ssm_a_log"][j] = dalog[0, :SSM_HEADS]
            small_layers["ssm_d"][j] = dd[0, :SSM_HEADS]
            dxpre, dw_conv, db_conv = _ssm_conv_bwd(dxbc, s["xpre"], full["ssm_w_conv"][j], full["ssm_b_conv"][j][None],
                                                    name=f"ssm_conv_bwd_{i}")
            small_layers["ssm_w_conv"][j] = dw_conv
            small_layers["ssm_b_conv"][j] = db_conv[0]
            dzx = _ssm_pad(dz, dxpre, ddt_raw, name=f"ssm_pad_{i}")
            dh1 = _mm_cols_nt(dzx, wc, name=f"ssm_in_dx_{i}")
            dwc = _mm_cols_tn(s["h1"], dzx, name=f"ssm_in_dw_{i}")
        dx, dxb, dg, _ = _rms_bwd(s["x_in"], full["norm_mix_g"][i][None], dh1, dx, name=f"rms_mix_bwd_{i}")
        small_layers["norm_mix_g"][i] = dg[0]
        behind = emit(2 * i, dwc, dwr)
    for n, layers in small_layers.items():
        grads[n] = jnp.stack(layers)
    return loss, dx, grads


def _update_big(names, recvs, w, m, v):
    out = {}
    for n in names:
        b = recvs[n][0].shape[-1]
        pad = ((0, 0), (0, 0), (0, b - w[n].shape[-1]))
        res = _adamw(recvs[n], *[jnp.pad(t, pad) for t in (w[n], m[n], v[n])], name="adamw_" + n)
        out[n] = [r[..., :w[n].shape[-1]] for r in res]
    return out


def _update_small(grads, w, m, v):
    me = _my_index()
    out = {}
    small_part = _pack([grads[n] for n in SMALL])
    small_all = _all_gather(small_part[None], by_rows=False, name="gather_small_grads")[0]
    small_sum = _sum_slots(small_all, name="sum_small_grads")
    gfull = dict(zip(SMALL, _unpack(small_sum, [grads[n].shape for n in SMALL])))
    glocal = []
    for n in SMALL:
        g = gfull[n]
        if n in SHARDED_SMALL:
            cols = w[n].shape[-1]
            g = lax.dynamic_slice_in_dim(g, me * cols, cols, axis=g.ndim - 1)
        glocal.append(g)
    packed = [_pack(glocal)] + [_pack([t[n] for n in SMALL]) for t in (w, m, v)]
    res = _adamw_small(*packed, name="adamw_small")
    shapes = [w[n].shape for n in SMALL]
    unpacked = [_unpack(r, shapes) for r in res]
    for idx, n in enumerate(SMALL):
        out[n] = [glocal[idx]] + [u[idx] for u in unpacked]
    return out


GATHER_AHEAD = 3


def _train_step(x, target, w, m, v):
    full = _gather_small_weights(w)
    gathers, exchanges = {}, {}
    last_start = [jnp.zeros((8, 128), F32)]

    def start_gather(si, after):
        cn, rn, j = _stage_weights(si)
        wc, wr = w[cn][j].astype(BF16), w[rn][j].astype(BF16)
        if cn == "ssm_w_in":
            wc = jnp.pad(wc, ((0, 0), (0, SSM_SLAB_PAD - SSM_SLAB)))
        shapes = [(N_DEV,) + wc.shape, (N_DEV * wr.shape[0], wr.shape[1])]
        gathers[si], last_start[0] = _copies_start([wc, wr], shapes, scatter=False, after=after,
                                                   name=f"gather_start_{si}")

    for si in range(GATHER_AHEAD):
        start_gather(si, [last_start[0]])

    def weights_of(si, x_act):
        lc, lr = _copies_wait(gathers.pop(si), scatter=False, after=[x_act, last_start[0]], name=f"gather_wait_{si}")
        behind = None
        if si + GATHER_AHEAD < N_STAGES:
            start_gather(si + GATHER_AHEAD, [last_start[0], lc])
            behind = last_start[0]
        return lc, lr, behind

    def emit(si, dwc, dwr):
        shapes = [dwc.shape, (N_DEV, dwr.shape[0] // N_DEV, dwr.shape[1])]
        exchanges[si], token = _copies_start([dwc, dwr], shapes, scatter=True, after=[], name=f"exchange_start_{si}")
        return token

    loss, dx, grads = _forward_backward(x, target, weights_of, full, emit)

    out = _update_small(grads, w, m, v)
    recvs = {n: [None] * w[n].shape[0] for n in BIG_COLS + BIG_ROWS}
    behind = [dx, out[SMALL[0]][1]]
    for si in reversed(range(N_STAGES)):
        cn, rn, j = _stage_weights(si)
        if si == 0:
            out.update(_update_big([n for n in BIG_COLS + BIG_ROWS if not n.startswith("cv_")], recvs, w, m, v))
            behind = [out["ffn_w_up"][1], out["ffn_w_down"][1], out["ssm_w_in"][1], out["ssm_w_out"][1]]
        recvs[cn][j], recvs[rn][j] = _copies_wait(exchanges.pop(si), scatter=True, after=behind,
                                                  name=f"exchange_wait_{si}")
        behind = [recvs[cn][j]]
    out.update(_update_big(["cv_w_in", "cv_w_out"], recvs, w, m, v))
    return lax.psum(loss[0, 0], AXES), dx, out


def kernel(x, norm_mix_g, norm_ffn_g, norm_final_g, cv_w_in, cv_b_in, cv_w_dw, cv_b_dw, cv_ln_g, cv_ln_b, cv_w_out, cv_b_out, ssm_w_in, ssm_w_conv, ssm_b_conv, ssm_dt_bias, ssm_a_log, ssm_d, ssm_norm_g, ssm_w_out, ffn_w_up, ffn_w_dw, ffn_b_dw, ffn_w_down, loss_target, m_norm_mix_g, m_norm_ffn_g, m_norm_final_g, m_cv_w_in, m_cv_b_in, m_cv_w_dw, m_cv_b_dw, m_cv_ln_g, m_cv_ln_b, m_cv_w_out, m_cv_b_out, m_ssm_w_in, m_ssm_w_conv, m_ssm_b_conv, m_ssm_dt_bias, m_ssm_a_log, m_ssm_d, m_ssm_norm_g, m_ssm_w_out, m_ffn_w_up, m_ffn_w_dw, m_ffn_b_dw, m_ffn_w_down, v_norm_mix_g, v_norm_ffn_g, v_norm_final_g, v_cv_w_in, v_cv_b_in, v_cv_w_dw, v_cv_b_dw, v_cv_ln_g, v_cv_ln_b, v_cv_w_out, v_cv_b_out, v_ssm_w_in, v_ssm_w_conv, v_ssm_b_conv, v_ssm_dt_bias, v_ssm_a_log, v_ssm_d, v_ssm_norm_g, v_ssm_w_out, v_ffn_w_up, v_ffn_w_dw, v_ffn_b_dw, v_ffn_w_down):
    args = locals()
    w = {n: args[n] for n in WEIGHTS}
    m = {n: args["m_" + n] for n in WEIGHTS}
    v = {n: args["v_" + n] for n in WEIGHTS}
    loss, dx, out = _train_step(x[0], loss_target[0], w, m, v)
    return (loss, dx[None], *[out[n][0] for n in WEIGHTS], *[out[n][1] for n in WEIGHTS],
            *[out[n][2] for n in WEIGHTS], *[out[n][3] for n in WEIGHTS])
```

```python
import functools

import jax
import jax.numpy as jnp
from jax import lax
from jax.experimental import pallas as pl
from jax.experimental.pallas import tpu as pltpu

F32, BF16 = jnp.float32, jnp.bfloat16
AXES = ("x", "y", "c")
N_DEV = 8
MESH_ID = pl.DeviceIdType.MESH

D_MODEL = 2048
DEPTH = 4
CHUNK = 64
CONV_K = 31
SSM_INNER = 4096
SSM_HEADS = 64
SSM_HEAD_DIM = 64
SSM_GROUPS = 8
SSM_GROUP_W = SSM_INNER // SSM_GROUPS
SSM_STATE = 128
SSM_CONV_K = 4
SSM_CONV_DIM = SSM_INNER + 2 * SSM_GROUPS * SSM_STATE
SSM_IN_DIM = SSM_INNER + SSM_CONV_DIM + SSM_HEADS
SSM_SLAB = SSM_IN_DIM // N_DEV
SSM_SLAB_PAD = 1408
SSM_COMPACT = 1280 * (N_DEV - 1) + SSM_SLAB_PAD
FFN_HIDDEN = 5632
FFN_K = 3
HEAD_LANES = 128
RMS_EPS = 1e-6
LN_EPS = 1e-5
ADAM_LR, ADAM_B1, ADAM_B2, ADAM_EPS, ADAM_WD, ADAM_STEP = 0.001, 0.9, 0.999, 1e-08, 0.01, 10

_DIMS = {
    "nn": (((1,), (0,)), ((), ())),
    "nt": (((1,), (1,)), ((), ())),
    "tn": (((0,), (0,)), ((), ())),
}


def _params(sem=None, vmem_mb=48):
    return pltpu.CompilerParams(dimension_semantics=sem, vmem_limit_bytes=vmem_mb << 20)


_HBM = pl.BlockSpec(memory_space=pltpu.HBM)
_ANY = pl.BlockSpec(memory_space=pl.ANY)
_SEM = pl.BlockSpec(memory_space=pltpu.SEMAPHORE)


def _dot(a, b, mode="nn"):
    return lax.dot_general(a, b, _DIMS[mode], preferred_element_type=F32)


def _split3(x):
    hi = x.astype(BF16)
    r1 = x - hi.astype(F32)
    mid = r1.astype(BF16)
    lo = (r1 - mid.astype(F32)).astype(BF16)
    return hi, mid, lo


def _dot3_l(x, m, mode="nn", parts=3):
    return sum(_dot(p, m, mode) for p in _split3(x)[:parts])


def _dot3_r(m, x, mode="nn", parts=3):
    return sum(_dot(m, p, mode) for p in _split3(x)[:parts])


def _sigmoid(x):
    return jax.nn.sigmoid(x)


def _dsilu(x, sg):
    return sg * (1.0 + x * (1.0 - sg))


def _softplus(x):
    return jnp.maximum(x, 0.0) + jnp.log(1.0 + jnp.exp(-jnp.abs(x)))


def _tile(n, pref):
    return min(n, pref)


def _matmul(a, b, *, mode, grid, a_spec, b_spec, o_spec, o_block, out_shape, nk, name,
            extras=(), epilogue=None, after=None, vmem_mb=48):
    n_extra = len(extras)
    if after is not None:
        extras = tuple(extras) + ((after, _ANY),)
    n_in = len(extras)

    def body(a_ref, b_ref, *rest):
        extra_refs = rest[:n_extra]
        o_ref = rest[n_in]
        if len(b_ref.shape) == 3:
            n = b_ref.shape[-1]
            part = (_dot(a_ref[:, 0:n].astype(BF16), b_ref[0].astype(BF16), mode)
                    + _dot(a_ref[:, n:2 * n].astype(BF16), b_ref[1].astype(BF16), mode))
        else:
            part = _dot(a_ref[...].astype(BF16), b_ref[...].astype(BF16), mode)

        def finish(acc):
            if epilogue is not None:
                acc = epilogue(acc, *[r[...] for r in extra_refs])
            o_ref[...] = acc.astype(o_ref.dtype)

        if nk == 1:
            finish(part)
        else:
            acc_ref = rest[n_in + 1]
            k = pl.program_id(len(grid) - 1)

            @pl.when(k == 0)
            def _():
                acc_ref[...] = part

            @pl.when(k > 0)
            def _():
                acc_ref[...] += part

            @pl.when(k == nk - 1)
            def _():
                finish(acc_ref[...])

    scratch = [] if nk == 1 else [pltpu.VMEM(o_block, F32)]
    sem = ("parallel",) * (len(grid) - 1) + ("arbitrary",)
    return pl.pallas_call(
        body, name=name, grid=grid,
        in_specs=[a_spec, b_spec] + [s for _, s in extras],
        out_specs=o_spec, out_shape=out_shape, scratch_shapes=scratch,
        compiler_params=_params(sem, vmem_mb),
    )(a, b, *[x for x, _ in extras])


HALF_SLABS = N_DEV // 2


def _mm_cols_nn(h, wg, *, name, bias=None, after=None, halves=False):
    L, K = h.shape
    n = wg.shape[-1]
    tm = _tile(L, 512)
    extras, epi = (), None
    if bias is not None:
        extras = ((bias, pl.BlockSpec((1, n), lambda s, i: (0, s))),)
        epi = lambda acc, b: acc + b
    if halves:
        o_spec = pl.BlockSpec((None, tm, n), lambda s, i: (s // HALF_SLABS, i, s % HALF_SLABS))
        out_shape = jax.ShapeDtypeStruct((2, L, HALF_SLABS * n), F32)
    else:
        o_spec = pl.BlockSpec((tm, n), lambda s, i: (i, s))
        out_shape = jax.ShapeDtypeStruct((L, N_DEV * n), F32)
    return _matmul(
        h, wg, mode="nn", grid=(N_DEV, L // tm),
        a_spec=pl.BlockSpec((tm, K), lambda s, i: (i, 0)),
        b_spec=pl.BlockSpec((None, K, n), lambda s, i: (s, 0, 0)),
        o_spec=o_spec, o_block=(tm, n), out_shape=out_shape, nk=1, name=name,
        extras=extras, epilogue=epi, after=after)


def _mm_cols_nt(du, wg, *, name):
    L = du.shape[-2]
    K, n = wg.shape[-2:]
    tm = _tile(L, 512)
    pairs = HALF_SLABS // 2
    if du.ndim == 3:
        a_spec = pl.BlockSpec((None, tm, 2 * n), lambda i, s: (s // pairs, i, s % pairs))
    else:
        a_spec = pl.BlockSpec((tm, 2 * n), lambda i, s: (i, s))
    return _matmul(
        du, wg, mode="nt", grid=(L // tm, N_DEV // 2), a_spec=a_spec,
        b_spec=pl.BlockSpec((2, K, n), lambda i, s: (s, 0, 0)),
        o_spec=pl.BlockSpec((tm, K), lambda i, s: (i, 0)), o_block=(tm, K),
        out_shape=jax.ShapeDtypeStruct((L, K), F32), nk=N_DEV // 2, name=name)


def _mm_cols_tn(h, du, *, name):
    L, K = h.shape
    tm = _tile(L, 512)
    if du.ndim == 3:
        n = du.shape[2] // HALF_SLABS
        b_spec = pl.BlockSpec((None, L, n), lambda s, q: (s // HALF_SLABS, 0, s % HALF_SLABS))
    else:
        n = du.shape[1] // N_DEV
        b_spec = pl.BlockSpec((L, n), lambda s, q: (0, s))
    tq = 256 if n > 1024 else 512
    return _matmul(
        h, du, mode="tn", grid=(N_DEV, K // tq),
        a_spec=pl.BlockSpec((L, tq), lambda s, q: (0, q)), b_spec=b_spec,
        o_spec=pl.BlockSpec((None, tq, n), lambda s, q: (s, q, 0)), o_block=(tq, n),
        out_shape=jax.ShapeDtypeStruct((N_DEV, K, n), BF16), nk=1, name=name)


def _mm_rows_nn(a, wg, *, res, name, bias=None):
    L, Kw = a.shape
    N = wg.shape[-1]
    tm, tn = _tile(L, 512), 512
    extras = [(res, pl.BlockSpec((tm, tn), lambda j, i: (i, j)))]
    if bias is not None:
        extras.append((bias, pl.BlockSpec((1, tn), lambda j, i: (0, j))))
        epi = lambda acc, r, b: acc + r + b
    else:
        epi = lambda acc, r: acc + r
    return _matmul(
        a, wg, mode="nn", grid=(N // tn, L // tm),
        a_spec=pl.BlockSpec((tm, Kw), lambda j, i: (i, 0)),
        b_spec=pl.BlockSpec((Kw, tn), lambda j, i: (0, j)),
        o_spec=pl.BlockSpec((tm, tn), lambda j, i: (i, j)), o_block=(tm, tn),
        out_shape=jax.ShapeDtypeStruct((L, N), F32), nk=1, name=name,
        extras=tuple(extras), epilogue=epi)


def _mm_rows_nt(dy, wg, *, name, after=None):
    L, N = dy.shape
    Kw = wg.shape[-2]
    tm, tn = _tile(L, 512), 512
    return _matmul(
        dy, wg, mode="nt", grid=(L // tm, Kw // tn),
        a_spec=pl.BlockSpec((tm, N), lambda i, q: (i, 0)),
        b_spec=pl.BlockSpec((tn, N), lambda i, q: (q, 0)),
        o_spec=pl.BlockSpec((tm, tn), lambda i, q: (i, q)), o_block=(tm, tn),
        out_shape=jax.ShapeDtypeStruct((L, Kw), F32), nk=1, name=name, after=after)


def _mm_rows_tn(a, dy, *, name):
    L, Kw = a.shape
    N = dy.shape[1]
    tq, tn = 512, 1024
    return _matmul(
        a, dy, mode="tn", grid=(N // tn, Kw // tq),
        a_spec=pl.BlockSpec((L, tq), lambda p, q: (0, q)),
        b_spec=pl.BlockSpec((L, tn), lambda p, q: (0, p)),
        o_spec=pl.BlockSpec((tq, tn), lambda p, q: (q, p)), o_block=(tq, tn),
        out_shape=jax.ShapeDtypeStruct((Kw, N), BF16), nk=1, name=name)


def _rms_fwd(x, g, *, name):
    L, Dm = x.shape
    tm = _tile(L, 256)

    def body(x_ref, g_ref, h_ref):
        xv = x_ref[...]
        r = lax.rsqrt(jnp.mean(xv * xv, axis=-1, keepdims=True) + RMS_EPS)
        h_ref[...] = (xv * r * g_ref[...]).astype(BF16)

    return pl.pallas_call(
        body, name=name, grid=(L // tm,),
        in_specs=[pl.BlockSpec((tm, Dm), lambda i: (i, 0)), pl.BlockSpec((1, Dm), lambda i: (0, 0))],
        out_specs=pl.BlockSpec((tm, Dm), lambda i: (i, 0)),
        out_shape=jax.ShapeDtypeStruct((L, Dm), BF16),
        compiler_params=_params(("parallel",)),
    )(x, g)


def _rms_bwd(x, g, dh, dres, *, name):
    L, Dm = x.shape
    tm = _tile(L, 256)

    def body(x_ref, g_ref, dh_ref, dres_ref, dx_ref, dxb_ref, dg_ref, cs_ref):
        i = pl.program_id(0)
        xv = x_ref[...]
        r = lax.rsqrt(jnp.mean(xv * xv, axis=-1, keepdims=True) + RMS_EPS)
        xh = xv * r
        dh = dh_ref[...]
        dxh = dh * g_ref[...]
        dx = dres_ref[...] + r * (dxh - xh * jnp.mean(dxh * xh, axis=-1, keepdims=True))
        dx_ref[...] = dx
        dxb_ref[...] = dx.astype(BF16)

        @pl.when(i == 0)
        def _():
            dg_ref[...] = jnp.zeros_like(dg_ref)
            cs_ref[...] = jnp.zeros_like(cs_ref)

        dg_ref[...] += jnp.sum(dh * xh, axis=0, keepdims=True)
        cs_ref[...] += jnp.sum(dx, axis=0, keepdims=True)

    row = pl.BlockSpec((tm, Dm), lambda i: (i, 0))
    vec = pl.BlockSpec((1, Dm), lambda i: (0, 0))
    return pl.pallas_call(
        body, name=name, grid=(L // tm,),
        in_specs=[row, vec, row, row], out_specs=[row, row, vec, vec],
        out_shape=[jax.ShapeDtypeStruct((L, Dm), F32), jax.ShapeDtypeStruct((L, Dm), BF16),
                   jax.ShapeDtypeStruct((1, Dm), F32), jax.ShapeDtypeStruct((1, Dm), F32)],
        compiler_params=_params(("arbitrary",)),
    )(x, g, dh, dres)


def _final_loss(x, g, target, *, name):
    L, Dm = x.shape
    tm = _tile(L, 256)

    def body(x_ref, g_ref, t_ref, dx_ref, dxb_ref, dg_ref, loss_ref):
        i = pl.program_id(0)
        xv = x_ref[...]
        gv = g_ref[...]
        r = lax.rsqrt(jnp.mean(xv * xv, axis=-1, keepdims=True) + RMS_EPS)
        xh = xv * r
        err = xh * gv - t_ref[...]
        dy = err * (1.0 / Dm)
        dxh = dy * gv
        dx = r * (dxh - xh * jnp.mean(dxh * xh, axis=-1, keepdims=True))
        dx_ref[...] = dx
        dxb_ref[...] = dx.astype(BF16)

        @pl.when(i == 0)
        def _():
            dg_ref[...] = jnp.zeros_like(dg_ref)
            loss_ref[...] = jnp.zeros_like(loss_ref)

        dg_ref[...] += jnp.sum(dy * xh, axis=0, keepdims=True)
        loss_ref[...] += 0.5 * jnp.sum(jnp.mean(err * err, axis=-1, keepdims=True), axis=0, keepdims=True)

    row = pl.BlockSpec((tm, Dm), lambda i: (i, 0))
    vec = pl.BlockSpec((1, Dm), lambda i: (0, 0))
    return pl.pallas_call(
        body, name=name, grid=(L // tm,),
        in_specs=[row, vec, row],
        out_specs=[row, row, vec, pl.BlockSpec((1, 128), lambda i: (0, 0))],
        out_shape=[jax.ShapeDtypeStruct((L, Dm), F32), jax.ShapeDtypeStruct((L, Dm), BF16),
                   jax.ShapeDtypeStruct((1, Dm), F32), jax.ShapeDtypeStruct((1, 128), F32)],
        compiler_params=_params(("arbitrary",)),
    )(x, g, target)


def _conv_from_ext(ext_ref, w_ref, bias, taps, halo, rows):
    acc = jnp.broadcast_to(bias, (rows, ext_ref.shape[1]))
    for j in range(taps):
        acc = acc + w_ref[taps - 1 - j:taps - j, :] * ext_ref[halo - j:halo - j + rows, :]
    return acc


def _conv_back(dext_ref, x_cur, w_ref, dw_ref, taps, rows):
    dx = jnp.zeros((rows, dext_ref.shape[1]), F32)
    for j in range(taps):
        sh = dext_ref[j:j + rows, :]
        dx = dx + w_ref[taps - 1 - j:taps - j, :] * sh
        dw_ref[taps - 1 - j:taps - j, :] += jnp.sum(x_cur * sh, axis=0, keepdims=True)
    return dx


STRIP = 128
ROW_BLOCK = 64


def _strips(width):
    return [slice(s, s + STRIP) for s in range(0, width, STRIP)]


def _row_blocks(n):
    return [(lo, min(ROW_BLOCK, n - lo)) for lo in range(0, n, ROW_BLOCK)]


def _conv_strip(ext_ref, taps, bias, halo, lo, n, ls):
    acc = jnp.broadcast_to(bias, (n, STRIP))
    for j in range(len(taps)):
        acc = acc + taps[len(taps) - 1 - j] * ext_ref[halo + lo - j:halo + lo - j + n, ls]
    return acc


def _conv_back_strip(dext_ref, x_blk, taps, dw_acc, lo, n, ls):
    dx = jnp.zeros((n, STRIP), F32)
    for j in range(len(taps)):
        k = len(taps) - 1 - j
        sh = dext_ref[lo + j:lo + j + n, ls]
        dx = dx + taps[k] * sh
        dw_acc[k] = dw_acc[k] + jnp.sum(x_blk * sh, axis=0, keepdims=True)
    return dx


def _prev_blk(i, r):
    return jnp.maximum(i * r - 1, 0)


def _next_blk(i, r, nblk):
    return jnp.minimum((i + 1) * r, nblk - 1)


CV_HALO = 32
SUBLANES = 8
LN_ROWS = 16


def _cv_mid_fwd(u, w_dw, b_dw, ln_g, ln_b, *, name):
    L = u.shape[0]
    Dm = D_MODEL
    tm = _tile(L, 256)
    r = tm // CV_HALO

    def body(a_ref, g_ref, ah_ref, gh_ref, w_ref, bdw_ref, lg_ref, lb_ref, s_ref, c_ref, ext_ref):
        i = pl.program_id(0)
        keep = (i > 0).astype(F32)
        ext_ref[0, 0:CV_HALO, :] = ah_ref[...] * _sigmoid(gh_ref[...]) * keep
        ext_ref[0, CV_HALO:CV_HALO + tm, :] = a_ref[...] * _sigmoid(g_ref[...])
        for b in range(1, SUBLANES):
            ext_ref[b, SUBLANES:CV_HALO + tm, :] = ext_ref[0, SUBLANES - b:CV_HALO + tm - b, :]
        for ls in _strips(Dm):
            for lo, n in _row_blocks(tm):
                acc = jnp.broadcast_to(bdw_ref[:, ls], (n, STRIP))
                for j in range(CONV_K):
                    a8, b = divmod(j, SUBLANES)
                    top = CV_HALO + lo - SUBLANES * a8
                    acc = acc + w_ref[CONV_K - 1 - j:CONV_K - j, ls] * ext_ref[b, top:top + n, ls]
                c_ref[lo:lo + n, ls] = acc
        for lo in range(0, tm, LN_ROWS):
            c = c_ref[lo:lo + LN_ROWS, :]
            xc = c - jnp.mean(c, axis=-1, keepdims=True)
            rstd = lax.rsqrt(jnp.mean(xc * xc, axis=-1, keepdims=True) + LN_EPS)
            l = xc * rstd * lg_ref[...] + lb_ref[...]
            s_ref[lo:lo + LN_ROWS, :] = (l * _sigmoid(l)).astype(BF16)

    vec = pl.BlockSpec((1, Dm), lambda i: (0, 0))
    return pl.pallas_call(
        body, name=name, grid=(L // tm,),
        in_specs=[pl.BlockSpec((tm, Dm), lambda i: (i, 0)), pl.BlockSpec((tm, Dm), lambda i: (i, 1)),
                  pl.BlockSpec((CV_HALO, Dm), lambda i: (_prev_blk(i, r), 0)),
                  pl.BlockSpec((CV_HALO, Dm), lambda i: (_prev_blk(i, r), 1)),
                  pl.BlockSpec((CONV_K, Dm), lambda i: (0, 0)), vec, vec, vec],
        out_specs=[pl.BlockSpec((tm, Dm), lambda i: (i, 0)), pl.BlockSpec((tm, Dm), lambda i: (i, 0))],
        out_shape=[jax.ShapeDtypeStruct((L, Dm), BF16), jax.ShapeDtypeStruct((L, Dm), F32)],
        scratch_shapes=[pltpu.VMEM((SUBLANES, CV_HALO + tm, Dm), F32)],
        compiler_params=_params(("parallel",)),
    )(u, u, u, u, w_dw, b_dw, ln_g, ln_b)


def _cv_bwd(ds, c, u, w_dw, ln_g, ln_b, *, name):
    L = u.shape[0]
    Dm = D_MODEL
    tm = _tile(L, 128)
    r = tm // CV_HALO
    nt = L // tm
    nblk = L // CV_HALO
    ext_rows = tm + CV_HALO

    def body(ds_ref, dsn_ref, c_ref, cn_ref, a_ref, g_ref, w_ref, lg_ref, lb_ref,
             du_ref, dw_ref, dbdw_ref, dlg_ref, dlb_ref, dbin_ref, dext_ref):
        i = pl.program_id(0)

        @pl.when(i == 0)
        def _():
            for ref in (dw_ref, dbdw_ref, dlg_ref, dlb_ref, dbin_ref):
                ref[...] = jnp.zeros_like(ref)

        keep = (i < nt - 1).astype(F32)
        for lo in range(0, ext_rows, LN_ROWS):
            if lo < tm:
                dsv, cv = ds_ref[lo:lo + LN_ROWS, :], c_ref[lo:lo + LN_ROWS, :]
            else:
                dsv, cv = dsn_ref[lo - tm:lo - tm + LN_ROWS, :] * keep, cn_ref[lo - tm:lo - tm + LN_ROWS, :]
            xc = cv - jnp.mean(cv, axis=-1, keepdims=True)
            rstd = lax.rsqrt(jnp.mean(xc * xc, axis=-1, keepdims=True) + LN_EPS)
            nrm = xc * rstd
            lg = lg_ref[...]
            l = nrm * lg + lb_ref[...]
            dl = dsv * _dsilu(l, _sigmoid(l))
            dn = dl * lg
            dc = rstd * (dn - jnp.mean(dn, axis=-1, keepdims=True)
                         - nrm * jnp.mean(dn * nrm, axis=-1, keepdims=True))
            dext_ref[0, lo:lo + LN_ROWS, :] = dc
            if lo < tm:
                dlg_ref[...] += jnp.sum(dl * nrm, axis=0, keepdims=True)
                dlb_ref[...] += jnp.sum(dl, axis=0, keepdims=True)
                dbdw_ref[...] += jnp.sum(dc, axis=0, keepdims=True)

        for b in range(1, SUBLANES):
            dext_ref[b, 0:ext_rows - SUBLANES, :] = dext_ref[0, b:ext_rows - SUBLANES + b, :]
        for ls in _strips(Dm):
            gs = slice(Dm + ls.start, Dm + ls.stop)
            for lo, n in _row_blocks(tm):
                av, sg = a_ref[lo:lo + n, ls], _sigmoid(g_ref[lo:lo + n, ls])
                v = av * sg
                dv = jnp.zeros((n, STRIP), F32)
                for j in range(CONV_K):
                    k = CONV_K - 1 - j
                    a8, b = divmod(j, SUBLANES)
                    sh = dext_ref[b, lo + SUBLANES * a8:lo + SUBLANES * a8 + n, ls]
                    dv = dv + w_ref[k:k + 1, ls] * sh
                    dw_ref[k:k + 1, ls] += jnp.sum(v * sh, axis=0, keepdims=True)
                da = dv * sg
                dgate = dv * av * sg * (1.0 - sg)
                du_ref[lo:lo + n, ls] = da.astype(BF16)
                du_ref[lo:lo + n, gs] = dgate.astype(BF16)
                dbin_ref[:, ls] += jnp.sum(da, axis=0, keepdims=True)
                dbin_ref[:, gs] += jnp.sum(dgate, axis=0, keepdims=True)

    row = pl.BlockSpec((tm, Dm), lambda i: (i, 0))
    nxt = pl.BlockSpec((CV_HALO, Dm), lambda i: (_next_blk(i, r, nblk), 0))
    vec = pl.BlockSpec((1, Dm), lambda i: (0, 0))
    return pl.pallas_call(
        body, name=name, grid=(nt,),
        in_specs=[row, nxt, row, nxt, row, pl.BlockSpec((tm, Dm), lambda i: (i, 1)),
                  pl.BlockSpec((CONV_K, Dm), lambda i: (0, 0)), vec, vec],
        out_specs=[pl.BlockSpec((tm, 2 * Dm), lambda i: (i, 0)), pl.BlockSpec((CONV_K, Dm), lambda i: (0, 0)),
                   vec, vec, vec, pl.BlockSpec((1, 2 * Dm), lambda i: (0, 0))],
        out_shape=[jax.ShapeDtypeStruct((L, 2 * Dm), BF16), jax.ShapeDtypeStruct((CONV_K, Dm), F32),
                   jax.ShapeDtypeStruct((1, Dm), F32), jax.ShapeDtypeStruct((1, Dm), F32),
                   jax.ShapeDtypeStruct((1, Dm), F32), jax.ShapeDtypeStruct((1, 2 * Dm), F32)],
        scratch_shapes=[pltpu.VMEM((SUBLANES, ext_rows, Dm), F32)],
        compiler_params=_params(("arbitrary",)),
    )(ds, ds, c, c, u, u, w_dw, ln_g, ln_b)


FFN_TC = 512
FFN_NJ = FFN_HIDDEN // FFN_TC
HALO8 = 8


def _ffn_mid_fwd(u0, w_dw, b_dw, *, name):
    L = u0.shape[1]
    tm = _tile(L, 256)
    r = tm // HALO8
    tc, nj = FFN_TC, FFN_NJ

    def body(u_ref, uh_ref, w_ref, b_ref, o_ref, ext_ref):
        keep = (pl.program_id(0) > 0).astype(F32)
        for h in range(2):
            ext = ext_ref.at[h]
            ext[0:HALO8, :] = uh_ref[h] * keep
            ext[HALO8:HALO8 + tm, :] = u_ref[h]
        for ls in _strips(tc):
            taps = [[w_ref[h, k:k + 1, ls] for k in range(FFN_K)] for h in range(2)]
            for lo, n in _row_blocks(tm):
                ug = _conv_strip(ext_ref.at[0], taps[0], b_ref[0, :, ls], HALO8, lo, n, ls)
                uv = _conv_strip(ext_ref.at[1], taps[1], b_ref[1, :, ls], HALO8, lo, n, ls)
                o_ref[lo:lo + n, ls] = (ug * _sigmoid(ug) * uv).astype(BF16)

    return pl.pallas_call(
        body, name=name, grid=(L // tm, nj),
        in_specs=[pl.BlockSpec((2, tm, tc), lambda i, j: (0, i, j)),
                  pl.BlockSpec((2, HALO8, tc), lambda i, j: (0, _prev_blk(i, r), j)),
                  pl.BlockSpec((2, FFN_K, tc), lambda i, j: (0, 0, j)), pl.BlockSpec((2, 1, tc), lambda i, j: (0, 0, j))],
        out_specs=pl.BlockSpec((tm, tc), lambda i, j: (i, j)),
        out_shape=jax.ShapeDtypeStruct((L, FFN_HIDDEN), BF16),
        scratch_shapes=[pltpu.VMEM((2, HALO8 + tm, tc), F32)],
        compiler_params=_params(("parallel", "parallel")),
    )(u0, u0, w_dw, b_dw)


def _ffn_bwd(dg, u0, w_dw, b_dw, *, name):
    L = u0.shape[1]
    tm = _tile(L, 256)
    r = tm // HALO8
    nt = L // tm
    nblk = L // HALO8
    tc, nj = FFN_TC, FFN_NJ
    rows = tm + HALO8

    def body(up_ref, uc_ref, un_ref, dg_ref, dgn_ref, w_ref, b_ref, du0_ref, dw_ref, db_ref, ext_ref, dext_ref):
        i = pl.program_id(1)

        @pl.when(i == 0)
        def _():
            dw_ref[...] = jnp.zeros_like(dw_ref)
            db_ref[...] = jnp.zeros_like(db_ref)

        keep_prev = (i > 0).astype(F32)
        keep_next = (i < nt - 1).astype(F32)
        for h in range(2):
            ext = ext_ref.at[h]
            ext[0:HALO8, :] = up_ref[h] * keep_prev
            ext[HALO8:HALO8 + tm, :] = uc_ref[h]
            ext[HALO8 + tm:HALO8 + rows, :] = un_ref[h]
        for ls in _strips(tc):
            taps = [[w_ref[h, k:k + 1, ls] for k in range(FFN_K)] for h in range(2)]
            dw_acc = [[jnp.zeros((1, STRIP), F32)] * FFN_K for _ in range(2)]
            db_acc = [jnp.zeros((1, STRIP), F32)] * 2
            for lo, n in _row_blocks(rows):
                ug = _conv_strip(ext_ref.at[0], taps[0], b_ref[0, :, ls], HALO8, lo, n, ls)
                uv = _conv_strip(ext_ref.at[1], taps[1], b_ref[1, :, ls], HALO8, lo, n, ls)
                dgx = dg_ref[lo:lo + n, ls] if lo < tm else dgn_ref[:, ls] * keep_next
                sg = _sigmoid(ug)
                for h, du in ((0, dgx * uv * _dsilu(ug, sg)), (1, dgx * (ug * sg))):
                    dext_ref[h, lo:lo + n, ls] = du
                    if lo < tm:
                        db_acc[h] = db_acc[h] + jnp.sum(du, axis=0, keepdims=True)
            for lo, n in _row_blocks(tm):
                for h in range(2):
                    dx = _conv_back_strip(dext_ref.at[h], uc_ref[h, lo:lo + n, ls], taps[h], dw_acc[h], lo, n, ls)
                    du0_ref[h, lo:lo + n, ls] = dx.astype(BF16)
            for h in range(2):
                db_ref[h, :, ls] += db_acc[h]
                for k in range(FFN_K):
                    dw_ref[h, k:k + 1, ls] += dw_acc[h][k]

    cur = pl.BlockSpec((2, tm, tc), lambda j, i: (0, i, j))
    prv = pl.BlockSpec((2, HALO8, tc), lambda j, i: (0, _prev_blk(i, r), j))
    nxt = pl.BlockSpec((2, HALO8, tc), lambda j, i: (0, _next_blk(i, r, nblk), j))
    wsp = pl.BlockSpec((2, FFN_K, tc), lambda j, i: (0, 0, j))
    bsp = pl.BlockSpec((2, 1, tc), lambda j, i: (0, 0, j))
    return pl.pallas_call(
        body, name=name, grid=(nj, nt),
        in_specs=[prv, cur, nxt, pl.BlockSpec((tm, tc), lambda j, i: (i, j)),
                  pl.BlockSpec((HALO8, tc), lambda j, i: (_next_blk(i, r, nblk), j)), wsp, bsp],
        out_specs=[cur, wsp, bsp],
        out_shape=[jax.ShapeDtypeStruct((2, L, FFN_HIDDEN), BF16), jax.ShapeDtypeStruct((2, FFN_K, FFN_HIDDEN), F32),
                   jax.ShapeDtypeStruct((2, 1, FFN_HIDDEN), F32)],
        scratch_shapes=[pltpu.VMEM((2, HALO8 + rows, tc), F32), pltpu.VMEM((2, rows, tc), F32)],
        compiler_params=_params(("parallel", "arbitrary")),
    )(u0, u0, u0, dg, dg, w_dw, b_dw)


def _ssm_unpad(zx_pad, *, name):
    L = zx_pad.shape[0]
    tm = _tile(L, 128)

    def body(p_ref, z_ref, xbc_ref, dt_ref, comp_ref):
        comp_ref[...] = jnp.zeros_like(comp_ref)
        for d in range(N_DEV):
            slab = p_ref[:, d * SSM_SLAB_PAD:(d + 1) * SSM_SLAB_PAD]
            if d:
                slab = pltpu.roll(slab, 8 * d, axis=1)
            comp_ref[:, 1280 * d:1280 * d + SSM_SLAB_PAD] += slab
        z_ref[...] = comp_ref[:, 0:SSM_INNER]
        xbc_ref[...] = comp_ref[:, SSM_INNER:SSM_INNER + SSM_CONV_DIM]
        dt_ref[...] = comp_ref[:, SSM_INNER + SSM_CONV_DIM:SSM_COMPACT]

    return pl.pallas_call(
        body, name=name, grid=(L // tm,),
        in_specs=[pl.BlockSpec((tm, N_DEV * SSM_SLAB_PAD), lambda i: (i, 0))],
        out_specs=[pl.BlockSpec((tm, SSM_INNER), lambda i: (i, 0)), pl.BlockSpec((tm, SSM_CONV_DIM), lambda i: (i, 0)),
                   pl.BlockSpec((tm, HEAD_LANES), lambda i: (i, 0))],
        out_shape=[jax.ShapeDtypeStruct((L, SSM_INNER), F32), jax.ShapeDtypeStruct((L, SSM_CONV_DIM), F32),
                   jax.ShapeDtypeStruct((L, HEAD_LANES), F32)],
        scratch_shapes=[pltpu.VMEM((tm, SSM_COMPACT), F32)],
        compiler_params=_params(("parallel",)),
    )(zx_pad)


def _ssm_pad(dz, dxbc, ddt, *, name):
    L = dz.shape[0]
    tm = _tile(L, 128)

    def body(dz_ref, dx_ref, dt_ref, p_ref, comp_ref):
        comp_ref[:, 0:SSM_INNER] = dz_ref[...]
        comp_ref[:, SSM_INNER:SSM_INNER + SSM_CONV_DIM] = dx_ref[...]
        lane = lax.broadcasted_iota(jnp.int32, (tm, HEAD_LANES), 1)
        comp_ref[:, SSM_INNER + SSM_CONV_DIM:SSM_COMPACT] = jnp.where(lane < SSM_HEADS, dt_ref[...], 0.0)
        col = lax.broadcasted_iota(jnp.int32, (tm, SSM_SLAB_PAD), 1)
        for d in range(N_DEV):
            win = comp_ref[:, 1280 * d:1280 * d + SSM_SLAB_PAD]
            if d:
                win = pltpu.roll(win, SSM_SLAB_PAD - 8 * d, axis=1)
            p_ref[:, d * SSM_SLAB_PAD:(d + 1) * SSM_SLAB_PAD] = jnp.where(col < SSM_SLAB, win, 0.0).astype(BF16)

    return pl.pallas_call(
        body, name=name, grid=(L // tm,),
        in_specs=[pl.BlockSpec((tm, SSM_INNER), lambda i: (i, 0)), pl.BlockSpec((tm, SSM_CONV_DIM), lambda i: (i, 0)),
                  pl.BlockSpec((tm, HEAD_LANES), lambda i: (i, 0))],
        out_specs=pl.BlockSpec((tm, N_DEV * SSM_SLAB_PAD), lambda i: (i, 0)),
        out_shape=jax.ShapeDtypeStruct((L, N_DEV * SSM_SLAB_PAD), BF16),
        scratch_shapes=[pltpu.VMEM((tm, SSM_COMPACT), F32)],
        compiler_params=_params(("parallel",)),
    )(dz, dxbc, ddt)


SSM_TC = 1024


def _ssm_conv_fwd(xpre, w, b, *, name):
    L, C = xpre.shape
    tm = _tile(L, 256)
    r = tm // HALO8
    tc = SSM_TC

    def body(x_ref, xh_ref, w_ref, b_ref, o_ref, ext_ref):
        keep = (pl.program_id(0) > 0).astype(F32)
        ext_ref[0:HALO8, :] = xh_ref[...] * keep
        ext_ref[HALO8:HALO8 + tm, :] = x_ref[...]
        for ls in _strips(tc):
            taps = [w_ref[k:k + 1, ls] for k in range(SSM_CONV_K)]
            for lo, n in _row_blocks(tm):
                pre = _conv_strip(ext_ref, taps, b_ref[:, ls], HALO8, lo, n, ls)
                o_ref[lo:lo + n, ls] = pre * _sigmoid(pre)

    return pl.pallas_call(
        body, name=name, grid=(L // tm, C // tc),
        in_specs=[pl.BlockSpec((tm, tc), lambda i, j: (i, j)), pl.BlockSpec((HALO8, tc), lambda i, j: (_prev_blk(i, r), j)),
                  pl.BlockSpec((SSM_CONV_K, tc), lambda i, j: (0, j)), pl.BlockSpec((1, tc), lambda i, j: (0, j))],
        out_specs=pl.BlockSpec((tm, tc), lambda i, j: (i, j)),
        out_shape=jax.ShapeDtypeStruct((L, C), F32),
        scratch_shapes=[pltpu.VMEM((HALO8 + tm, tc), F32)],
        compiler_params=_params(("parallel", "parallel")),
    )(xpre, xpre, w, b)


def _ssm_conv_bwd(dy, xpre, w, b, *, name):
    L, C = xpre.shape
    tm = _tile(L, 256)
    r = tm // HALO8
    nt = L // tm
    nblk = L // HALO8
    tc = SSM_TC
    rows = tm + HALO8

    def body(xp_ref, xc_ref, xn_ref, dy_ref, dyn_ref, w_ref, b_ref, dx_ref, dw_ref, db_ref, ext_ref, dext_ref):
        i = pl.program_id(1)

        @pl.when(i == 0)
        def _():
            dw_ref[...] = jnp.zeros_like(dw_ref)
            db_ref[...] = jnp.zeros_like(db_ref)

        ext_ref[0:HALO8, :] = xp_ref[...] * (i > 0).astype(F32)
        ext_ref[HALO8:HALO8 + tm, :] = xc_ref[...]
        ext_ref[HALO8 + tm:HALO8 + rows, :] = xn_ref[...]
        keep_next = (i < nt - 1).astype(F32)
        for ls in _strips(tc):
            taps = [w_ref[k:k + 1, ls] for k in range(SSM_CONV_K)]
            dw_acc = [jnp.zeros((1, STRIP), F32)] * SSM_CONV_K
            db_acc = jnp.zeros((1, STRIP), F32)
            for lo, n in _row_blocks(rows):
                pre = _conv_strip(ext_ref, taps, b_ref[:, ls], HALO8, lo, n, ls)
                dyx = dy_ref[lo:lo + n, ls] if lo < tm else dyn_ref[:, ls] * keep_next
                dpre = dyx * _dsilu(pre, _sigmoid(pre))
                dext_ref[lo:lo + n, ls] = dpre
                if lo < tm:
                    db_acc = db_acc + jnp.sum(dpre, axis=0, keepdims=True)
            for lo, n in _row_blocks(tm):
                dx_ref[lo:lo + n, ls] = _conv_back_strip(dext_ref, xc_ref[lo:lo + n, ls], taps, dw_acc, lo, n, ls)
            db_ref[:, ls] += db_acc
            for k in range(SSM_CONV_K):
                dw_ref[k:k + 1, ls] += dw_acc[k]

    cur = pl.BlockSpec((tm, tc), lambda j, i: (i, j))
    prv = pl.BlockSpec((HALO8, tc), lambda j, i: (_prev_blk(i, r), j))
    nxt = pl.BlockSpec((HALO8, tc), lambda j, i: (_next_blk(i, r, nblk), j))
    return pl.pallas_call(
        body, name=name, grid=(C // tc, nt),
        in_specs=[prv, cur, nxt, cur, nxt, pl.BlockSpec((SSM_CONV_K, tc), lambda j, i: (0, j)),
                  pl.BlockSpec((1, tc), lambda j, i: (0, j))],
        out_specs=[cur, pl.BlockSpec((SSM_CONV_K, tc), lambda j, i: (0, j)), pl.BlockSpec((1, tc), lambda j, i: (0, j))],
        out_shape=[jax.ShapeDtypeStruct((L, C), F32), jax.ShapeDtypeStruct((SSM_CONV_K, C), F32),
                   jax.ShapeDtypeStruct((1, C), F32)],
        scratch_shapes=[pltpu.VMEM((HALO8 + rows, tc), F32), pltpu.VMEM((rows, tc), F32)],
        compiler_params=_params(("parallel", "arbitrary")),
    )(xpre, xpre, xpre, dy, dy, w, b)


def _gnorm_fwd(y, z, g, *, name):
    L, C = y.shape
    tm = _tile(L, 256)
    gw = SSM_GROUP_W

    def body(y_ref, z_ref, g_ref, o_ref):
        for k in range(SSM_GROUPS):
            sl = slice(k * gw, (k + 1) * gw)
            zv = z_ref[:, sl]
            yz = y_ref[:, sl] * zv * _sigmoid(zv)
            r = lax.rsqrt(jnp.mean(yz * yz, axis=-1, keepdims=True) + RMS_EPS)
            o_ref[:, sl] = (yz * r * g_ref[:, sl]).astype(BF16)

    row = pl.BlockSpec((tm, C), lambda i: (i, 0))
    return pl.pallas_call(
        body, name=name, grid=(L // tm,),
        in_specs=[row, row, pl.BlockSpec((1, C), lambda i: (0, 0))], out_specs=row,
        out_shape=jax.ShapeDtypeStruct((L, C), BF16),
        compiler_params=_params(("parallel",)),
    )(y, z, g)


def _gnorm_bwd(dn, y, z, g, *, name):
    L, C = y.shape
    tm = _tile(L, 256)
    gw = SSM_GROUP_W

    def body(dn_ref, y_ref, z_ref, g_ref, dy_ref, dz_ref, dg_ref):
        @pl.when(pl.program_id(0) == 0)
        def _():
            dg_ref[...] = jnp.zeros_like(dg_ref)

        for k in range(SSM_GROUPS):
            sl = slice(k * gw, (k + 1) * gw)
            zv, yv = z_ref[:, sl], y_ref[:, sl]
            sz = _sigmoid(zv)
            silu = zv * sz
            yz = yv * silu
            r = lax.rsqrt(jnp.mean(yz * yz, axis=-1, keepdims=True) + RMS_EPS)
            nrm = yz * r
            dnv = dn_ref[:, sl]
            dg_ref[:, sl] += jnp.sum(dnv * nrm, axis=0, keepdims=True)
            dh = dnv * g_ref[:, sl]
            dyz = r * (dh - nrm * jnp.mean(dh * nrm, axis=-1, keepdims=True))
            dy_ref[:, sl] = dyz * silu
            dz_ref[:, sl] = dyz * yv * _dsilu(zv, sz)

    row = pl.BlockSpec((tm, C), lambda i: (i, 0))
    vec = pl.BlockSpec((1, C), lambda i: (0, 0))
    return pl.pallas_call(
        body, name=name, grid=(L // tm,),
        in_specs=[row, row, row, vec], out_specs=[row, row, vec],
        out_shape=[jax.ShapeDtypeStruct((L, C), F32), jax.ShapeDtypeStruct((L, C), F32),
                   jax.ShapeDtypeStruct((1, C), F32)],
        compiler_params=_params(("arbitrary",)),
    )(dn, y, z, g)


def _ssd_consts():
    q, gw = CHUNK, SSM_GROUP_W
    sub = lax.broadcasted_iota(jnp.int32, (q, gw), 0)
    lane_in = lax.broadcasted_iota(jnp.int32, (q, gw), 1) % q
    sub2 = lax.broadcasted_iota(jnp.int32, (gw, gw), 0) // q
    lane2 = lax.broadcasted_iota(jnp.int32, (gw, gw), 1) // q
    t = lax.broadcasted_iota(jnp.int32, (q, q), 0)
    u = lax.broadcasted_iota(jnp.int32, (q, q), 1)
    return dict(
        diag=(lane_in == sub), low=(lane_in <= sub), up=(sub <= lane_in), block=(sub2 == lane2),
        tri=(u <= t).astype(BF16), trit=(u >= t).astype(BF16), ones=jnp.ones((q, q), BF16),
        last=(lax.broadcasted_iota(jnp.int32, (q, HEAD_LANES), 0) == q - 1))


def _ssd_chunk_terms(dt_raw, bias, a_log, e, et, k):
    dt = _softplus(dt_raw + bias)
    a_neg = -jnp.exp(a_log)
    cs = _dot3_r(k["tri"], dt * a_neg)
    cs_last = cs[CHUNK - 1:CHUNK, :]
    ecs = jnp.exp(cs)
    dte = jnp.exp(cs_last - cs)
    ecl_hn = _dot3_l(ecs, k["last"].astype(BF16), "tn")
    rows = _dot3_r(et, ecl_hn, parts=2)
    return dict(dt=dt, a_neg=a_neg, cs=cs, ecs=ecs, dte=dte, rows=rows,
                dtx=_dot3_l(dt, e, parts=1), csx=_dot3_l(cs, e, parts=2), ecsx=_dot3_l(ecs, e, parts=1),
                dtex=_dot3_l(dte, e, parts=1))


def _tile8(x):
    return jnp.concatenate([x] * 8, axis=0)


def _ssd_fwd(xbc, dt_raw, dt_bias, a_log, d_x, e, et, *, name):
    L = xbc.shape[0]
    nc = L // CHUNK
    q, gw, ns = CHUNK, SSM_GROUP_W, SSM_STATE

    def body(xbc_ref, dt_ref, bias_ref, alog_ref, dx_ref, e_ref, et_ref, y_ref, st_ref, s_ref):
        @pl.when(pl.program_id(0) == 0)
        def _():
            s_ref[...] = jnp.zeros_like(s_ref)

        k = _ssd_consts()
        t = _ssd_chunk_terms(dt_ref[...], bias_ref[...], alog_ref[...], e_ref[...], et_ref[...], k)
        st_ref[...] = s_ref[...]
        for g in range(SSM_GROUPS):
            ch = slice(g * gw, (g + 1) * gw)
            xs = xbc_ref[:, ch]
            bm = xbc_ref[:, SSM_INNER + g * ns:SSM_INNER + (g + 1) * ns].astype(BF16)
            cm = xbc_ref[:, SSM_INNER + (SSM_GROUPS + g) * ns:SSM_INNER + (SSM_GROUPS + g + 1) * ns].astype(BF16)
            xd = xs * t["dtx"][:, ch]
            csx = t["csx"][:, ch]
            csrow = _dot3_r(k["ones"], jnp.where(k["diag"], csx, 0.0), parts=2)
            lcat = jnp.where(k["low"], jnp.exp(jnp.minimum(csx - csrow, 0.0)), 0.0)
            mcat = _dot(cm, _tile8(bm), "nt") * lcat
            xdbd = jnp.where(k["block"], _tile8(xd), 0.0).astype(BF16)
            sg = s_ref[ch, :]
            y = _dot(mcat.astype(BF16), xdbd)
            y = y + _dot(cm, sg.astype(BF16), "nt") * t["ecsx"][:, ch]
            y_ref[:, ch] = y + dx_ref[:, ch] * xs
            s_ref[ch, :] = sg * t["rows"][ch, :] + _dot((xd * t["dtex"][:, ch]).astype(BF16), bm, "tn")

    hv = pl.BlockSpec((1, HEAD_LANES), lambda c: (0, 0))
    return pl.pallas_call(
        body, name=name, grid=(nc,),
        in_specs=[pl.BlockSpec((q, SSM_CONV_DIM), lambda c: (c, 0)), pl.BlockSpec((q, HEAD_LANES), lambda c: (c, 0)),
                  hv, hv, pl.BlockSpec((1, SSM_INNER), lambda c: (0, 0)),
                  pl.BlockSpec((HEAD_LANES, SSM_INNER), lambda c: (0, 0)),
                  pl.BlockSpec((SSM_INNER, HEAD_LANES), lambda c: (0, 0))],
        out_specs=[pl.BlockSpec((q, SSM_INNER), lambda c: (c, 0)),
                   pl.BlockSpec((None, SSM_INNER, ns), lambda c: (c, 0, 0))],
        out_shape=[jax.ShapeDtypeStruct((L, SSM_INNER), F32), jax.ShapeDtypeStruct((nc, SSM_INNER, ns), F32)],
        scratch_shapes=[pltpu.VMEM((SSM_INNER, ns), F32)],
        compiler_params=_params(("arbitrary",)),
    )(xbc, dt_raw, dt_bias, a_log, d_x, e, et)


def _ssd_bwd(dy, xbc, dt_raw, states, dt_bias, a_log, d_x, e, et, *, name):
    L = xbc.shape[0]
    nc = L // CHUNK
    q, gw, ns = CHUNK, SSM_GROUP_W, SSM_STATE

    def body(dy_ref, xbc_ref, dt_ref, st_ref, bias_ref, alog_ref, dx_ref, e_ref, et_ref,
             dxbc_ref, ddt_ref, dbias_ref, dalog_ref, dd_ref, ds_ref, f1_ref, f2_ref, f3_ref, f4_ref, fs_ref):
        @pl.when(pl.program_id(0) == 0)
        def _():
            ds_ref[...] = jnp.zeros_like(ds_ref)
            for ref in (dbias_ref, dalog_ref, dd_ref):
                ref[...] = jnp.zeros_like(ref)

        k = _ssd_consts()
        ev = e_ref[...]
        t = _ssd_chunk_terms(dt_ref[...], bias_ref[...], alog_ref[...], ev, et_ref[...], k)
        ones8 = jnp.ones((8, ns), BF16)
        for g in range(SSM_GROUPS):
            ch = slice(g * gw, (g + 1) * gw)
            bsl = slice(SSM_INNER + g * ns, SSM_INNER + (g + 1) * ns)
            csl = slice(SSM_INNER + (SSM_GROUPS + g) * ns, SSM_INNER + (SSM_GROUPS + g + 1) * ns)
            xs = xbc_ref[:, ch]
            bm = xbc_ref[:, bsl].astype(BF16)
            cm = xbc_ref[:, csl].astype(BF16)
            dyv = dy_ref[:, ch]
            dtx, ecsx, dtex, csx = t["dtx"][:, ch], t["ecsx"][:, ch], t["dtex"][:, ch], t["csx"][:, ch]
            xd = xs * dtx
            csrow = _dot3_r(k["ones"], jnp.where(k["diag"], csx, 0.0), parts=2)
            lcat = jnp.where(k["low"], jnp.exp(jnp.minimum(csx - csrow, 0.0)), 0.0)
            ltcat = jnp.where(k["up"], jnp.exp(jnp.minimum(csrow - csx, 0.0)), 0.0)
            mcat = _dot(cm, _tile8(bm), "nt") * lcat
            mtcat = _dot(bm, _tile8(cm), "nt") * ltcat
            xdbd = jnp.where(k["block"], _tile8(xd), 0.0).astype(BF16)
            dybd = jnp.where(k["block"], _tile8(dyv), 0.0).astype(BF16)
            gq = _dot(dyv.astype(BF16), xdbd, "nt") * mcat
            gt = _dot(xd.astype(BF16), dybd, "nt")
            gqt = gt * mtcat
            dcbt = _dot((gt * ltcat).astype(BF16), jnp.where(k["diag"], 1.0, 0.0).astype(BF16), "nt")
            dcbt = dcbt.astype(BF16)
            sg = st_ref[ch, :]
            dsg = ds_ref[ch, :]
            sgb, dsgb = sg.astype(BF16), dsg.astype(BF16)
            yoff = _dot(cm, sgb, "nt") * ecsx
            dye = (dyv * ecsx).astype(BF16)
            xdd = xd * dtex
            dxbc_ref[:, csl] = _dot(dcbt, bm, "tn") + _dot(dye, sgb)
            dxbc_ref[:, bsl] = _dot(dcbt, cm) + _dot(xdd.astype(BF16), dsgb)
            bds = _dot(bm, dsgb, "nt")
            dxd = _dot(mtcat.astype(BF16), dybd) + dtex * bds
            wx = xdd * bds
            ds_ref[ch, :] = t["rows"][ch, :] * dsg + _dot(dye, cm, "tn")
            f1_ref[:, ch] = gq - gqt + dyv * yoff - wx
            f2_ref[:, ch] = wx
            f3_ref[:, ch] = dxd * xs
            f4_ref[:, ch] = dyv * xs
            dxbc_ref[:, ch] = dxd * dtx + dx_ref[:, ch] * dyv
            fs_ref[:, ch] = _dot3_r(ones8, dsg * sg, "nt", parts=2)
        fold = lambda v, parts: _dot3_l(v, ev, "nt", parts=parts)
        f2 = fold(f2_ref[...], 1)
        last_row = jnp.sum(f2, axis=0, keepdims=True) + t["ecs"][q - 1:q, :] * fold(fs_ref[...], 2)[0:1, :]
        dcs = fold(f1_ref[...], 2) + jnp.where(k["last"], last_row, 0.0)
        da = _dot3_r(k["trit"], dcs)
        ddt = da * t["a_neg"] + fold(f3_ref[...], 2)
        ddt_raw = ddt * _sigmoid(dt_ref[...] + bias_ref[...])
        ddt_ref[...] = ddt_raw
        dbias_ref[...] += jnp.sum(ddt_raw, axis=0, keepdims=True)
        dalog_ref[...] += jnp.sum(da * t["dt"], axis=0, keepdims=True) * t["a_neg"]
        dd_ref[...] += jnp.sum(fold(f4_ref[...], 1), axis=0, keepdims=True)

    rev = lambda c: (nc - 1 - c, 0)
    hv = pl.BlockSpec((1, HEAD_LANES), lambda c: (0, 0))
    return pl.pallas_call(
        body, name=name, grid=(nc,),
        in_specs=[pl.BlockSpec((q, SSM_INNER), rev), pl.BlockSpec((q, SSM_CONV_DIM), rev),
                  pl.BlockSpec((q, HEAD_LANES), rev),
                  pl.BlockSpec((None, SSM_INNER, ns), lambda c: (nc - 1 - c, 0, 0)),
                  hv, hv, pl.BlockSpec((1, SSM_INNER), lambda c: (0, 0)),
                  pl.BlockSpec((HEAD_LANES, SSM_INNER), lambda c: (0, 0)),
                  pl.BlockSpec((SSM_INNER, HEAD_LANES), lambda c: (0, 0))],
        out_specs=[pl.BlockSpec((q, SSM_CONV_DIM), rev), pl.BlockSpec((q, HEAD_LANES), rev), hv, hv, hv],
        out_shape=[jax.ShapeDtypeStruct((L, SSM_CONV_DIM), F32), jax.ShapeDtypeStruct((L, HEAD_LANES), F32)]
        + [jax.ShapeDtypeStruct((1, HEAD_LANES), F32)] * 3,
        scratch_shapes=[pltpu.VMEM((SSM_INNER, ns), F32)] + [pltpu.VMEM((q, SSM_INNER), F32)] * 4
        + [pltpu.VMEM((8, SSM_INNER), F32)],
        compiler_params=_params(("arbitrary",)),
    )(dy, xbc, dt_raw, states, dt_bias, a_log, d_x, e, et)


def _my_index():
    return 4 * lax.axis_index("x") + 2 * lax.axis_index("y") + lax.axis_index("c")


def _peer(k):
    return (lax.axis_index("x") ^ ((k >> 2) & 1), lax.axis_index("y") ^ ((k >> 1) & 1), lax.axis_index("c") ^ (k & 1))


def _all_gather(shard, *, by_rows, name):
    nl, a, b = shard.shape
    out_shape = (nl, N_DEV * a, b) if by_rows else (nl, N_DEV, a, b)

    def body(src_ref, out_ref, send_sems, recv_sems, local_sems):
        me = _my_index()

        def mine(j):
            if by_rows:
                return out_ref.at[j, pl.ds(pl.multiple_of(me * a, 16), a), :]
            return out_ref.at[j, me]

        local = [pltpu.make_async_copy(src_ref.at[j], mine(j), local_sems.at[j]) for j in range(nl)]
        for cp in local:
            cp.start()
        copies = []
        for j in range(nl):
            for k in range(1, N_DEV):
                cp = pltpu.make_async_remote_copy(
                    src_ref=src_ref.at[j], dst_ref=mine(j), send_sem=send_sems.at[j, k - 1],
                    recv_sem=recv_sems.at[j, k - 1], device_id=_peer(k), device_id_type=MESH_ID)
                cp.start()
                copies.append(cp)
        for cp in copies:
            cp.wait()
        for cp in local:
            cp.wait()

    return pl.pallas_call(
        body, name=name, in_specs=[_HBM], out_specs=_HBM,
        out_shape=jax.ShapeDtypeStruct(out_shape, shard.dtype),
        scratch_shapes=[pltpu.SemaphoreType.DMA((nl, N_DEV - 1)), pltpu.SemaphoreType.DMA((nl, N_DEV - 1)),
                        pltpu.SemaphoreType.DMA((nl,))],
    )(shard)


_EFFECT = pltpu.SideEffectType.DATAFLOW_SIDE_EFFECTING


def _blk(ref, i, rows):
    if len(ref.shape) == 3:
        return ref.at[i]
    return ref.at[pl.ds(pl.multiple_of(i * rows, 16), rows), :]


def _remote_copies(scatter, src_ref, land_ref, send_sems, recv_sems, idx):
    me = _my_index()
    out = []
    for k in range(1, N_DEV):
        if scatter:
            src, dst = _blk(src_ref, me ^ k, land_ref.shape[1]), land_ref.at[me]
        else:
            src, dst = src_ref, _blk(land_ref, me, src_ref.shape[0])
        sem = idx * (N_DEV - 1) + k - 1
        out.append(pltpu.make_async_remote_copy(
            src_ref=src, dst_ref=dst, send_sem=send_sems.at[sem], recv_sem=recv_sems.at[sem],
            device_id=_peer(k), device_id_type=MESH_ID))
    return out


def _own_copy(scatter, src_ref, land_ref, own_sems, idx):
    me = _my_index()
    if scatter:
        src, dst = _blk(src_ref, me, land_ref.shape[1]), land_ref.at[me]
    else:
        src, dst = src_ref, _blk(land_ref, me, src_ref.shape[0])
    return pltpu.make_async_copy(src, dst, own_sems.at[idx])


def _copies_start(srcs, land_shapes, *, scatter, after, name):
    n = len(srcs)
    n_after = len(after)

    def body(*refs):
        src_refs, land_refs = refs[:n], refs[n:2 * n]
        send_sems, recv_sems, own_sems = refs[2 * n + n_after:2 * n + n_after + 3]
        for i in range(n):
            for cp in _remote_copies(scatter, src_refs[i], land_refs[i], send_sems, recv_sems, i):
                cp.start()
            _own_copy(scatter, src_refs[i], land_refs[i], own_sems, i).start()
        refs[-1][...] = jnp.zeros_like(refs[-1])

    lands = [lax.empty(shp, t.dtype) for shp, t in zip(land_shapes, srcs)]
    arrays = [pltpu.with_memory_space_constraint(t, pltpu.HBM) for t in list(srcs) + lands]
    sems = pltpu.SemaphoreType.DMA((n * (N_DEV - 1),))
    res = pl.pallas_call(
        body, name=name,
        out_shape=(sems, sems, pltpu.SemaphoreType.DMA((n,)), *[pltpu.HBM(t.shape, t.dtype) for t in arrays],
                   jax.ShapeDtypeStruct((8, 128), F32)),
        in_specs=[_HBM] * (2 * n) + [_ANY] * n_after,
        out_specs=(_SEM, _SEM, _SEM, *[_HBM] * (2 * n), pl.BlockSpec(memory_space=pltpu.VMEM)),
        input_output_aliases={i: 3 + i for i in range(2 * n)},
        compiler_params=pltpu.CompilerParams(has_side_effects=_EFFECT),
    )(*arrays, *after)
    return (res[:3], res[3:3 + n], res[3 + n:3 + 2 * n]), res[-1]


def _copies_wait(started, *, scatter, after, name):
    sems, srcs, lands = started
    n = len(srcs)

    def body(*refs):
        src_refs, land_refs = refs[:n], refs[n:2 * n]
        s_sems, r_sems, o_sems = refs[2 * n:2 * n + 3]
        for i in range(n):
            for cp in _remote_copies(scatter, src_refs[i], land_refs[i], s_sems, r_sems, i):
                cp.wait_send()
                cp.wait_recv()
            _own_copy(scatter, src_refs[i], land_refs[i], o_sems, i).wait()

    arrays = list(srcs) + list(lands)
    res = pl.pallas_call(
        body, name=name,
        out_shape=tuple(pltpu.HBM(t.shape, t.dtype) for t in arrays),
        in_specs=[_HBM] * (2 * n) + [_SEM] * 3 + [_ANY] * len(after), out_specs=tuple([_HBM] * (2 * n)),
        input_output_aliases={i: i for i in range(2 * n)},
        compiler_params=pltpu.CompilerParams(has_side_effects=_EFFECT),
    )(*arrays, *sems, *after)
    return res[n:]


def _adamw_math(w, g, m, v):
    m = ADAM_B1 * m + (1.0 - ADAM_B1) * g
    v = ADAM_B2 * v + (1.0 - ADAM_B2) * (g * g)
    m_hat = m / (1.0 - ADAM_B1 ** ADAM_STEP)
    v_hat = v / (1.0 - ADAM_B2 ** ADAM_STEP)
    delta = -ADAM_LR * (m_hat / (jnp.sqrt(v_hat) + ADAM_EPS) + ADAM_WD * w)
    return delta, m, v


def _adamw(recvs, w, m, v, *, name):
    nl, R, C = w.shape
    tr = max(t for t in range(8, (128 if C > 1024 else 512) + 1, 8) if R % t == 0)

    def body(*refs):
        r_refs = refs[:nl]
        w_ref, m_ref, v_ref, g_ref, d_ref, nm_ref, nv_ref = refs[nl:]
        for layer in range(nl):
            @pl.when(pl.program_id(0) == layer)
            def _(r_ref=r_refs[layer]):
                g = r_ref[0].astype(F32)
                for s in range(1, N_DEV):
                    g = g + r_ref[s].astype(F32)
                delta, nm, nv = _adamw_math(w_ref[...], g, m_ref[...], v_ref[...])
                g_ref[...] = g
                d_ref[...] = delta
                nm_ref[...] = nm
                nv_ref[...] = nv

    def recv_spec(layer):
        return pl.BlockSpec((N_DEV, tr, C), lambda j, i: (0, jnp.where(j == layer, i, 0), 0))

    row = pl.BlockSpec((None, tr, C), lambda j, i: (j, i, 0))
    return pl.pallas_call(
        body, name=name, grid=(nl, R // tr),
        in_specs=[recv_spec(layer) for layer in range(nl)] + [row, row, row],
        out_specs=[row] * 4, out_shape=[jax.ShapeDtypeStruct((nl, R, C), F32)] * 4,
        compiler_params=_params(("parallel", "parallel")),
    )(*recvs, w, m, v)


def _sum_slots(g8, *, name):
    _, P, C = g8.shape

    def body(r_ref, o_ref):
        g = r_ref[0]
        for s in range(1, N_DEV):
            g = g + r_ref[s]
        o_ref[...] = g

    return pl.pallas_call(
        body, name=name, grid=(1,),
        in_specs=[pl.BlockSpec((N_DEV, P, C), lambda i: (0, 0, 0))],
        out_specs=pl.BlockSpec((P, C), lambda i: (0, 0)),
        out_shape=jax.ShapeDtypeStruct((P, C), F32),
        compiler_params=_params(("arbitrary",)),
    )(g8)


def _adamw_small(g, w, m, v, *, name):
    P, C = w.shape

    def body(g_ref, w_ref, m_ref, v_ref, d_ref, nm_ref, nv_ref):
        delta, nm, nv = _adamw_math(w_ref[...], g_ref[...], m_ref[...], v_ref[...])
        d_ref[...] = delta
        nm_ref[...] = nm
        nv_ref[...] = nv

    full = pl.BlockSpec((P, C), lambda i: (0, 0))
    return pl.pallas_call(
        body, name=name, grid=(1,), in_specs=[full] * 4, out_specs=[full] * 3,
        out_shape=[jax.ShapeDtypeStruct((P, C), F32)] * 3,
        compiler_params=_params(("arbitrary",)),
    )(g, w, m, v)


def _pack(arrays):
    flat = jnp.concatenate([a.reshape(-1) for a in arrays])
    pad = (-flat.shape[0]) % (8 * 128)
    return jnp.pad(flat, (0, pad)).reshape(-1, 128)


def _unpack(buf, shapes):
    flat = buf.reshape(-1)
    out, off = [], 0
    for shp in shapes:
        n = 1
        for s in shp:
            n *= s
        out.append(flat[off:off + n].reshape(shp))
        off += n
    return out


def _expand_matrices():
    h = lax.broadcasted_iota(jnp.int32, (HEAD_LANES, SSM_INNER), 0)
    col = lax.broadcasted_iota(jnp.int32, (HEAD_LANES, SSM_INNER), 1) // SSM_HEAD_DIM
    e = (h == col).astype(BF16)
    return e, e.T


def _pad_heads(v):
    return jnp.pad(v.reshape(1, -1), ((0, 0), (0, HEAD_LANES - v.shape[-1])))


SHARDED_SMALL = ("cv_w_dw", "ssm_w_conv", "ssm_b_conv", "ssm_norm_g", "ffn_w_dw")
REPLICATED_SMALL = ("norm_mix_g", "norm_ffn_g", "norm_final_g", "cv_b_in", "cv_b_dw", "cv_ln_g", "cv_ln_b",
                    "cv_b_out", "ssm_dt_bias", "ssm_a_log", "ssm_d", "ffn_b_dw")
SMALL = REPLICATED_SMALL + SHARDED_SMALL
BIG_COLS = ("cv_w_in", "ssm_w_in", "ffn_w_up")
BIG_ROWS = ("cv_w_out", "ssm_w_out", "ffn_w_down")
WEIGHTS = ("norm_mix_g", "norm_ffn_g", "norm_final_g", "cv_w_in", "cv_b_in", "cv_w_dw", "cv_b_dw", "cv_ln_g",
           "cv_ln_b", "cv_w_out", "cv_b_out", "ssm_w_in", "ssm_w_conv", "ssm_b_conv", "ssm_dt_bias", "ssm_a_log",
           "ssm_d", "ssm_norm_g", "ssm_w_out", "ffn_w_up", "ffn_w_dw", "ffn_b_dw", "ffn_w_down")


N_STAGES = 2 * DEPTH


def _stage_weights(si):
    i = si // 2
    if si % 2:
        return "ffn_w_up", "ffn_w_down", i
    return ("cv_w_in", "cv_w_out", i // 2) if i % 2 == 0 else ("ssm_w_in", "ssm_w_out", i // 2)


def _gather_small_weights(w):
    small_local = _pack([w[n] for n in SHARDED_SMALL])
    small8 = _all_gather(small_local[None], by_rows=False, name="gather_small")[0]
    full = {}
    per_dev = [_unpack(small8[d], [w[n].shape for n in SHARDED_SMALL]) for d in range(N_DEV)]
    for idx, n in enumerate(SHARDED_SMALL):
        full[n] = jnp.concatenate([per_dev[d][idx] for d in range(N_DEV)], axis=-1)
    for n in REPLICATED_SMALL:
        full[n] = w[n]
    return full


def _forward_backward(x, target, weights_of, full, emit):
    e, et = _expand_matrices()
    grads = {}

    saved = []
    for i in range(DEPTH):
        j = i // 2
        s = {"x_in": x}
        wc, wr, behind = weights_of(2 * i, x)
        h1 = _rms_fwd(x, full["norm_mix_g"][i][None], name=f"rms_mix_{i}")
        s["h1"] = h1
        if i % 2 == 0:
            u = _mm_cols_nn(h1, wc, bias=full["cv_b_in"][j][None], after=behind, name=f"cv_in_{i}")
            sact, c = _cv_mid_fwd(u, full["cv_w_dw"][j], full["cv_b_dw"][j][None], full["cv_ln_g"][j][None],
                                  full["cv_ln_b"][j][None], name=f"cv_mid_{i}")
            x = _mm_rows_nn(sact, wr, res=x, bias=full["cv_b_out"][j][None], name=f"cv_out_{i}")
            s.update(u=u, c=c, sact=sact)
        else:
            zx = _mm_cols_nn(h1, wc, after=behind, name=f"ssm_in_{i}")
            z, xpre, dt_raw = _ssm_unpad(zx, name=f"ssm_unpad_{i}")
            xbc = _ssm_conv_fwd(xpre, full["ssm_w_conv"][j], full["ssm_b_conv"][j][None], name=f"ssm_conv_{i}")
            hp = dict(dt_bias=_pad_heads(full["ssm_dt_bias"][j]), a_log=_pad_heads(full["ssm_a_log"][j]),
                      d_x=jnp.repeat(full["ssm_d"][j], SSM_HEAD_DIM)[None])
            y, states = _ssd_fwd(xbc, dt_raw, hp["dt_bias"], hp["a_log"], hp["d_x"], e, et, name=f"ssd_fwd_{i}")
            yn = _gnorm_fwd(y, z, full["ssm_norm_g"][j][None], name=f"gnorm_{i}")
            x = _mm_rows_nn(yn, wr, res=x, name=f"ssm_out_{i}")
            s.update(z=z, xpre=xpre, dt_raw=dt_raw, xbc=xbc, y=y, states=states, yn=yn, hp=hp)
        s["x_mid"] = x
        wc2, wr2, behind = weights_of(2 * i + 1, x)
        h2 = _rms_fwd(x, full["norm_ffn_g"][i][None], name=f"rms_ffn_{i}")
        u0 = _mm_cols_nn(h2, wc2, after=behind, halves=True, name=f"ffn_up_{i}")
        ffn_w = full["ffn_w_dw"][i].reshape(FFN_K, 2, FFN_HIDDEN).transpose(1, 0, 2)
        ffn_b = full["ffn_b_dw"][i].reshape(2, 1, FFN_HIDDEN)
        gact = _ffn_mid_fwd(u0, ffn_w, ffn_b, name=f"ffn_mid_{i}")
        x = _mm_rows_nn(gact, wr2, res=x, name=f"ffn_down_{i}")
        s.update(h2=h2, u0=u0, gact=gact, ffn_w=ffn_w, ffn_b=ffn_b, weights=(wc, wr, wc2, wr2))
        saved.append(s)

    dx, dxb, dg_final, loss = _final_loss(x, full["norm_final_g"][None], target, name="final_loss")
    grads["norm_final_g"] = dg_final[0]

    small_layers = {n: [None] * full[n].shape[0] for n in SMALL if n != "norm_final_g"}
    behind = None
    for i in reversed(range(DEPTH)):
        j = i // 2
        s = saved[i]
        wc, wr, wc2, wr2 = s["weights"]
        dgact = _mm_rows_nt(dxb, wr2, after=behind, name=f"ffn_down_dx_{i}")
        dwr2 = _mm_rows_tn(s["gact"], dxb, name=f"ffn_down_dw_{i}")
        du0, dw_dw, db_dw = _ffn_bwd(dgact, s["u0"], s["ffn_w"], s["ffn_b"], name=f"ffn_bwd_{i}")
        small_layers["ffn_w_dw"][i] = dw_dw.transpose(1, 0, 2).reshape(FFN_K, 2 * FFN_HIDDEN)
        small_layers["ffn_b_dw"][i] = db_dw.reshape(2 * FFN_HIDDEN)
        dh2 = _mm_cols_nt(du0, wc2, name=f"ffn_up_dx_{i}")
        dwc2 = _mm_cols_tn(s["h2"], du0, name=f"ffn_up_dw_{i}")
        dx, dxb, dg, colsum = _rms_bwd(s["x_mid"], full["norm_ffn_g"][i][None], dh2, dx, name=f"rms_ffn_bwd_{i}")
        small_layers["norm_ffn_g"][i] = dg[0]
        behind = emit(2 * i + 1, dwc2, dwr2)
        if i % 2 == 0:
            small_layers["cv_b_out"][j] = colsum[0]
            dsact = _mm_rows_nt(dxb, wr, after=behind, name=f"cv_out_dx_{i}")
            dwr = _mm_rows_tn(s["sact"], dxb, name=f"cv_out_dw_{i}")
            du, dw_dw, db_dw, dlg, dlb, db_in = _cv_bwd(dsact, s["c"], s["u"], full["cv_w_dw"][j], full["cv_ln_g"][j][None],
                                                        full["cv_ln_b"][j][None], name=f"cv_bwd_{i}")
            small_layers["cv_w_dw"][j] = dw_dw
            small_layers["cv_b_dw"][j] = db_dw[0]
            small_layers["cv_ln_g"][j] = dlg[0]
            small_layers["cv_ln_b"][j] = dlb[0]
            small_layers["cv_b_in"][j] = db_in[0]
            dh1 = _mm_cols_nt(du, wc, name=f"cv_in_dx_{i}")
            dwc = _mm_cols_tn(s["h1"], du, name=f"cv_in_dw_{i}")
        else:
            hp = s["hp"]
            dyn = _mm_rows_nt(dxb, wr, after=behind, name=f"ssm_out_dx_{i}")
            dwr = _mm_rows_tn(s["yn"], dxb, name=f"ssm_out_dw_{i}")
            dy, dz, dng = _gnorm_bwd(dyn, s["y"], s["z"], full["ssm_norm_g"][j][None], name=f"gnorm_bwd_{i}")
            small_layers["ssm_norm_g"][j] = dng[0]
            dxbc, ddt_raw, dbias, dalog, dd = _ssd_bwd(dy, s["xbc"], s["dt_raw"], s["states"], hp["dt_bias"], hp["a_log"],
                                                      hp["d_x"], e, et, name=f"ssd_bwd_{i}")
            small_layers["ssm_dt_bias"][j] = dbias[0, :SSM_HEADS]
            small_layers["ssm_a_log"][j] = dalog[0, :SSM_HEADS]
            small_layers["ssm_d"][j] = dd[0, :SSM_HEADS]
            dxpre, dw_conv, db_conv = _ssm_conv_bwd(dxbc, s["xpre"], full["ssm_w_conv"][j], full["ssm_b_conv"][j][None],
                                                    name=f"ssm_conv_bwd_{i}")
            small_layers["ssm_w_conv"][j] = dw_conv
            small_layers["ssm_b_conv"][j] = db_conv[0]
            dzx = _ssm_pad(dz, dxpre, ddt_raw, name=f"ssm_pad_{i}")
            dh1 = _mm_cols_nt(dzx, wc, name=f"ssm_in_dx_{i}")
            dwc = _mm_cols_tn(s["h1"], dzx, name=f"ssm_in_dw_{i}")
        dx, dxb, dg, _ = _rms_bwd(s["x_in"], full["norm_mix_g"][i][None], dh1, dx, name=f"rms_mix_bwd_{i}")
        small_layers["norm_mix_g"][i] = dg[0]
        behind = emit(2 * i, dwc, dwr)
    for n, layers in small_layers.items():
        grads[n] = jnp.stack(layers)
    return loss, dx, grads


def _update_big(names, recvs, w, m, v):
    out = {}
    for n in names:
        b = recvs[n][0].shape[-1]
        pad = ((0, 0), (0, 0), (0, b - w[n].shape[-1]))
        res = _adamw(recvs[n], *[jnp.pad(t, pad) for t in (w[n], m[n], v[n])], name="adamw_" + n)
        out[n] = [r[..., :w[n].shape[-1]] for r in res]
    return out


def _update_small(grads, w, m, v):
    me = _my_index()
    out = {}
    small_part = _pack([grads[n] for n in SMALL])
    small_all = _all_gather(small_part[None], by_rows=False, name="gather_small_grads")[0]
    small_sum = _sum_slots(small_all, name="sum_small_grads")
    gfull = dict(zip(SMALL, _unpack(small_sum, [grads[n].shape for n in SMALL])))
    glocal = []
    for n in SMALL:
        g = gfull[n]
        if n in SHARDED_SMALL:
            cols = w[n].shape[-1]
            g = lax.dynamic_slice_in_dim(g, me * cols, cols, axis=g.ndim - 1)
        glocal.append(g)
    packed = [_pack(glocal)] + [_pack([t[n] for n in SMALL]) for t in (w, m, v)]
    res = _adamw_small(*packed, name="adamw_small")
    shapes = [w[n].shape for n in SMALL]
    unpacked = [_unpack(r, shapes) for r in res]
    for idx, n in enumerate(SMALL):
        out[n] = [glocal[idx]] + [u[idx] for u in unpacked]
    return out


GATHER_AHEAD = 3


def _train_step(x, target, w, m, v):
    full = _gather_small_weights(w)
    gathers, exchanges = {}, {}
    last_start = [jnp.zeros((8, 128), F32)]

    def start_gather(si, after):
        cn, rn, j = _stage_weights(si)
        wc, wr = w[cn][j].astype(BF16), w[rn][j].astype(BF16)
        if cn == "ssm_w_in":
            wc = jnp.pad(wc, ((0, 0), (0, SSM_SLAB_PAD - SSM_SLAB)))
        shapes = [(N_DEV,) + wc.shape, (N_DEV * wr.shape[0], wr.shape[1])]
        gathers[si], last_start[0] = _copies_start([wc, wr], shapes, scatter=False, after=after,
                                                   name=f"gather_start_{si}")

    for si in range(GATHER_AHEAD):
        start_gather(si, [last_start[0]])

    def weights_of(si, x_act):
        lc, lr = _copies_wait(gathers.pop(si), scatter=False, after=[x_act, last_start[0]], name=f"gather_wait_{si}")
        behind = None
        if si + GATHER_AHEAD < N_STAGES:
            start_gather(si + GATHER_AHEAD, [last_start[0], lc])
            behind = last_start[0]
        return lc, lr, behind

    def emit(si, dwc, dwr):
        shapes = [dwc.shape, (N_DEV, dwr.shape[0] // N_DEV, dwr.shape[1])]
        exchanges[si], token = _copies_start([dwc, dwr], shapes, scatter=True, after=[], name=f"exchange_start_{si}")
        return token

    loss, dx, grads = _forward_backward(x, target, weights_of, full, emit)

    out = _update_small(grads, w, m, v)
    recvs = {n: [None] * w[n].shape[0] for n in BIG_COLS + BIG_ROWS}
    behind = [dx, out[SMALL[0]][1]]
    for si in reversed(range(N_STAGES)):
        cn, rn, j = _stage_weights(si)
        if si == 0:
            out.update(_update_big([n for n in BIG_COLS + BIG_ROWS if not n.startswith("cv_")], recvs, w, m, v))
            behind = [out["ffn_w_up"][1], out["ffn_w_down"][1], out["ssm_w_in"][1], out["ssm_w_out"][1]]
        recvs[cn][j], recvs[rn][j] = _copies_wait(exchanges.pop(si), scatter=True, after=behind,
                                                  name=f"exchange_wait_{si}")
        behind = [recvs[cn][j]]
    out.update(_update_big(["cv_w_in", "cv_w_out"], recvs, w, m, v))
    return lax.psum(loss[0, 0], AXES), dx, out


def kernel(x, norm_mix_g, norm_ffn_g, norm_final_g, cv_w_in, cv_b_in, cv_w_dw, cv_b_dw, cv_ln_g, cv_ln_b, cv_w_out, cv_b_out, ssm_w_in, ssm_w_conv, ssm_b_conv, ssm_dt_bias, ssm_a_log, ssm_d, ssm_norm_g, ssm_w_out, ffn_w_up, ffn_w_dw, ffn_b_dw, ffn_w_down, loss_target, m_norm_mix_g, m_norm_ffn_g, m_norm_final_g, m_cv_w_in, m_cv_b_in, m_cv_w_dw, m_cv_b_dw, m_cv_ln_g, m_cv_ln_b, m_cv_w_out, m_cv_b_out, m_ssm_w_in, m_ssm_w_conv, m_ssm_b_conv, m_ssm_dt_bias, m_ssm_a_log, m_ssm_d, m_ssm_norm_g, m_ssm_w_out, m_ffn_w_up, m_ffn_w_dw, m_ffn_b_dw, m_ffn_w_down, v_norm_mix_g, v_norm_ffn_g, v_norm_final_g, v_cv_w_in, v_cv_b_in, v_cv_w_dw, v_cv_b_dw, v_cv_ln_g, v_cv_ln_b, v_cv_w_out, v_cv_b_out, v_ssm_w_in, v_ssm_w_conv, v_ssm_b_conv, v_ssm_dt_bias, v_ssm_a_log, v_ssm_d, v_ssm_norm_g, v_ssm_w_out, v_ffn_w_up, v_ffn_w_dw, v_ffn_b_dw, v_ffn_w_down):
    args = locals()
    w = {n: args[n] for n in WEIGHTS}
    m = {n: args["m_" + n] for n in WEIGHTS}
    v = {n: args["v_" + n] for n in WEIGHTS}
    loss, dx, out = _train_step(x[0], loss_target[0], w, m, v)
    return (loss, dx[None], *[out[n][0] for n in WEIGHTS], *[out[n][1] for n in WEIGHTS],
            *[out[n][2] for n in WEIGHTS], *[out[n][3] for n in WEIGHTS])
```
